```python
import jax
import jax.numpy as jnp
from jax import lax
import numpy as np

D_MODEL = 2048
BATCH = 2
SEQ = 4096
DEPTH = 4
DEC_BATCH = 8
DEC_SEQ = 1
PAST_LEN = 16384
PAGE_SIZE = 128

N_EVEN = (DEPTH + 1) // 2
N_ODD = DEPTH // 2
EPS = 1e-6
A_WIDTH = D_MODEL // 2
A_HEADS = 8
A_HEAD_DIM = A_WIDTH // A_HEADS
GMLP_CHUNK = 128
B_WIDTH = D_MODEL - A_WIDTH
POOL_WINDOWS = (2, 4, 8, 16)
N_POOL_GROUPS = len(POOL_WINDOWS)
POOL_GROUP = B_WIDTH // N_POOL_GROUPS
POOL_PAD = max(POOL_WINDOWS) - 1
HEAD_DIM = 128
C_HEADS = D_MODEL // HEAD_DIM
C_KV_HEADS = 4
C_REP = C_HEADS // C_KV_HEADS
C_WIDTH = C_HEADS * HEAD_DIM
KV_WIDTH = C_KV_HEADS * HEAD_DIM
CMP_BLOCK = 64
N_SEL = 16
WINDOW = 512
Q_BLOCK = 64
SEL_FORCE = 1e4
NEG_INF = -1e30
ROPE_THETA = 500000.0
ROT_DIM = HEAD_DIM // 4
EVEN_SPLITS = (A_WIDTH, A_WIDTH, A_WIDTH, B_WIDTH, B_WIDTH)
ODD_SPLITS = (C_WIDTH,) + (KV_WIDTH,) * 6 + (3 * C_HEADS, C_WIDTH)
EVEN_IN = sum(EVEN_SPLITS)
ODD_IN = sum(ODD_SPLITS)

kernel_name = 'hybrid_gmlp_pool_nsa_decode_step'


def split_cols(y, sizes):
    idx = np.cumsum(sizes)[:-1].tolist()
    return jnp.split(y, idx, axis=-1)


def rmsnorm(x, g):
    x32 = x.astype(jnp.float32)
    y = x32 * lax.rsqrt(jnp.mean(x32 * x32, axis=-1, keepdims=True) + EPS)
    return (y * g.astype(jnp.float32)).astype(x.dtype)


def rope(x, pos):
    half = ROT_DIM // 2
    inv_freq = ROPE_THETA ** (-jnp.arange(half, dtype=jnp.float32) * (2.0 / ROT_DIM))
    ang = pos.astype(jnp.float32)[:, None] * inv_freq[None, :]
    cos = jnp.cos(ang)[:, None, :]
    sin = jnp.sin(ang)[:, None, :]
    xf = x.astype(jnp.float32)
    x1, x2, rest = xf[..., :half], xf[..., half:ROT_DIM], xf[..., ROT_DIM:]
    return jnp.concatenate([x1 * cos - x2 * sin, x1 * sin + x2 * cos, rest], axis=-1).astype(x.dtype)


def masked_softmax(s, mask):
    p = jax.nn.softmax(jnp.where(mask, s, NEG_INF), axis=-1)
    return p * mask


def gmlp_mix(u, v, w_s, b_s):
    b, t, _ = v.shape
    l = w_s.shape[-1]
    vh = v.reshape(b, t // l, l, A_HEADS, A_HEAD_DIM)
    causal = jnp.tril(jnp.ones((l, l), dtype=bool))
    w = jnp.where(causal, w_s, 0.0)
    s = jnp.einsum('hij,bcjhd->bcihd', w, vh) + b_s.T[None, None, :, :, None]
    return u * s.reshape(b, t, A_WIDTH).astype(u.dtype)


def pool_mix(p, prefix, pos, w_pool, scale):
    b, t, _ = p.shape
    ext = jnp.concatenate([prefix.astype(p.dtype), p], axis=1).astype(jnp.float32)
    cs = jnp.concatenate([jnp.zeros((b, 1, B_WIDTH), jnp.float32), jnp.cumsum(ext, axis=1)], axis=1)
    cur = ext[:, POOL_PAD:]
    groups = []
    for g, w in enumerate(POOL_WINDOWS):
        sl = slice(g * POOL_GROUP, (g + 1) * POOL_GROUP)
        win_sum = cs[:, POOL_PAD + 1:POOL_PAD + 1 + t, sl] - cs[:, POOL_PAD + 1 - w:POOL_PAD + 1 - w + t, sl]
        cnt = jnp.minimum(pos + 1, w).astype(jnp.float32)[None, :, None]
        groups.append(win_sum / cnt - cur[:, :, sl])
    d = jnp.stack(groups, axis=2)
    y = jnp.einsum('btgc,gce->btge', d, w_pool).reshape(b, t, B_WIDTH)
    return (y * scale).astype(p.dtype)


def even_layer(x, pool_prefix, pos, norm_g, w_in, v_gain, w_s, b_s, w_pool, pool_scale, w_out):
    b, t, _ = x.shape
    l = min(t, GMLP_CHUNK)
    h = rmsnorm(x, norm_g)
    u, v, z_a, p, z_b = split_cols(h @ w_in, EVEN_SPLITS)
    v = rmsnorm(v.reshape(b, t, A_HEADS, A_HEAD_DIM), v_gain.reshape(A_HEADS, A_HEAD_DIM)).reshape(b, t, A_WIDTH)
    a = gmlp_mix(u, v, w_s[:, :l, :l], b_s[:, :l]) * jax.nn.silu(z_a)
    m = pool_mix(p, pool_prefix, pos, w_pool, pool_scale) * jax.nn.silu(z_b)
    y = x + (jnp.concatenate([a, m], axis=-1) @ w_out).astype(x.dtype)
    new_pool = jnp.concatenate([pool_prefix.astype(p.dtype), p], axis=1)[:, -POOL_PAD:]
    return y, v, new_pool


def odd_project(x, pos, norm_g, w_in, q_gain, k_gain):
    b, t, _ = x.shape
    h = rmsnorm(x, norm_g)
    q, kc, vc, ks, vs, kw, vw, g, z = split_cols(h @ w_in, ODD_SPLITS)
    q = rope(rmsnorm(q.reshape(b, t, C_HEADS, HEAD_DIM), q_gain), pos)
    q = q.reshape(b, t, C_KV_HEADS, C_REP, HEAD_DIM)
    heads = lambda a: a.reshape(b, t, C_KV_HEADS, HEAD_DIM)
    key = lambda a, i: rope(rmsnorm(heads(a), k_gain[i]), pos)
    kv_cmp = jnp.stack([key(kc, 0), heads(vc)], axis=2)
    kv_sel = jnp.stack([key(ks, 1), heads(vs)], axis=2)
    kv_win = jnp.stack([key(kw, 2), heads(vw)], axis=2)
    gates = jax.nn.sigmoid(g.astype(jnp.float32)).reshape(b, t, C_KV_HEADS, C_REP, 3)
    return q, kv_cmp, kv_sel, kv_win, gates, z


def block_views(kv_cmp, kv_sel):
    b, length = kv_cmp.shape[:2]
    nb = -(-length // CMP_BLOCK)
    pad = nb * CMP_BLOCK - length
    def blocks(a):
        a = jnp.pad(a, ((0, 0), (0, pad), (0, 0), (0, 0), (0, 0)))
        return a.reshape(b, nb, CMP_BLOCK, 2, C_KV_HEADS, HEAD_DIM)
    means = blocks(kv_cmp).astype(jnp.float32).mean(axis=2)
    sel = blocks(kv_sel).transpose(0, 3, 4, 1, 2, 5).reshape(b, 2, C_KV_HEADS, nb, CMP_BLOCK * HEAD_DIM)
    return means[:, :, 0], means[:, :, 1], sel[:, 0], sel[:, 1]


def nsa_core(q, q_pos, kc_blk, vc_blk, ks_g, vs_g, kw, vw, w_pos, gates):
    b, t = q.shape[:2]
    nb = kc_blk.shape[1]
    scale = HEAD_DIM ** -0.5
    blk = jnp.arange(nb)
    s_c = jnp.einsum('btgrd,bngd->bgrtn', q, kc_blk, preferred_element_type=jnp.float32) * scale
    cmp_ok = ((blk + 1) * CMP_BLOCK - 1)[None, :] <= q_pos[:, None]
    p_c = masked_softmax(s_c, cmp_ok)
    o_c = jnp.einsum('bgrtn,bngd->btgrd', p_c, vc_blk)
    cur = q_pos // CMP_BLOCK
    forced = (blk[None, :] == 0) | (blk[None, :] == cur[:, None]) | (blk[None, :] == cur[:, None] - 1)
    imp = jnp.where(blk[None, :] <= cur[:, None], jnp.where(forced, SEL_FORCE, p_c.sum(axis=2)), -1.0)
    n_sel = min(N_SEL, nb)
    top_v, top_i = lax.top_k(imp, n_sel)
    gather = jax.vmap(jax.vmap(lambda a, i: a[i]))
    k_sel = gather(ks_g, top_i).reshape(b, C_KV_HEADS, t, n_sel * CMP_BLOCK, HEAD_DIM)
    v_sel = gather(vs_g, top_i).reshape(b, C_KV_HEADS, t, n_sel * CMP_BLOCK, HEAD_DIM)
    tok = top_i[..., None] * CMP_BLOCK + jnp.arange(CMP_BLOCK)
    sel_ok = (top_v >= 0)[..., None] & (tok <= q_pos[None, None, :, None, None])
    sel_ok = sel_ok.reshape(b, C_KV_HEADS, 1, t, n_sel * CMP_BLOCK)
    s_s = jnp.einsum('btgrd,bgtmd->bgrtm', q, k_sel, preferred_element_type=jnp.float32) * scale
    p_s = masked_softmax(s_s, sel_ok)
    o_s = jnp.einsum('bgrtm,bgtmd->btgrd', p_s, v_sel)
    rel = q_pos[:, None] - w_pos[None, :]
    w_ok = (w_pos[None, :] >= 0) & (rel >= 0) & (rel < WINDOW)
    s_w = jnp.einsum('btgrd,blgd->bgrtl', q, kw, preferred_element_type=jnp.float32) * scale
    p_w = masked_softmax(s_w, w_ok)
    o_w = jnp.einsum('bgrtl,blgd->btgrd', p_w, vw)
    return gates[..., 0:1] * o_c + gates[..., 1:2] * o_s + gates[..., 2:3] * o_w


def nsa_prompt(q, gates, kv_cmp, kv_sel, kv_win):
    b, s = q.shape[:2]
    kc_blk, vc_blk, ks_g, vs_g = block_views(kv_cmp, kv_sel)
    win_pad = jnp.pad(kv_win, ((0, 0), (WINDOW, 0), (0, 0), (0, 0), (0, 0)))
    def one_block(i):
        start = i * Q_BLOCK
        qb = lax.dynamic_slice_in_dim(q, start, Q_BLOCK, axis=1)
        gb = lax.dynamic_slice_in_dim(gates, start, Q_BLOCK, axis=1)
        wb = lax.dynamic_slice_in_dim(win_pad, start, WINDOW + Q_BLOCK, axis=1)
        q_pos = start + jnp.arange(Q_BLOCK)
        w_pos = start - WINDOW + jnp.arange(WINDOW + Q_BLOCK)
        return nsa_core(qb, q_pos, kc_blk, vc_blk, ks_g, vs_g, wb[:, :, 0], wb[:, :, 1], w_pos, gb)
    o = lax.map(one_block, jnp.arange(s // Q_BLOCK))
    return jnp.moveaxis(o, 0, 1).reshape(b, s, C_KV_HEADS, C_REP, HEAD_DIM)


def nsa_sample(q, gates, kv_cmp, kv_sel, kv_win, pool_cmp, pool_sel, win_buf, page_table):
    db, t = q.shape[:2]
    past_len = page_table.shape[1] * PAGE_SIZE
    def past(pool):
        return pool[page_table].reshape(db, past_len, 2, C_KV_HEADS, HEAD_DIM).astype(kv_cmp.dtype)
    full_cmp = jnp.concatenate([past(pool_cmp), kv_cmp], axis=1)
    full_sel = jnp.concatenate([past(pool_sel), kv_sel], axis=1)
    kc_blk, vc_blk, ks_g, vs_g = block_views(full_cmp, full_sel)
    n_buf = win_buf.shape[1]
    win = jnp.concatenate([win_buf.astype(kv_win.dtype), kv_win], axis=1)
    q_pos = past_len + jnp.arange(t)
    w_pos = past_len - n_buf + jnp.arange(n_buf + t)
    o = nsa_core(q, q_pos, kc_blk, vc_blk, ks_g, vs_g, win[:, :, 0], win[:, :, 1], w_pos, gates)
    return o, win[:, -n_buf:]


def odd_output(x, o, z, w_out):
    b, t = x.shape[:2]
    gated = o.reshape(b, t, C_WIDTH).astype(x.dtype) * jax.nn.silu(z)
    return x + (gated @ w_out).astype(x.dtype)


def setup_inputs(seed: int = 0) -> dict:
    key = jax.random.key(seed)
    ks = jax.random.split(key, 24)
    f32 = jnp.float32
    nrm = lambda k, shape, sc: jax.random.normal(k, shape, f32) * sc
    n_pages = PAST_LEN // PAGE_SIZE
    n_phys = (5 * DEC_BATCH * n_pages + 3) // 4
    win_buf = min(WINDOW, PAST_LEN)
    page_table = jax.random.permutation(ks[6], n_phys)[:DEC_BATCH * n_pages]
    page_table = page_table.reshape(DEC_BATCH, n_pages).astype(jnp.int32)
    return {
        'x_prompt': nrm(ks[0], (BATCH, SEQ, D_MODEL), 1.0),
        'x_sample': nrm(ks[1], (DEC_BATCH, DEC_SEQ, D_MODEL), 1.0),
        'cache_cmp_kv': nrm(ks[2], (N_ODD, n_phys, PAGE_SIZE, 2, C_KV_HEADS, HEAD_DIM), 1.0),
        'cache_sel_kv': nrm(ks[3], (N_ODD, n_phys, PAGE_SIZE, 2, C_KV_HEADS, HEAD_DIM), 1.0),
        'state_win_kv': nrm(ks[4], (N_ODD, DEC_BATCH, win_buf, 2, C_KV_HEADS, HEAD_DIM), 1.0),
        'state_pool': nrm(ks[5], (N_EVEN, DEC_BATCH, POOL_PAD, B_WIDTH), 1.0),
        'page_table': page_table,
        'norm_even': 1.0 + nrm(ks[7], (N_EVEN, D_MODEL), 0.02),
        'w_in_even': nrm(ks[8], (N_EVEN, D_MODEL, EVEN_IN), D_MODEL ** -0.5),
        'v_norm': 1.0 + nrm(ks[9], (N_EVEN, A_WIDTH), 0.02),
        'w_spatial': nrm(ks[10], (N_EVEN, A_HEADS, GMLP_CHUNK, GMLP_CHUNK), GMLP_CHUNK ** -0.5),
        'b_spatial': 1.0 + nrm(ks[11], (N_EVEN, A_HEADS, GMLP_CHUNK), 0.02),
        'w_pool': nrm(ks[12], (N_EVEN, N_POOL_GROUPS, POOL_GROUP, POOL_GROUP), POOL_GROUP ** -0.5),
        'pool_scale': 1.0 + nrm(ks[13], (N_EVEN, B_WIDTH), 0.02),
        'w_out_even': nrm(ks[14], (N_EVEN, A_WIDTH + B_WIDTH, D_MODEL), (A_WIDTH + B_WIDTH) ** -0.5),
        'norm_odd': 1.0 + nrm(ks[15], (N_ODD, D_MODEL), 0.02),
        'w_in_odd': nrm(ks[16], (N_ODD, D_MODEL, ODD_IN), D_MODEL ** -0.5),
        'q_norm': 1.0 + nrm(ks[17], (N_ODD, HEAD_DIM), 0.02),
        'k_norm': 1.0 + nrm(ks[18], (N_ODD, 3, HEAD_DIM), 0.02),
        'w_out_odd': nrm(ks[19], (N_ODD, C_WIDTH, D_MODEL), C_WIDTH ** -0.5),
    }


def reference(x_prompt, x_sample, cache_cmp_kv, cache_sel_kv, state_win_kv, state_pool, page_table,
              norm_even, w_in_even, v_norm, w_spatial, b_spatial, w_pool, pool_scale, w_out_even,
              norm_odd, w_in_odd, q_norm, k_norm, w_out_odd):
    bsz, seq, _ = x_prompt.shape
    dec_t = x_sample.shape[1]
    past_len = page_table.shape[1] * PAGE_SIZE
    pos_p = jnp.arange(seq, dtype=jnp.int32)
    pos_s = past_len + jnp.arange(dec_t, dtype=jnp.int32)
    win_keep = min(WINDOW, seq)
    xp, xs = x_prompt, x_sample
    cmp_p, cmp_s, sel_p, sel_s, win_p, win_s = [], [], [], [], [], []
    pool_p, pool_s, gv_s = [], [], []
    for layer in range(DEPTH):
        li = layer // 2
        if layer % 2 == 0:
            ew = (norm_even[li], w_in_even[li], v_norm[li], w_spatial[li], b_spatial[li],
                  w_pool[li], pool_scale[li], w_out_even[li])
            zero_prefix = jnp.zeros((bsz, POOL_PAD, B_WIDTH), xp.dtype)
            xp, _, pp = even_layer(xp, zero_prefix, pos_p, *ew)
            xs, v_rows, ps = even_layer(xs, state_pool[li], pos_s, *ew)
            pool_p.append(pp)
            pool_s.append(ps)
            gv_s.append(v_rows)
        else:
            ow = (norm_odd[li], w_in_odd[li], q_norm[li], k_norm[li])
            q, kvc, kvs, kvw, gates, z = odd_project(xp, pos_p, *ow)
            xp = odd_output(xp, nsa_prompt(q, gates, kvc, kvs, kvw), z, w_out_odd[li])
            cmp_p.append(kvc)
            sel_p.append(kvs)
            win_p.append(kvw[:, seq - win_keep:])
            q, kvc, kvs, kvw, gates, z = odd_project(xs, pos_s, *ow)
            o, win_new = nsa_sample(q, gates, kvc, kvs, kvw, cache_cmp_kv[li], cache_sel_kv[li],
                                    state_win_kv[li], page_table)
            xs = odd_output(xs, o, z, w_out_odd[li])
            cmp_s.append(kvc)
            sel_s.append(kvs)
            win_s.append(win_new)
    return (xp, xs, jnp.stack(cmp_p), jnp.stack(cmp_s), jnp.stack(sel_p), jnp.stack(sel_s),
            jnp.stack(win_p), jnp.stack(win_s), jnp.stack(pool_p), jnp.stack(pool_s), jnp.stack(gv_s))
```

```python
import functools

import jax
import jax.numpy as jnp
import numpy as np
from jax import lax
from jax.experimental import pallas as pl
from jax.experimental.pallas import tpu as pltpu

F32 = jnp.float32
BF16 = jnp.bfloat16
I32 = jnp.int32

EPS = 1e-6
PAGE_SIZE = 128
A_HEADS = 8
GMLP_CHUNK = 128
POOL_WINDOWS = (2, 4, 8, 16)
POOL_PAD = max(POOL_WINDOWS) - 1
HEAD_DIM = 128
C_KV_HEADS = 4
CMP_BLOCK = 64
N_SEL = 16
WINDOW = 512
SEL_FORCE = 1e4
NEG_INF = -1e30
ROPE_THETA = 500000.0
ROT_DIM = HEAD_DIM // 4
ATTN_SCALE = HEAD_DIM ** -0.5

LANES = 128
MIB = 1024 * 1024


def _cparams(vmem_mib, semantics=None):
    return pltpu.CompilerParams(vmem_limit_bytes=int(vmem_mib * MIB), dimension_semantics=semantics)


def _silu(x):
    return x * (1.0 / (1.0 + jnp.exp(-x)))


def _sigmoid(x):
    return 1.0 / (1.0 + jnp.exp(-x))


def _dot_nt(a, b):
    return lax.dot_general(a, b, (((1,), (1,)), ((), ())), preferred_element_type=F32)


def _norm_mm_kernel(x_ref, g_ref, w_ref, o_ref):
    x = x_ref[...]
    ms = jnp.mean(x * x, axis=-1, keepdims=True)
    h = (x * lax.rsqrt(ms + EPS) * g_ref[...]).astype(BF16)
    o_ref[...] = jnp.dot(h, w_ref[...], preferred_element_type=F32)


def _mm_res_kernel(a_ref, w_ref, r_ref, o_ref):
    o_ref[...] = r_ref[...] + jnp.dot(a_ref[...], w_ref[...], preferred_element_type=F32)


def _row_tile(m, want):
    return want if m % want == 0 else m


def _norm_matmul(x, g, w, *, tn, tm=512):
    m, k = x.shape
    n = w.shape[1]
    tm = _row_tile(m, tm)
    assert n % tn == 0
    vmem = 2 * (tm * k * 4 + k * tn * 2 + tm * tn * 4) / MIB + 8
    return pl.pallas_call(
        _norm_mm_kernel,
        grid=(n // tn, m // tm),
        in_specs=[
            pl.BlockSpec((tm, k), lambda j, i: (i, 0)),
            pl.BlockSpec((1, k), lambda j, i: (0, 0)),
            pl.BlockSpec((k, tn), lambda j, i: (0, j)),
        ],
        out_specs=pl.BlockSpec((tm, tn), lambda j, i: (i, j)),
        out_shape=jax.ShapeDtypeStruct((m, n), F32),
        compiler_params=_cparams(vmem),
        name="norm_matmul",
    )(x, g, w)


def _matmul_residual(a, w, res, *, tm=512):
    m, k = a.shape
    n = w.shape[1]
    tm = _row_tile(m, tm)
    vmem = 2 * (tm * k * 2 + k * n * 2 + 2 * tm * n * 4) / MIB + 8
    return pl.pallas_call(
        _mm_res_kernel,
        grid=(m // tm,),
        in_specs=[
            pl.BlockSpec((tm, k), lambda i: (i, 0)),
            pl.BlockSpec((k, n), lambda i: (0, 0)),
            pl.BlockSpec((tm, n), lambda i: (i, 0)),
        ],
        out_specs=pl.BlockSpec((tm, n), lambda i: (i, 0)),
        out_shape=jax.ShapeDtypeStruct((m, n), F32),
        compiler_params=_cparams(vmem),
        name="matmul_residual",
    )(a, w, res)


def _even_mix_kernel(proj_ref, vg_ref, ws_ref, bst_ref, wp_ref, ps_ref, act_ref, pool_ref, ext_ref,
                     *, tm, aw, bw):
    c = pl.program_id(1)
    n_c = pl.num_programs(1)
    hd = aw // A_HEADS
    pg = bw // len(POOL_WINDOWS)

    row = lax.broadcasted_iota(I32, (tm, tm), 0)
    col = lax.broadcasted_iota(I32, (tm, tm), 1)
    causal = row >= col

    for h in range(A_HEADS):
        sl = slice(h * hd, (h + 1) * hd)
        u = proj_ref[0, :, sl]
        v = proj_ref[0, :, aw + h * hd:aw + (h + 1) * hd]
        za = proj_ref[0, :, 2 * aw + h * hd:2 * aw + (h + 1) * hd]
        ms = jnp.mean(v * v, axis=-1, keepdims=True)
        vn = v * lax.rsqrt(ms + EPS) * vg_ref[:, sl]
        w = jnp.where(causal, ws_ref[h], 0.0).astype(BF16)
        s = jnp.dot(w, vn.astype(BF16), preferred_element_type=F32) + bst_ref[:, h:h + 1]
        act_ref[0, :, sl] = ((u * s) * _silu(za)).astype(BF16)

    p_off = 3 * aw
    zb_off = 3 * aw + bw

    @pl.when(c == 0)
    def _():
        ext_ref[0:16, :] = jnp.zeros((16, bw), F32)

    ext_ref[16:16 + tm, :] = proj_ref[0, :, p_off:p_off + bw]
    pos = c * tm + lax.broadcasted_iota(I32, (tm, 1), 0)
    for g, wnd in enumerate(POOL_WINDOWS):
        sl = slice(g * pg, (g + 1) * pg)
        cur = ext_ref[16:16 + tm, sl]
        acc = cur
        for k in range(1, wnd):
            acc = acc + ext_ref[16 - k:16 - k + tm, sl]
        cnt = jnp.minimum(pos + 1, wnd).astype(F32)
        d = acc / cnt - cur
        y = jnp.dot(d.astype(BF16), wp_ref[g], preferred_element_type=F32)
        zb = proj_ref[0, :, zb_off + g * pg:zb_off + (g + 1) * pg]
        act_ref[0, :, aw + g * pg:aw + (g + 1) * pg] = ((y * ps_ref[:, sl]) * _silu(zb)).astype(BF16)

    tail = ext_ref[tm:tm + 16, :]
    ext_ref[0:16, :] = tail

    @pl.when(c == n_c - 1)
    def _():
        pool_ref[0] = tail


def _even_mix(proj3, v_gain, w_s, b_st, w_pool, pool_scale):
    b, s, _ = proj3.shape
    aw = v_gain.shape[1]
    bw = pool_scale.shape[1]
    tm = GMLP_CHUNK
    assert s % tm == 0 and s >= 16
    n_in = 3 * aw + 2 * bw
    kern = functools.partial(_even_mix_kernel, tm=tm, aw=aw, bw=bw)
    return pl.pallas_call(
        kern,
        grid=(b, s // tm),
        in_specs=[
            pl.BlockSpec((1, tm, n_in), lambda i, c: (i, c, 0)),
            pl.BlockSpec((1, aw), lambda i, c: (0, 0)),
            pl.BlockSpec(w_s.shape, lambda i, c: (0, 0, 0)),
            pl.BlockSpec(b_st.shape, lambda i, c: (0, 0)),
            pl.BlockSpec(w_pool.shape, lambda i, c: (0, 0, 0)),
            pl.BlockSpec((1, bw), lambda i, c: (0, 0)),
        ],
        out_specs=[
            pl.BlockSpec((1, tm, aw + bw), lambda i, c: (i, c, 0)),
            pl.BlockSpec((1, 16, bw), lambda i, c: (i, 0, 0)),
        ],
        out_shape=[
            jax.ShapeDtypeStruct((b, s, aw + bw), BF16),
            jax.ShapeDtypeStruct((b, 16, bw), F32),
        ],
        scratch_shapes=[pltpu.VMEM((16 + tm, bw), F32)],
        compiler_params=_cparams(32, ("arbitrary", "arbitrary")),
        name="even_mix",
    )(proj3, v_gain, w_s, b_st, w_pool, pool_scale)


def _even_mix_sample_kernel(proj_ref, vg_ref, w00_ref, b0_ref, wp_ref, ps_ref, st_ref,
                            act_ref, vn_ref, newst_ref, *, aw, bw, pos):
    hd = aw // A_HEADS
    pg = bw // len(POOL_WINDOWS)
    for h in range(A_HEADS):
        sl = slice(h * hd, (h + 1) * hd)
        u = proj_ref[:, sl]
        v = proj_ref[:, aw + h * hd:aw + (h + 1) * hd]
        za = proj_ref[:, 2 * aw + h * hd:2 * aw + (h + 1) * hd]
        ms = jnp.mean(v * v, axis=-1, keepdims=True)
        vn = v * lax.rsqrt(ms + EPS) * vg_ref[:, sl]
        vn_ref[:, sl] = vn
        s = w00_ref[:, sl] * vn + b0_ref[:, sl]
        act_ref[:, sl] = ((u * s) * _silu(za)).astype(BF16)

    p = proj_ref[:, 3 * aw:3 * aw + bw]
    for g, wnd in enumerate(POOL_WINDOWS):
        sl = slice(g * pg, (g + 1) * pg)
        cur = p[:, sl]
        acc = cur
        for k in range(1, wnd):
            acc = acc + st_ref[POOL_PAD - k, :, sl]
        cnt = float(min(pos + 1, wnd))
        d = acc / cnt - cur
        y = jnp.dot(d.astype(BF16), wp_ref[g], preferred_element_type=F32)
        zb = proj_ref[:, 3 * aw + bw + g * pg:3 * aw + bw + (g + 1) * pg]
        act_ref[:, aw + g * pg:aw + (g + 1) * pg] = ((y * ps_ref[:, sl]) * _silu(zb)).astype(BF16)

    for k in range(POOL_PAD - 1):
        newst_ref[k] = st_ref[k + 1]
    newst_ref[POOL_PAD - 1] = p


def _even_mix_sample(proj, v_gain, w00, b0, w_pool, pool_scale, state_t, pos):
    db = proj.shape[0]
    aw = v_gain.shape[1]
    bw = pool_scale.shape[1]
    kern = functools.partial(_even_mix_sample_kernel, aw=aw, bw=bw, pos=pos)
    return pl.pallas_call(
        kern,
        out_shape=[
            jax.ShapeDtypeStruct((db, aw + bw), BF16),
            jax.ShapeDtypeStruct((db, aw), F32),
            jax.ShapeDtypeStruct((POOL_PAD, db, bw), F32),
        ],
        name="even_mix_sample",
    )(proj, v_gain, w00, b0, w_pool, pool_scale, state_t)


def _odd_post_kernel(proj_ref, cos_ref, sin_ref, qg_ref, kg_ref, *out_refs, tm, cw, kvw, with_means):
    if with_means:
        q_out, cmp_out, sel_out, win_out, selb_out, winb_out, means_out = out_refs
    else:
        q_out, cmp_out, sel_out, win_out, selb_out, winb_out = out_refs
    cosf = cos_ref[...]
    sinf = sin_ref[...]
    half = ROT_DIM // 2
    first = lax.broadcasted_iota(I32, (tm, HEAD_DIM), 1) < half

    def norm_rope(x, gain):
        ms = jnp.mean(x * x, axis=-1, keepdims=True)
        y = x * lax.rsqrt(ms + EPS) * gain
        rot = jnp.where(first, pltpu.roll(y, HEAD_DIM - half, 1), pltpu.roll(y, half, 1))
        return y * cosf + rot * sinf

    qg = qg_ref[...]
    for h in range(cw // HEAD_DIM):
        sl = slice(h * HEAD_DIM, (h + 1) * HEAD_DIM)
        q_out[:, sl] = (norm_rope(proj_ref[:, sl], qg) * ATTN_SCALE).astype(BF16)

    outs = ((cmp_out, None), (sel_out, selb_out), (win_out, winb_out))
    for br, (o32, o16) in enumerate(outs):
        k_off = cw + br * 2 * kvw
        v_off = k_off + kvw
        kg = kg_ref[br:br + 1, :]
        for g in range(kvw // HEAD_DIM):
            kk = norm_rope(proj_ref[:, k_off + g * HEAD_DIM:k_off + (g + 1) * HEAD_DIM], kg)
            o32[:, g * HEAD_DIM:(g + 1) * HEAD_DIM] = kk
            if o16 is not None:
                o16[:, g * HEAD_DIM:(g + 1) * HEAD_DIM] = kk.astype(BF16)
        vv = proj_ref[:, v_off:v_off + kvw]
        o32[:, kvw:2 * kvw] = vv
        if o16 is not None:
            o16[:, kvw:2 * kvw] = vv.astype(BF16)

    if with_means:
        for j in range(tm // CMP_BLOCK):
            blk = cmp_out[j * CMP_BLOCK:(j + 1) * CMP_BLOCK, :]
            means_out[0, j:j + 1, :] = jnp.mean(blk, axis=0, keepdims=True)


def _odd_post(proj, cos_t, sin_t, q_gain, k_gain, *, cw, kvw, with_means, tm=256):
    m = proj.shape[0]
    tm = _row_tile(m, tm)
    n_pos_tiles = cos_t.shape[0] // tm
    n_used = cw + 6 * kvw
    kern = functools.partial(_odd_post_kernel, tm=tm, cw=cw, kvw=kvw, with_means=with_means)
    row_spec = lambda w: pl.BlockSpec((tm, w), lambda i: (i, 0))
    out_shape = [
        jax.ShapeDtypeStruct((m, cw), BF16),
        jax.ShapeDtypeStruct((m, 2 * kvw), F32),
        jax.ShapeDtypeStruct((m, 2 * kvw), F32),
        jax.ShapeDtypeStruct((m, 2 * kvw), F32),
        jax.ShapeDtypeStruct((m, 2 * kvw), BF16),
        jax.ShapeDtypeStruct((m, 2 * kvw), BF16),
    ]
    out_specs = [row_spec(cw)] + [row_spec(2 * kvw)] * 5
    if with_means:
        assert tm % CMP_BLOCK == 0
        out_shape.append(jax.ShapeDtypeStruct((m // tm, tm // CMP_BLOCK, 2 * kvw), F32))
        out_specs.append(pl.BlockSpec((1, tm // CMP_BLOCK, 2 * kvw), lambda i: (i, 0, 0)))
    return pl.pallas_call(
        kern,
        grid=(m // tm,),
        in_specs=[
            pl.BlockSpec((tm, n_used), lambda i: (i, 0)),
            pl.BlockSpec((tm, HEAD_DIM), lambda i: (i % n_pos_tiles, 0)),
            pl.BlockSpec((tm, HEAD_DIM), lambda i: (i % n_pos_tiles, 0)),
            pl.BlockSpec((1, HEAD_DIM), lambda i: (0, 0)),
            pl.BlockSpec((3, HEAD_DIM), lambda i: (0, 0)),
        ],
        out_specs=out_specs,
        out_shape=out_shape,
        compiler_params=_cparams(40),
        name="odd_post",
    )(proj, cos_t, sin_t, q_gain, k_gain)


def _nsa_prompt_kernel(q_ref, ks_ref, vs_ref, kw_ref, vw_ref, kc_ref, vc_ref, g_ref, z_ref, e_ref,
                       o_ref, *, tq, tk, seq, nb, rep, lw):
    i = pl.program_id(2)
    n_sel = min(N_SEL, nb)
    rows = rep * tq

    q = jnp.concatenate([q_ref[0, :, r * HEAD_DIM:(r + 1) * HEAD_DIM] for r in range(rep)], axis=0)
    t_col = i * tq + lax.broadcasted_iota(I32, (tq, 1), 0)
    t_row = i * tq + lax.broadcasted_iota(I32, (1, tq), 1)

    kc = kc_ref[0].astype(BF16)
    vc = vc_ref[0].astype(BF16)

    blk_row = lax.broadcasted_iota(I32, (1, nb), 1)
    s_c = _dot_nt(q, kc).reshape(rep, tq, nb)
    ok_c = ((blk_row + 1) * CMP_BLOCK - 1) <= t_col
    s_c = s_c + jnp.where(ok_c, 0.0, NEG_INF)[None]
    m_c = jnp.max(s_c, axis=-1, keepdims=True)
    e_c = jnp.exp(s_c - m_c)
    p_c = e_c / jnp.sum(e_c, axis=-1, keepdims=True) * jnp.where(ok_c, 1.0, 0.0)[None]
    o_c = jnp.dot(p_c.reshape(rows, nb).astype(BF16), vc, preferred_element_type=F32)

    blk_col = lax.broadcasted_iota(I32, (nb, 1), 0)
    s_t = _dot_nt(kc, q)
    ok_t = ((blk_col + 1) * CMP_BLOCK - 1) <= t_row
    bias_t = jnp.where(ok_t, 0.0, NEG_INF)
    okf_t = jnp.where(ok_t, 1.0, 0.0)
    imp = jnp.zeros((nb, tq), F32)
    for r in range(rep):
        s_r = s_t[:, r * tq:(r + 1) * tq] + bias_t
        e_r = jnp.exp(s_r - jnp.max(s_r, axis=0, keepdims=True))
        imp = imp + e_r / jnp.sum(e_r, axis=0, keepdims=True) * okf_t
    cur = t_row // CMP_BLOCK
    forced = (blk_col == 0) | (blk_col == cur) | (blk_col == cur - 1)
    imp = jnp.where(blk_col <= cur, jnp.where(forced, SEL_FORCE, imp), -1.0)
    cnt = jnp.zeros((nb, tq), F32)
    for j in range(nb):
        rj = imp[j:j + 1, :]
        beats = (rj > imp) | ((rj == imp) & (blk_col > j))
        cnt = cnt + jnp.where(beats, 1.0, 0.0)
    sel_t = jnp.where((cnt < n_sel) & (imp >= 0.0), 1.0, 0.0)
    nbp = e_ref.shape[0]
    if nbp > nb:
        sel_t = jnp.concatenate([sel_t, jnp.zeros((nbp - nb, tq), F32)], axis=0)
    sel = sel_t.T.astype(BF16)

    def sel_body(kt, carry):
        m_p, l_p, acc = carry
        k0 = pl.multiple_of(kt * tk, tk)
        k = ks_ref[0, pl.ds(k0, tk), :]
        v = vs_ref[0, pl.ds(k0, tk), :]
        s = _dot_nt(q, k).reshape(rep, tq, tk)
        picked = jnp.dot(sel, e_ref[:, pl.ds(k0, tk)], preferred_element_type=F32)
        kpos = k0 + lax.broadcasted_iota(I32, (1, tk), 1)
        ok = (picked > 0.5) & (kpos <= t_col)
        s = s + jnp.where(ok, 0.0, NEG_INF)[None]
        m_n = jnp.maximum(m_p, jnp.max(s, axis=-1, keepdims=True))
        alpha = jnp.exp(m_p - m_n)
        p = jnp.exp(s - m_n)
        l_n = alpha * l_p + jnp.sum(p, axis=-1, keepdims=True)
        pv = jnp.dot(p.reshape(rows, tk).astype(BF16), v, preferred_element_type=F32)
        return m_n, l_n, alpha * acc + pv.reshape(rep, tq, HEAD_DIM)

    n_kt = ((i + 1) * tq + tk - 1) // tk
    init = (jnp.full((rep, tq, 1), NEG_INF, F32), jnp.zeros((rep, tq, 1), F32),
            jnp.zeros((rep, tq, HEAD_DIM), F32))
    _, l_s, acc_s = lax.fori_loop(0, n_kt, sel_body, init)
    o_s = acc_s / l_s

    w0 = pl.multiple_of(jnp.clip(i * tq + tq - lw, 0, seq - lw), 8)
    kw = kw_ref[0, pl.ds(w0, lw), :]
    vw = vw_ref[0, pl.ds(w0, lw), :]
    s_w = _dot_nt(q, kw).reshape(rep, tq, lw)
    rel = t_col - (w0 + lax.broadcasted_iota(I32, (1, lw), 1))
    ok_w = (rel >= 0) & (rel < WINDOW)
    s_w = s_w + jnp.where(ok_w, 0.0, NEG_INF)[None]
    p_w = jnp.exp(s_w - jnp.max(s_w, axis=-1, keepdims=True))
    l_w = jnp.sum(p_w, axis=-1, keepdims=True)
    o_w = jnp.dot(p_w.reshape(rows, lw).astype(BF16), vw, preferred_element_type=F32)
    o_w = o_w.reshape(rep, tq, HEAD_DIM) / l_w

    gs = _sigmoid(g_ref[0])
    o_c = o_c.reshape(rep, tq, HEAD_DIM)
    for r in range(rep):
        o = gs[:, 3 * r:3 * r + 1] * o_c[r] + gs[:, 3 * r + 1:3 * r + 2] * o_s[r] \
            + gs[:, 3 * r + 2:3 * r + 3] * o_w[r]
        z = z_ref[0, :, r * HEAD_DIM:(r + 1) * HEAD_DIM]
        o_ref[0, :, r * HEAD_DIM:(r + 1) * HEAD_DIM] = (o * _silu(z)).astype(BF16)


def _nsa_prompt(q3, kvb_sel, kvb_win, means3, proj3, expand, *, cw, z_off, g_off, tq=128, tk=512):
    b, s, _ = q3.shape
    kvw = kvb_sel.shape[2] // 2
    groups = kvw // HEAD_DIM
    rep = cw // kvw
    nb = means3.shape[1]
    gw = rep * HEAD_DIM
    tk = min(tk, s)
    lw = min(WINDOW + tq, s)
    assert s % tq == 0 and s % tk == 0 and tq % CMP_BLOCK == 0 and nb * CMP_BLOCK == s
    assert z_off % gw == 0 and g_off % LANES == 0
    kern = functools.partial(_nsa_prompt_kernel, tq=tq, tk=tk, seq=s, nb=nb, rep=rep, lw=lw)
    kv_spec = lambda off: pl.BlockSpec((1, s, HEAD_DIM), lambda bi, g, i: (bi, 0, off + g))
    mean_spec = lambda off: pl.BlockSpec((1, nb, HEAD_DIM), lambda bi, g, i: (bi, 0, off + g))
    return pl.pallas_call(
        kern,
        grid=(b, groups, s // tq),
        in_specs=[
            pl.BlockSpec((1, tq, gw), lambda bi, g, i: (bi, i, g)),
            kv_spec(0), kv_spec(groups), kv_spec(0), kv_spec(groups),
            mean_spec(0), mean_spec(groups),
            pl.BlockSpec((1, tq, LANES), lambda bi, g, i: (bi, i, g_off // LANES + g)),
            pl.BlockSpec((1, tq, gw), lambda bi, g, i: (bi, i, z_off // gw + g)),
            pl.BlockSpec(expand.shape, lambda bi, g, i: (0, 0)),
        ],
        out_specs=pl.BlockSpec((1, tq, gw), lambda bi, g, i: (bi, i, g)),
        out_shape=jax.ShapeDtypeStruct((b, s, cw), BF16),
        compiler_params=_cparams(48),
        name="nsa_prompt",
    )(q3, kvb_sel, kvb_sel, kvb_win, kvb_win, means3, means3, proj3, proj3, expand)


def _page_means_kernel(pt_ref, *refs, pps):
    out_ref = refs[pps]
    bpp = PAGE_SIZE // CMP_BLOCK
    for r in range(pps):
        for h in range(bpp):
            blk = refs[r][h * CMP_BLOCK:(h + 1) * CMP_BLOCK, :]
            out_ref[0, r * bpp + h:r * bpp + h + 1, :] = jnp.mean(blk, axis=0, keepdims=True)


def _page_means(cache4, page_table, layer, *, pps=8):
    db, n_pages = page_table.shape
    w = cache4.shape[3]
    pps = pps if n_pages % pps == 0 else 1
    bpp = PAGE_SIZE // CMP_BLOCK

    def page_spec(r):
        return pl.BlockSpec((None, None, PAGE_SIZE, w),
                            lambda bi, j, pt: (layer, pt[bi, j * pps + r], 0, 0))

    grid_spec = pltpu.PrefetchScalarGridSpec(
        num_scalar_prefetch=1,
        grid=(db, n_pages // pps),
        in_specs=[page_spec(r) for r in range(pps)],
        out_specs=pl.BlockSpec((1, pps * bpp, w), lambda bi, j, pt: (bi, j, 0)),
    )
    return pl.pallas_call(
        functools.partial(_page_means_kernel, pps=pps),
        grid_spec=grid_spec,
        out_shape=jax.ShapeDtypeStruct((db, n_pages * bpp, w), F32),
        compiler_params=_cparams(24),
        name="page_means",
    )(page_table, *([cache4] * pps))


def _nsa_sample_select_kernel(q_ref, mean_ref, oc_ref, idx_ref, *, nbp, groups, rep, q_pos, n_past_sel):
    kvw = groups * HEAD_DIM
    q = q_ref[0]
    heads = groups * rep
    lane = lax.broadcasted_iota(I32, (1, nbp), 1)
    ri = lax.broadcasted_iota(I32, (nbp, nbp), 0)
    ci = lax.broadcasted_iota(I32, (nbp, nbp), 1)
    diag = ri == ci
    ok = ((lane + 1) * CMP_BLOCK - 1) <= q_pos
    cur = q_pos // CMP_BLOCK
    forced = (lane == 0) | (lane == cur) | (lane == cur - 1)
    hrow = lax.broadcasted_iota(I32, (heads, 1), 0) // rep
    out_lane = lax.broadcasted_iota(I32, (1, LANES), 1)
    o_c = jnp.zeros((heads, HEAD_DIM), F32)
    for g in range(groups):
        kc = mean_ref[0, :, g * HEAD_DIM:(g + 1) * HEAD_DIM].astype(BF16)
        vc = mean_ref[0, :, kvw + g * HEAD_DIM:kvw + (g + 1) * HEAD_DIM].astype(BF16)
        s = _dot_nt(q, kc) + jnp.where(ok, 0.0, NEG_INF)
        e = jnp.exp(s - jnp.max(s, axis=-1, keepdims=True))
        p = e / jnp.sum(e, axis=-1, keepdims=True) * jnp.where(ok, 1.0, 0.0)
        in_g = hrow == g
        o_c = o_c + jnp.where(in_g, jnp.dot(p.astype(BF16), vc, preferred_element_type=F32), 0.0)
        imp = jnp.sum(jnp.where(in_g, p, 0.0), axis=0, keepdims=True)
        imp = jnp.where(lane <= cur, jnp.where(forced, SEL_FORCE, imp), -1.0)
        imp_b = jnp.broadcast_to(imp, (nbp, nbp))
        imp_col = jnp.sum(jnp.where(diag, imp_b, 0.0), axis=1, keepdims=True)
        beats = (imp_col > imp_b) | ((imp_col == imp_b) & (ri < ci))
        cnt = jnp.sum(jnp.where(beats, 1.0, 0.0), axis=0, keepdims=True)
        sel = jnp.where((cnt < n_past_sel) & (imp >= 0.0), 1.0, 0.0)
        sel_col = jnp.sum(jnp.where(diag, jnp.broadcast_to(sel, (nbp, nbp)), 0.0), axis=1, keepdims=True)
        before = jnp.sum(jnp.where(ri < ci, sel_col, 0.0), axis=0, keepdims=True)
        row = jnp.zeros((1, LANES), F32)
        for k in range(n_past_sel):
            hit = (sel > 0.5) & (before == float(k))
            idx_k = jnp.sum(jnp.where(hit, lane.astype(F32), 0.0), axis=1, keepdims=True)
            row = jnp.where(out_lane == k, idx_k, row)
        idx_ref[0, g:g + 1, :] = row.astype(I32)
    oc_ref[0] = o_c


def _nsa_sample_select(q3, means, *, groups, rep, q_pos, n_past_sel):
    db, heads, _ = q3.shape
    nbp = means.shape[1]
    kern = functools.partial(_nsa_sample_select_kernel, nbp=nbp, groups=groups, rep=rep, q_pos=q_pos,
                             n_past_sel=n_past_sel)
    return pl.pallas_call(
        kern,
        grid=(db,),
        in_specs=[
            pl.BlockSpec((1, heads, HEAD_DIM), lambda bi: (bi, 0, 0)),
            pl.BlockSpec((1, nbp, means.shape[2]), lambda bi: (bi, 0, 0)),
        ],
        out_specs=[
            pl.BlockSpec((1, heads, HEAD_DIM), lambda bi: (bi, 0, 0)),
            pl.BlockSpec((1, groups, LANES), lambda bi: (bi, 0, 0)),
        ],
        out_shape=[
            jax.ShapeDtypeStruct((db, heads, HEAD_DIM), F32),
            jax.ShapeDtypeStruct((db, groups, LANES), I32),
        ],
        compiler_params=_cparams(24),
        name="nsa_sample_select",
    )(q3, means)


def _nsa_sample_attend_kernel(pt_ref, si_ref, q_ref, *refs, groups, rep, n_gather, n_buf):
    kv_refs = refs[:2 * groups]
    (newsel_ref, newwin_ref, oc_ref, win_ref, g_ref, z_ref,
     act_ref, winout_ref, m_ref, l_ref, acc_ref) = refs[2 * groups:]
    kvw = groups * HEAD_DIM
    heads = groups * rep
    k_id = pl.program_id(1)
    q = q_ref[0]
    qf = q.astype(F32)
    hgrp = lax.broadcasted_iota(I32, (heads, 1), 0) // rep

    def by_group(fn):
        out = None
        for g in range(groups):
            val = jnp.where(hgrp == g, fn(g), 0.0)
            out = val if out is None else out + val
        return out

    def new_token_scores(row_ref):
        return by_group(lambda g: jnp.sum(
            qf * row_ref[0, :, g * HEAD_DIM:(g + 1) * HEAD_DIM], axis=-1, keepdims=True))

    def new_token_values(row_ref):
        return by_group(lambda g: jnp.broadcast_to(
            row_ref[0, :, kvw + g * HEAD_DIM:kvw + (g + 1) * HEAD_DIM], (heads, HEAD_DIM)))

    @pl.when(k_id == 0)
    def _():
        m_ref[...] = jnp.full((heads, 1), NEG_INF, F32)
        l_ref[...] = jnp.zeros((heads, 1), F32)
        acc_ref[...] = jnp.zeros((heads, HEAD_DIM), F32)

    s = by_group(lambda g: _dot_nt(q, kv_refs[2 * g][...].astype(BF16)))
    m_p = m_ref[...]
    m_n = jnp.maximum(m_p, jnp.max(s, axis=-1, keepdims=True))
    alpha = jnp.exp(m_p - m_n)
    p = jnp.exp(s - m_n)
    pb = p.astype(BF16)
    pv = by_group(lambda g: jnp.dot(pb, kv_refs[2 * g + 1][...].astype(BF16), preferred_element_type=F32))
    m_ref[...] = m_n
    l_ref[...] = alpha * l_ref[...] + jnp.sum(p, axis=-1, keepdims=True)
    acc_ref[...] = alpha * acc_ref[...] + pv

    @pl.when(k_id == n_gather - 1)
    def _():
        s_n = new_token_scores(newsel_ref)
        m_p = m_ref[...]
        m_n = jnp.maximum(m_p, s_n)
        alpha = jnp.exp(m_p - m_n)
        p_n = jnp.exp(s_n - m_n)
        l_s = alpha * l_ref[...] + p_n
        o_s = (alpha * acc_ref[...] + p_n * new_token_values(newsel_ref)) / l_s

        win = win_ref[0]
        idx = lax.broadcasted_iota(I32, (1, n_buf), 1)
        ok_w = (n_buf - idx) < WINDOW
        s_w = by_group(lambda g: _dot_nt(q, win[:, g * HEAD_DIM:(g + 1) * HEAD_DIM].astype(BF16)))
        s_w = s_w + jnp.where(ok_w, 0.0, NEG_INF)
        s_wn = new_token_scores(newwin_ref)
        m_w = jnp.maximum(jnp.max(s_w, axis=-1, keepdims=True), s_wn)
        p_w = jnp.exp(s_w - m_w)
        p_wn = jnp.exp(s_wn - m_w)
        l_w = jnp.sum(p_w, axis=-1, keepdims=True) + p_wn
        p_wb = p_w.astype(BF16)
        o_w = by_group(lambda g: jnp.dot(
            p_wb, win[:, kvw + g * HEAD_DIM:kvw + (g + 1) * HEAD_DIM].astype(BF16),
            preferred_element_type=F32))
        o_w = (o_w + p_wn * new_token_values(newwin_ref)) / l_w

        rolled = pltpu.roll(win, n_buf - 1, 0)
        ridx = lax.broadcasted_iota(I32, (n_buf, 1), 0)
        winout_ref[0] = jnp.where(ridx == n_buf - 1, newwin_ref[0], rolled)

        graw = jnp.broadcast_to(g_ref[0], (heads, groups * LANES))
        glane = lax.broadcasted_iota(I32, (heads, groups * LANES), 1)
        hidx = lax.broadcasted_iota(I32, (heads, 1), 0)
        gbase = (hidx // rep) * LANES + 3 * (hidx % rep)
        gate = lambda c: _sigmoid(jnp.sum(jnp.where(glane == gbase + c, graw, 0.0), axis=-1, keepdims=True))
        o = gate(0) * oc_ref[0] + gate(1) * o_s + gate(2) * o_w
        for h in range(heads):
            z = z_ref[0, :, h * HEAD_DIM:(h + 1) * HEAD_DIM]
            act_ref[0, :, h * HEAD_DIM:(h + 1) * HEAD_DIM] = (o[h:h + 1, :] * _silu(z)).astype(BF16)


def _nsa_sample_attend(page_table, sel_idx, q3, cache4, layer, new_sel, new_win, o_c, win_state, gates3,
                       z3, *, groups, rep, n_gather):
    db, heads, _ = q3.shape
    kvw = groups * HEAD_DIM
    n_buf = win_state.shape[2]
    bpp = PAGE_SIZE // CMP_BLOCK
    assert sel_idx.shape[2] == n_gather

    def gather_spec(g, is_v):
        col = g + (groups if is_v else 0)

        def index_map(bi, k, pt, si):
            blk = si[bi, g, k]
            return (layer, pt[bi, blk // bpp], blk % bpp, col)
        return pl.BlockSpec((None, None, CMP_BLOCK, HEAD_DIM), index_map)

    in_specs = [pl.BlockSpec((1, heads, HEAD_DIM), lambda bi, k, pt, si: (bi, 0, 0))]
    for g in range(groups):
        in_specs += [gather_spec(g, False), gather_spec(g, True)]
    row3 = lambda w: pl.BlockSpec((1, 1, w), lambda bi, k, pt, si: (bi, 0, 0))
    in_specs += [
        row3(2 * kvw), row3(2 * kvw),
        pl.BlockSpec((1, heads, HEAD_DIM), lambda bi, k, pt, si: (bi, 0, 0)),
        pl.BlockSpec((None, 1, n_buf, 2 * kvw), lambda bi, k, pt, si: (layer, bi, 0, 0)),
        row3(groups * LANES), row3(heads * HEAD_DIM),
    ]
    grid_spec = pltpu.PrefetchScalarGridSpec(
        num_scalar_prefetch=2,
        grid=(db, n_gather),
        in_specs=in_specs,
        out_specs=[
            row3(heads * HEAD_DIM),
            pl.BlockSpec((1, n_buf, 2 * kvw), lambda bi, k, pt, si: (bi, 0, 0)),
        ],
        scratch_shapes=[pltpu.VMEM((heads, 1), F32), pltpu.VMEM((heads, 1), F32),
                        pltpu.VMEM((heads, HEAD_DIM), F32)],
    )
    kern = functools.partial(_nsa_sample_attend_kernel, groups=groups, rep=rep, n_gather=n_gather,
                             n_buf=n_buf)
    return pl.pallas_call(
        kern,
        grid_spec=grid_spec,
        out_shape=[
            jax.ShapeDtypeStruct((db, 1, heads * HEAD_DIM), BF16),
            jax.ShapeDtypeStruct((db, n_buf, 2 * kvw), F32),
        ],
        compiler_params=_cparams(32, ("arbitrary", "arbitrary")),
        name="nsa_sample_attend",
    )(page_table, sel_idx, q3, *([cache4] * (2 * groups)), new_sel, new_win, o_c, win_state, gates3, z3)


def _rope_tables(pos):
    half = ROT_DIM // 2
    inv_freq = ROPE_THETA ** (-jnp.arange(half, dtype=F32) * (2.0 / ROT_DIM))
    ang = pos.astype(F32)[:, None] * inv_freq[None, :]
    cos, sin = jnp.cos(ang), jnp.sin(ang)
    n = pos.shape[0]
    cos_t = jnp.concatenate([cos, cos, jnp.ones((n, HEAD_DIM - ROT_DIM), F32)], axis=1)
    sin_t = jnp.concatenate([-sin, sin, jnp.zeros((n, HEAD_DIM - ROT_DIM), F32)], axis=1)
    return cos_t, sin_t


def _odd_weight_layout(w, cw, kvw, rep):
    k = w.shape[0]
    groups = kvw // HEAD_DIM
    n_qkv = cw + 6 * kvw
    n_gate = 3 * groups * rep
    gates = w[:, n_qkv:n_qkv + n_gate].reshape(k, groups, 3 * rep)
    gates = jnp.pad(gates, ((0, 0), (0, 0), (0, LANES - 3 * rep))).reshape(k, groups * LANES)
    return jnp.concatenate([w[:, :n_qkv], w[:, n_qkv + n_gate:], gates], axis=1).astype(BF16)


def kernel(x_prompt, x_sample, cache_cmp_kv, cache_sel_kv, state_win_kv, state_pool, page_table,
           norm_even, w_in_even, v_norm, w_spatial, b_spatial, w_pool, pool_scale, w_out_even,
           norm_odd, w_in_odd, q_norm, k_norm, w_out_odd):
    bsz, seq, d_model = x_prompt.shape
    db, dec_t, _ = x_sample.shape
    assert dec_t == 1
    n_even, n_odd = norm_even.shape[0], norm_odd.shape[0]
    depth = n_even + n_odd
    n_pages = page_table.shape[1]
    past_len = n_pages * PAGE_SIZE
    aw = v_norm.shape[1]
    bw = pool_scale.shape[1]
    groups = C_KV_HEADS
    kvw = groups * HEAD_DIM
    cw = w_out_odd.shape[1]
    rep = cw // kvw
    heads = cw // HEAD_DIM
    n_phys = cache_cmp_kv.shape[1]
    n_buf = state_win_kv.shape[2]
    assert past_len % CMP_BLOCK == 0 and seq % CMP_BLOCK == 0
    nb_p = seq // CMP_BLOCK
    nb_past = past_len // CMP_BLOCK
    n_past_sel = min(N_SEL, nb_past + 1) - 1
    assert nb_past >= 2 and n_past_sel >= 2

    xp = x_prompt.reshape(bsz * seq, d_model)
    xs = x_sample.reshape(db, d_model)

    cos_p, sin_p = _rope_tables(jnp.arange(seq, dtype=I32))
    cos_s, sin_s = _rope_tables(jnp.full((db,), past_len, dtype=I32))
    nbp_pad = -(-nb_p // LANES) * LANES
    expand = (jnp.arange(seq, dtype=I32)[None, :] // CMP_BLOCK
              == jnp.arange(nbp_pad, dtype=I32)[:, None]).astype(BF16)

    cache_cmp4 = cache_cmp_kv.reshape(n_odd, n_phys, PAGE_SIZE, 2 * kvw)
    cache_sel4 = cache_sel_kv.reshape(n_odd, n_phys, PAGE_SIZE, 2 * kvw)
    win_state4 = state_win_kv.reshape(n_odd, db, n_buf, 2 * kvw)

    z_off = cw + 6 * kvw
    g_off = z_off + cw
    tn_in = 2560

    cmp_p, cmp_s, sel_p, sel_s, win_p, win_s = [], [], [], [], [], []
    pool_p, pool_s, gv_s = [], [], []
    for layer in range(depth):
        li = layer // 2
        if layer % 2 == 0:
            w_in = w_in_even[li].astype(BF16)
            w_out = w_out_even[li].astype(BF16)
            g_in = norm_even[li][None, :]
            vg = v_norm[li][None, :]
            ps = pool_scale[li][None, :]
            wp = w_pool[li].astype(BF16)
            tn = tn_in if w_in.shape[1] % tn_in == 0 else w_in.shape[1]
            proj = _norm_matmul(xp, g_in, w_in, tn=tn)
            act, pool16 = _even_mix(proj.reshape(bsz, seq, -1), vg, w_spatial[li], b_spatial[li].T, wp, ps)
            xp = _matmul_residual(act.reshape(bsz * seq, aw + bw), w_out, xp)
            pool_p.append(pool16[:, 16 - POOL_PAD:])
            proj_s = _norm_matmul(xs, g_in, w_in, tn=tn)
            hd = aw // A_HEADS
            w00 = jnp.repeat(w_spatial[li][:, 0, 0], hd)[None, :]
            b0 = jnp.repeat(b_spatial[li][:, 0], hd)[None, :]
            act_s, vn_s, new_state = _even_mix_sample(
                proj_s, vg, w00, b0, wp, ps, jnp.swapaxes(state_pool[li], 0, 1), past_len)
            xs = _matmul_residual(act_s, w_out, xs)
            pool_s.append(jnp.swapaxes(new_state, 0, 1))
            gv_s.append(vn_s.reshape(db, 1, aw))
        else:
            w_in = _odd_weight_layout(w_in_odd[li], cw, kvw, rep)
            w_out = w_out_odd[li].astype(BF16)
            g_in = norm_odd[li][None, :]
            qg = q_norm[li][None, :]
            kg = k_norm[li]
            tn = tn_in if w_in.shape[1] % tn_in == 0 else w_in.shape[1]
            proj = _norm_matmul(xp, g_in, w_in, tn=tn)
            q_b, kv_cmp, kv_sel, kv_win, kvb_sel, kvb_win, means = _odd_post(
                proj, cos_p, sin_p, qg, kg, cw=cw, kvw=kvw, with_means=True)
            r3 = lambda a: a.reshape(bsz, seq, a.shape[-1])
            act = _nsa_prompt(r3(q_b), r3(kvb_sel), r3(kvb_win), means.reshape(bsz, nb_p, 2 * kvw),
                              r3(proj), expand, cw=cw, z_off=z_off, g_off=g_off)
            xp = _matmul_residual(act.reshape(bsz * seq, cw), w_out, xp)
            kv6 = lambda a, n: a.reshape(n, -1, 2, groups, HEAD_DIM)
            cmp_p.append(kv6(kv_cmp, bsz))
            sel_p.append(kv6(kv_sel, bsz))
            win_keep = min(WINDOW, seq)
            win_p.append(kv6(kv_win, bsz)[:, seq - win_keep:])
            proj_s = _norm_matmul(xs, g_in, w_in, tn=tn)
            q_s, kvc_s, kvs_s, kvw_s, _, _ = _odd_post(
                proj_s, cos_s, sin_s, qg, kg, cw=cw, kvw=kvw, with_means=False)
            q3s = q_s.reshape(db, heads, HEAD_DIM)
            means_s = _page_means(cache_cmp4, page_table, li)
            o_c, idx = _nsa_sample_select(q3s, means_s, groups=groups, rep=rep, q_pos=past_len,
                                          n_past_sel=n_past_sel)
            act_s, win_new = _nsa_sample_attend(
                page_table, idx[:, :, :n_past_sel], q3s, cache_sel4, li,
                kvs_s.reshape(db, 1, 2 * kvw), kvw_s.reshape(db, 1, 2 * kvw), o_c, win_state4,
                proj_s[:, g_off:g_off + groups * LANES].reshape(db, 1, groups * LANES),
                proj_s[:, z_off:z_off + cw].reshape(db, 1, cw),
                groups=groups, rep=rep, n_gather=n_past_sel)
            xs = _matmul_residual(act_s.reshape(db, cw), w_out, xs)
            cmp_s.append(kv6(kvc_s, db))
            sel_s.append(kv6(kvs_s, db))
            win_s.append(win_new.reshape(db, n_buf, 2, groups, HEAD_DIM))
    return (xp.reshape(bsz, seq, d_model), xs.reshape(db, 1, d_model),
            jnp.stack(cmp_p), jnp.stack(cmp_s), jnp.stack(sel_p), jnp.stack(sel_s),
            jnp.stack(win_p), jnp.stack(win_s), jnp.stack(pool_p), jnp.stack(pool_s), jnp.stack(gv_s))
```

```python
import functools

import jax
import jax.numpy as jnp
import numpy as np
from jax import lax
from jax.experimental import pallas as pl
from jax.experimental.pallas import tpu as pltpu

F32 = jnp.float32
BF16 = jnp.bfloat16
I32 = jnp.int32

EPS = 1e-6
PAGE_SIZE = 128
A_HEADS = 8
GMLP_CHUNK = 128
POOL_WINDOWS = (2, 4, 8, 16)
POOL_PAD = max(POOL_WINDOWS) - 1
HEAD_DIM = 128
C_KV_HEADS = 4
CMP_BLOCK = 64
N_SEL = 16
WINDOW = 512
SEL_FORCE = 1e4
NEG_INF = -1e30
ROPE_THETA = 500000.0
ROT_DIM = HEAD_DIM // 4
LOG2E = 1.4426950408889634
Q_SCALE = HEAD_DIM ** -0.5 * LOG2E

LANES = 128
SUBLANES = 8
MIB = 1024 * 1024


def _cparams(vmem_mib, semantics=None):
    return pltpu.CompilerParams(vmem_limit_bytes=int(vmem_mib * MIB), dimension_semantics=semantics)


def _silu(x):
    return x * (1.0 / (1.0 + jnp.exp(-x)))


def _sigmoid(x):
    return 1.0 / (1.0 + jnp.exp(-x))


def _dot_nt(a, b):
    return lax.dot_general(a, b, (((1,), (1,)), ((), ())), preferred_element_type=F32)


def _norm_mm_kernel(x_ref, g_ref, w_ref, o_ref):
    x = x_ref[...]
    ms = jnp.mean(x * x, axis=-1, keepdims=True)
    h = (x * lax.rsqrt(ms + EPS) * g_ref[...]).astype(BF16)
    o_ref[...] = jnp.dot(h, w_ref[...], preferred_element_type=F32)


def _mm_res_kernel(a_ref, w_ref, r_ref, o_ref):
    o_ref[...] = r_ref[...] + jnp.dot(a_ref[...], w_ref[...], preferred_element_type=F32)


def _row_tile(m, want):
    return want if m % want == 0 else m


def _norm_matmul(x, g, w, *, tn, tm=512):
    m, k = x.shape
    n = w.shape[1]
    tm = _row_tile(m, tm)
    assert n % tn == 0
    vmem = 2 * (tm * k * 4 + k * tn * 2 + tm * tn * 4) / MIB + 8
    return pl.pallas_call(
        _norm_mm_kernel,
        grid=(n // tn, m // tm),
        in_specs=[
            pl.BlockSpec((tm, k), lambda j, i: (i, 0)),
            pl.BlockSpec((1, k), lambda j, i: (0, 0)),
            pl.BlockSpec((k, tn), lambda j, i: (0, j)),
        ],
        out_specs=pl.BlockSpec((tm, tn), lambda j, i: (i, j)),
        out_shape=jax.ShapeDtypeStruct((m, n), F32),
        compiler_params=_cparams(vmem),
        name="norm_matmul",
    )(x, g, w)


def _matmul_residual(a, w, res, *, tm=512):
    m, k = a.shape
    n = w.shape[1]
    tm = _row_tile(m, tm)
    vmem = 2 * (tm * k * 2 + k * n * 2 + 2 * tm * n * 4) / MIB + 8
    return pl.pallas_call(
        _mm_res_kernel,
        grid=(m // tm,),
        in_specs=[
            pl.BlockSpec((tm, k), lambda i: (i, 0)),
            pl.BlockSpec((k, n), lambda i: (0, 0)),
            pl.BlockSpec((tm, n), lambda i: (i, 0)),
        ],
        out_specs=pl.BlockSpec((tm, n), lambda i: (i, 0)),
        out_shape=jax.ShapeDtypeStruct((m, n), F32),
        compiler_params=_cparams(vmem),
        name="matmul_residual",
    )(a, w, res)


def _even_mix_kernel(proj_ref, vg_ref, ws_ref, bst_ref, wp_ref, ps_ref, act_ref, pool_ref, ext_ref,
                     *, tm, aw, bw):
    c = pl.program_id(1)
    n_c = pl.num_programs(1)
    hd = aw // A_HEADS
    pg = bw // len(POOL_WINDOWS)

    row = lax.broadcasted_iota(I32, (tm, tm), 0)
    col = lax.broadcasted_iota(I32, (tm, tm), 1)
    causal = row >= col

    for h in range(A_HEADS):
        sl = slice(h * hd, (h + 1) * hd)
        u = proj_ref[0, :, sl]
        v = proj_ref[0, :, aw + h * hd:aw + (h + 1) * hd]
        za = proj_ref[0, :, 2 * aw + h * hd:2 * aw + (h + 1) * hd]
        ms = jnp.mean(v * v, axis=-1, keepdims=True)
        vn = v * lax.rsqrt(ms + EPS) * vg_ref[:, sl]
        w = jnp.where(causal, ws_ref[h], 0.0).astype(BF16)
        s = jnp.dot(w, vn.astype(BF16), preferred_element_type=F32) + bst_ref[:, h:h + 1]
        act_ref[0, :, sl] = ((u * s) * _silu(za)).astype(BF16)

    p_off = 3 * aw
    zb_off = 3 * aw + bw

    @pl.when(c == 0)
    def _():
        ext_ref[0:16, :] = jnp.zeros((16, bw), F32)

    ext_ref[16:16 + tm, :] = proj_ref[0, :, p_off:p_off + bw]
    pos = c * tm + lax.broadcasted_iota(I32, (tm, 1), 0)
    for g, wnd in enumerate(POOL_WINDOWS):
        sl = slice(g * pg, (g + 1) * pg)
        cur = ext_ref[16:16 + tm, sl]
        acc = cur
        for k in range(1, wnd):
            acc = acc + ext_ref[16 - k:16 - k + tm, sl]
        cnt = jnp.minimum(pos + 1, wnd).astype(F32)
        d = acc / cnt - cur
        y = jnp.dot(d.astype(BF16), wp_ref[g], preferred_element_type=F32)
        zb = proj_ref[0, :, zb_off + g * pg:zb_off + (g + 1) * pg]
        act_ref[0, :, aw + g * pg:aw + (g + 1) * pg] = ((y * ps_ref[:, sl]) * _silu(zb)).astype(BF16)

    tail = ext_ref[tm:tm + 16, :]
    ext_ref[0:16, :] = tail

    @pl.when(c == n_c - 1)
    def _():
        pool_ref[0] = tail


def _even_mix(proj3, v_gain, w_s, b_st, w_pool, pool_scale):
    b, s, _ = proj3.shape
    aw = v_gain.shape[1]
    bw = pool_scale.shape[1]
    tm = GMLP_CHUNK
    assert s % tm == 0 and s >= 16
    n_in = 3 * aw + 2 * bw
    kern = functools.partial(_even_mix_kernel, tm=tm, aw=aw, bw=bw)
    return pl.pallas_call(
        kern,
        grid=(b, s // tm),
        in_specs=[
            pl.BlockSpec((1, tm, n_in), lambda i, c: (i, c, 0)),
            pl.BlockSpec((1, aw), lambda i, c: (0, 0)),
            pl.BlockSpec(w_s.shape, lambda i, c: (0, 0, 0)),
            pl.BlockSpec(b_st.shape, lambda i, c: (0, 0)),
            pl.BlockSpec(w_pool.shape, lambda i, c: (0, 0, 0)),
            pl.BlockSpec((1, bw), lambda i, c: (0, 0)),
        ],
        out_specs=[
            pl.BlockSpec((1, tm, aw + bw), lambda i, c: (i, c, 0)),
            pl.BlockSpec((1, 16, bw), lambda i, c: (i, 0, 0)),
        ],
        out_shape=[
            jax.ShapeDtypeStruct((b, s, aw + bw), BF16),
            jax.ShapeDtypeStruct((b, 16, bw), F32),
        ],
        scratch_shapes=[pltpu.VMEM((16 + tm, bw), F32)],
        compiler_params=_cparams(32, ("arbitrary", "arbitrary")),
        name="even_mix",
    )(proj3, v_gain, w_s, b_st, w_pool, pool_scale)


def _even_mix_sample_kernel(proj_ref, vg_ref, w00_ref, b0_ref, wp_ref, ps_ref, st_ref,
                            act_ref, vn_ref, newst_ref, *, aw, bw, pos):
    hd = aw // A_HEADS
    pg = bw // len(POOL_WINDOWS)
    for h in range(A_HEADS):
        sl = slice(h * hd, (h + 1) * hd)
        u = proj_ref[:, sl]
        v = proj_ref[:, aw + h * hd:aw + (h + 1) * hd]
        za = proj_ref[:, 2 * aw + h * hd:2 * aw + (h + 1) * hd]
        ms = jnp.mean(v * v, axis=-1, keepdims=True)
        vn = v * lax.rsqrt(ms + EPS) * vg_ref[:, sl]
        vn_ref[:, sl] = vn
        s = w00_ref[:, sl] * vn + b0_ref[:, sl]
        act_ref[:, sl] = ((u * s) * _silu(za)).astype(BF16)

    p = proj_ref[:, 3 * aw:3 * aw + bw]
    for g, wnd in enumerate(POOL_WINDOWS):
        sl = slice(g * pg, (g + 1) * pg)
        cur = p[:, sl]
        acc = cur
        for k in range(1, wnd):
            acc = acc + st_ref[POOL_PAD - k, :, sl]
        cnt = float(min(pos + 1, wnd))
        d = acc / cnt - cur
        y = jnp.dot(d.astype(BF16), wp_ref[g], preferred_element_type=F32)
        zb = proj_ref[:, 3 * aw + bw + g * pg:3 * aw + bw + (g + 1) * pg]
        act_ref[:, aw + g * pg:aw + (g + 1) * pg] = ((y * ps_ref[:, sl]) * _silu(zb)).astype(BF16)

    for k in range(POOL_PAD - 1):
        newst_ref[k] = st_ref[k + 1]
    newst_ref[POOL_PAD - 1] = p


def _even_mix_sample(proj, v_gain, w00, b0, w_pool, pool_scale, state_t, pos):
    db = proj.shape[0]
    aw = v_gain.shape[1]
    bw = pool_scale.shape[1]
    kern = functools.partial(_even_mix_sample_kernel, aw=aw, bw=bw, pos=pos)
    return pl.pallas_call(
        kern,
        out_shape=[
            jax.ShapeDtypeStruct((db, aw + bw), BF16),
            jax.ShapeDtypeStruct((db, aw), F32),
            jax.ShapeDtypeStruct((POOL_PAD, db, bw), F32),
        ],
        name="even_mix_sample",
    )(proj, v_gain, w00, b0, w_pool, pool_scale, state_t)


def _odd_post_kernel(proj_ref, cos_ref, sin_ref, qg_ref, kg_ref, *out_refs, tm, cw, kvw, with_means):
    if with_means:
        q_out, cmp_out, sel_out, win_out, selb_out, winb_out, means_out = out_refs
    else:
        q_out, cmp_out, sel_out, win_out, selb_out, winb_out = out_refs
    cosf = cos_ref[...]
    sinf = sin_ref[...]
    half = ROT_DIM // 2
    first = lax.broadcasted_iota(I32, (tm, HEAD_DIM), 1) < half

    def norm_rope(x, gain):
        ms = jnp.mean(x * x, axis=-1, keepdims=True)
        y = x * lax.rsqrt(ms + EPS) * gain
        rot = jnp.where(first, pltpu.roll(y, HEAD_DIM - half, 1), pltpu.roll(y, half, 1))
        return y * cosf + rot * sinf

    qg = qg_ref[...]
    for h in range(cw // HEAD_DIM):
        sl = slice(h * HEAD_DIM, (h + 1) * HEAD_DIM)
        q_out[:, sl] = (norm_rope(proj_ref[:, sl], qg) * Q_SCALE).astype(BF16)

    outs = ((cmp_out, None), (sel_out, selb_out), (win_out, winb_out))
    for br, (o32, o16) in enumerate(outs):
        k_off = cw + br * 2 * kvw
        v_off = k_off + kvw
        kg = kg_ref[br:br + 1, :]
        for g in range(kvw // HEAD_DIM):
            kk = norm_rope(proj_ref[:, k_off + g * HEAD_DIM:k_off + (g + 1) * HEAD_DIM], kg)
            o32[:, g * HEAD_DIM:(g + 1) * HEAD_DIM] = kk
            if o16 is not None:
                o16[:, g * HEAD_DIM:(g + 1) * HEAD_DIM] = kk.astype(BF16)
        vv = proj_ref[:, v_off:v_off + kvw]
        o32[:, kvw:2 * kvw] = vv
        if o16 is not None:
            o16[:, kvw:2 * kvw] = vv.astype(BF16)

    if with_means:
        for j in range(tm // CMP_BLOCK):
            blk = cmp_out[j * CMP_BLOCK:(j + 1) * CMP_BLOCK, :]
            means_out[0, j:j + 1, :] = jnp.mean(blk, axis=0, keepdims=True)


def _odd_post(proj, cos_t, sin_t, q_gain, k_gain, *, cw, kvw, with_means, tm=256):
    m = proj.shape[0]
    tm = _row_tile(m, tm)
    n_pos_tiles = cos_t.shape[0] // tm
    n_used = cw + 6 * kvw
    kern = functools.partial(_odd_post_kernel, tm=tm, cw=cw, kvw=kvw, with_means=with_means)
    row_spec = lambda w: pl.BlockSpec((tm, w), lambda i: (i, 0))
    out_shape = [
        jax.ShapeDtypeStruct((m, cw), BF16),
        jax.ShapeDtypeStruct((m, 2 * kvw), F32),
        jax.ShapeDtypeStruct((m, 2 * kvw), F32),
        jax.ShapeDtypeStruct((m, 2 * kvw), F32),
        jax.ShapeDtypeStruct((m, 2 * kvw), BF16),
        jax.ShapeDtypeStruct((m, 2 * kvw), BF16),
    ]
    out_specs = [row_spec(cw)] + [row_spec(2 * kvw)] * 5
    if with_means:
        assert tm % CMP_BLOCK == 0
        out_shape.append(jax.ShapeDtypeStruct((m // tm, tm // CMP_BLOCK, 2 * kvw), F32))
        out_specs.append(pl.BlockSpec((1, tm // CMP_BLOCK, 2 * kvw), lambda i: (i, 0, 0)))
    return pl.pallas_call(
        kern,
        grid=(m // tm,),
        in_specs=[
            pl.BlockSpec((tm, n_used), lambda i: (i, 0)),
            pl.BlockSpec((tm, HEAD_DIM), lambda i: (i % n_pos_tiles, 0)),
            pl.BlockSpec((tm, HEAD_DIM), lambda i: (i % n_pos_tiles, 0)),
            pl.BlockSpec((1, HEAD_DIM), lambda i: (0, 0)),
            pl.BlockSpec((3, HEAD_DIM), lambda i: (0, 0)),
        ],
        out_specs=out_specs,
        out_shape=out_shape,
        compiler_params=_cparams(40),
        name="odd_post",
    )(proj, cos_t, sin_t, q_gain, k_gain)


def _nsa_prompt_kernel(q_ref, ks_ref, vs_ref, kw_ref, vw_ref, kc_ref, vc_ref, g_ref, z_ref,
                       eb_ref, band_ref, tri_ref, eye_ref, o_ref, cnt_ref, s0_ref, s1_ref,
                       *, tq, tk, seq, nb, rep, lw):
    i = pl.program_id(2)
    n_sel = min(N_SEL, nb)
    rows = rep * tq
    bpt = tq // CMP_BLOCK

    q = jnp.concatenate([q_ref[0, :, r * HEAD_DIM:(r + 1) * HEAD_DIM] for r in range(rep)], axis=0)
    t_col = i * tq + lax.broadcasted_iota(I32, (tq, 1), 0)
    t_row = i * tq + lax.broadcasted_iota(I32, (1, tq), 1)

    kc = kc_ref[0].astype(BF16)
    vc = vc_ref[0].astype(BF16)

    blk_row = lax.broadcasted_iota(I32, (1, nb), 1)
    s_c = _dot_nt(q, kc).reshape(rep, tq, nb)
    ok_c = ((blk_row + 1) * CMP_BLOCK - 1) <= t_col
    s_c = s_c + jnp.where(ok_c, 0.0, NEG_INF)[None]
    m_c = jnp.max(s_c, axis=-1, keepdims=True)
    e_c = jnp.exp2(s_c - m_c)
    p_c = e_c / jnp.sum(e_c, axis=-1, keepdims=True) * jnp.where(ok_c, 1.0, 0.0)[None]
    o_c = jnp.dot(p_c.reshape(rows, nb).astype(BF16), vc, preferred_element_type=F32)

    blk_col = lax.broadcasted_iota(I32, (nb, 1), 0)
    s_t = _dot_nt(kc, q)
    ok_t = ((blk_col + 1) * CMP_BLOCK - 1) <= t_row
    bias_t = jnp.where(ok_t, 0.0, NEG_INF)
    okf_t = jnp.where(ok_t, 1.0, 0.0)
    imp = jnp.zeros((nb, tq), F32)
    for r in range(rep):
        s_r = s_t[:, r * tq:(r + 1) * tq] + bias_t
        e_r = jnp.exp2(s_r - jnp.max(s_r, axis=0, keepdims=True))
        imp = imp + e_r / jnp.sum(e_r, axis=0, keepdims=True) * okf_t
    cur = t_row // CMP_BLOCK
    forced = (blk_col == 0) | (blk_col == cur) | (blk_col == cur - 1)
    imp = jnp.where(blk_col <= cur, jnp.where(forced, SEL_FORCE, imp), -1.0)

    q_eye = jnp.concatenate([q, eye_ref[...]], axis=1)

    w0 = pl.multiple_of(jnp.maximum(i * tq + tq - lw, 0), tq)
    k_w = jnp.concatenate([kw_ref[0, pl.ds(w0, lw), :], band_ref[0]], axis=1)
    v_w = jnp.concatenate([vw_ref[0, pl.ds(w0, lw), :], jnp.ones((lw, HEAD_DIM), BF16)], axis=1)
    s_w = _dot_nt(q_eye, k_w)
    p_w = jnp.exp2(s_w - jnp.max(s_w, axis=-1, keepdims=True))
    acc_w = jnp.dot(p_w.astype(BF16), v_w, preferred_element_type=F32)
    o_w = acc_w[:, :HEAD_DIM] / acc_w[:, HEAD_DIM:]

    d0 = pl.multiple_of(i * tq, tq)
    k_d = jnp.concatenate([ks_ref[0, pl.ds(d0, tq), :], tri_ref[...]], axis=1)
    v_d = jnp.concatenate([vs_ref[0, pl.ds(d0, tq), :], jnp.ones((tq, HEAD_DIM), BF16)], axis=1)
    s_d = _dot_nt(q_eye, k_d)
    m_d = jnp.max(s_d, axis=-1, keepdims=True)
    acc_d = jnp.dot(jnp.exp2(s_d - m_d).astype(BF16), v_d, preferred_element_type=F32)

    blocks_hi = (i + 1) * bpt
    blk8 = lax.broadcasted_iota(I32, (SUBLANES, 1), 0)
    cnt_ref[...] = jnp.zeros((nb, tq), F32)
    rank_chunk = 2 * SUBLANES
    for c0 in range(0, nb, rank_chunk):
        @pl.when((c0 < blocks_hi) & (blocks_hi > n_sel))
        def _(c0=c0):
            cnt = cnt_ref[...]
            for j in range(c0, min(c0 + rank_chunk, nb)):
                rj = imp[j:j + 1, :]
                parts = []
                for v0 in range(0, nb, SUBLANES):
                    sub = imp[v0:v0 + SUBLANES, :]
                    if v0 > j:
                        beats = rj >= sub
                    elif v0 + SUBLANES - 1 <= j:
                        beats = rj > sub
                    else:
                        beats = (rj > sub) | ((rj == sub) & (blk8 > j - v0))
                    parts.append(jnp.where(beats, 1.0, 0.0))
                cnt = cnt + jnp.concatenate(parts, axis=0)
            cnt_ref[...] = cnt

    keep = (cnt_ref[...] < n_sel) & (imp >= 0.0) & (blk_col < i * bpt)
    drop_t = jnp.where(keep, 0.0, 1.0)
    if LANES > nb:
        drop_t = jnp.concatenate([drop_t, jnp.zeros((LANES - nb, tq), F32)], axis=0)
    drop = drop_t.T.astype(BF16)
    q_drop = jnp.concatenate([q, jnp.concatenate([drop] * rep, axis=0)], axis=1)

    ones_k = jnp.ones((tk, HEAD_DIM), BF16)
    n_tiles = seq // tk

    def tile_scores(t):
        k0 = pl.multiple_of(t * tk, tk)
        k = jnp.concatenate([ks_ref[0, pl.ds(k0, tk), :], eb_ref[pl.ds(k0, tk), :]], axis=1)
        return _dot_nt(q_drop, k)

    def accumulate(carry, s, t):
        m_p, acc = carry
        k0 = pl.multiple_of(t * tk, tk)
        v = jnp.concatenate([vs_ref[0, pl.ds(k0, tk), :], ones_k], axis=1)
        m_n = jnp.maximum(m_p, jnp.max(s, axis=-1, keepdims=True))
        pv = jnp.dot(jnp.exp2(s - m_n).astype(BF16), v, preferred_element_type=F32)
        return m_n, jnp.exp2(m_p - m_n) * acc + pv

    def sel_body(j, carry):
        t1 = 2 * j + 1
        s1_ref[...] = tile_scores(t1)
        carry = accumulate(carry, s0_ref[...], 2 * j)
        s0_ref[...] = tile_scores(jnp.minimum(t1 + 1, n_tiles - 1))
        return accumulate(carry, s1_ref[...], t1)

    n_past = (i * tq + tk - 1) // tk
    s0_ref[...] = tile_scores(0)
    _, acc_s = lax.fori_loop(0, (n_past + 1) // 2, sel_body, (m_d, acc_d))
    o_s = acc_s[:, :HEAD_DIM] / acc_s[:, HEAD_DIM:]

    gs = _sigmoid(g_ref[0])
    for r in range(rep):
        rs = slice(r * tq, (r + 1) * tq)
        o = gs[:, 3 * r:3 * r + 1] * o_c[rs] + gs[:, 3 * r + 1:3 * r + 2] * o_s[rs] \
            + gs[:, 3 * r + 2:3 * r + 3] * o_w[rs]
        z = z_ref[0, :, r * HEAD_DIM:(r + 1) * HEAD_DIM]
        o_ref[0, :, r * HEAD_DIM:(r + 1) * HEAD_DIM] = (o * _silu(z)).astype(BF16)


def _mask_tables(seq, tq, lw, rep):
    neg = lambda ok: jnp.where(ok, 0.0, NEG_INF).astype(BF16)
    key = jnp.arange(seq, dtype=I32)[:, None]
    col = jnp.arange(LANES, dtype=I32)[None, :]
    block_cols = neg(key // CMP_BLOCK != col)
    c = jnp.arange(lw, dtype=I32)[None, :, None]
    t = jnp.arange(tq, dtype=I32)[None, None, :]
    off = jnp.arange(WINDOW // tq + 1, dtype=I32)[:, None, None] * tq
    rel = off + t - c
    band = neg((rel >= 0) & (rel < WINDOW))
    kd = jnp.arange(tq, dtype=I32)[:, None]
    tri = neg(kd <= jnp.arange(tq, dtype=I32)[None, :])
    eye = jnp.tile(jnp.eye(tq, dtype=BF16), (rep, 1))
    return block_cols, band, tri, eye


def _nsa_prompt(q3, kvb_sel, kvb_win, means3, proj3, tables, *, cw, z_off, g_off, tq=LANES, tk=512):
    b, s, _ = q3.shape
    kvw = kvb_sel.shape[2] // 2
    groups = kvw // HEAD_DIM
    rep = cw // kvw
    nb = means3.shape[1]
    gw = rep * HEAD_DIM
    block_cols, band, tri, eye = tables
    tk = min(tk, s)
    lw = band.shape[1]
    n_band = band.shape[0]
    assert tq == LANES and nb <= LANES and nb % SUBLANES == 0 and lw == WINDOW + tq and s >= lw
    assert s % tq == 0 and s % (2 * tk) == 0 and tk % tq == 0 and tq % CMP_BLOCK == 0
    assert nb * CMP_BLOCK == s
    assert z_off % gw == 0 and g_off % LANES == 0
    kern = functools.partial(_nsa_prompt_kernel, tq=tq, tk=tk, seq=s, nb=nb, rep=rep, lw=lw)
    kv_spec = lambda off: pl.BlockSpec((1, s, HEAD_DIM), lambda bi, g, i: (bi, 0, off + g))
    mean_spec = lambda off: pl.BlockSpec((1, nb, HEAD_DIM), lambda bi, g, i: (bi, 0, off + g))
    return pl.pallas_call(
        kern,
        grid=(b, groups, s // tq),
        in_specs=[
            pl.BlockSpec((1, tq, gw), lambda bi, g, i: (bi, i, g)),
            kv_spec(0), kv_spec(groups), kv_spec(0), kv_spec(groups),
            mean_spec(0), mean_spec(groups),
            pl.BlockSpec((1, tq, LANES), lambda bi, g, i: (bi, i, g_off // LANES + g)),
            pl.BlockSpec((1, tq, gw), lambda bi, g, i: (bi, i, z_off // gw + g)),
            pl.BlockSpec(block_cols.shape, lambda bi, g, i: (0, 0)),
            pl.BlockSpec((1, lw, tq), lambda bi, g, i: (jnp.minimum(i, n_band - 1), 0, 0)),
            pl.BlockSpec(tri.shape, lambda bi, g, i: (0, 0)),
            pl.BlockSpec(eye.shape, lambda bi, g, i: (0, 0)),
        ],
        out_specs=pl.BlockSpec((1, tq, gw), lambda bi, g, i: (bi, i, g)),
        out_shape=jax.ShapeDtypeStruct((b, s, cw), BF16),
        scratch_shapes=[pltpu.VMEM((nb, tq), F32), pltpu.VMEM((rep * tq, tk), F32),
                        pltpu.VMEM((rep * tq, tk), F32)],
        compiler_params=_cparams(48),
        name="nsa_prompt",
    )(q3, kvb_sel, kvb_sel, kvb_win, kvb_win, means3, means3, proj3, proj3, block_cols, band, tri, eye)


def _page_means_kernel(pt_ref, *refs, pps):
    out_ref = refs[pps]
    bpp = PAGE_SIZE // CMP_BLOCK
    n_rows = out_ref.shape[1]
    for r in range(pps):
        for h in range(bpp):
            mean = jnp.mean(refs[r][h * CMP_BLOCK:(h + 1) * CMP_BLOCK], axis=0)
            n = r * bpp + h
            for j in range(n_rows):
                out_ref[0, j, n:n + 1, :] = mean[j:j + 1, :]


def _page_means(cache5, page_table, layer, *, pps=8):
    db, n_pages = page_table.shape
    n_rows, d = cache5.shape[3], cache5.shape[4]
    pps = pps if n_pages % pps == 0 else 1
    bpp = PAGE_SIZE // CMP_BLOCK

    def page_spec(r):
        return pl.BlockSpec((None, None, PAGE_SIZE, n_rows, d),
                            lambda bi, j, pt: (layer, pt[bi, j * pps + r], 0, 0, 0))

    grid_spec = pltpu.PrefetchScalarGridSpec(
        num_scalar_prefetch=1,
        grid=(db, n_pages // pps),
        in_specs=[page_spec(r) for r in range(pps)],
        out_specs=pl.BlockSpec((1, n_rows, pps * bpp, d), lambda bi, j, pt: (bi, 0, j, 0)),
    )
    return pl.pallas_call(
        functools.partial(_page_means_kernel, pps=pps),
        grid_spec=grid_spec,
        out_shape=jax.ShapeDtypeStruct((db, n_rows, n_pages * bpp, d), F32),
        compiler_params=_cparams(24),
        name="page_means",
    )(page_table, *([cache5] * pps))


def _nsa_sample_select_kernel(q_ref, mean_ref, oc_ref, idx_ref, *, nbp, groups, rep, q_pos, n_past_sel):
    q = q_ref[0]
    heads = groups * rep
    lane = lax.broadcasted_iota(I32, (1, nbp), 1)
    ri = lax.broadcasted_iota(I32, (nbp, nbp), 0)
    ci = lax.broadcasted_iota(I32, (nbp, nbp), 1)
    diag = ri == ci
    ok = ((lane + 1) * CMP_BLOCK - 1) <= q_pos
    cur = q_pos // CMP_BLOCK
    forced = (lane == 0) | (lane == cur) | (lane == cur - 1)
    hrow = lax.broadcasted_iota(I32, (heads, 1), 0) // rep
    out_lane = lax.broadcasted_iota(I32, (1, LANES), 1)
    o_c = jnp.zeros((heads, HEAD_DIM), F32)
    for g in range(groups):
        kc = mean_ref[0, g].astype(BF16)
        vc = mean_ref[0, groups + g].astype(BF16)
        s = _dot_nt(q, kc) + jnp.where(ok, 0.0, NEG_INF)
        e = jnp.exp2(s - jnp.max(s, axis=-1, keepdims=True))
        p = e / jnp.sum(e, axis=-1, keepdims=True) * jnp.where(ok, 1.0, 0.0)
        in_g = hrow == g
        o_c = o_c + jnp.where(in_g, jnp.dot(p.astype(BF16), vc, preferred_element_type=F32), 0.0)
        imp = jnp.sum(jnp.where(in_g, p, 0.0), axis=0, keepdims=True)
        imp = jnp.where(lane <= cur, jnp.where(forced, SEL_FORCE, imp), -1.0)
        imp_b = jnp.broadcast_to(imp, (nbp, nbp))
        imp_col = jnp.sum(jnp.where(diag, imp_b, 0.0), axis=1, keepdims=True)
        beats = (imp_col > imp_b) | ((imp_col == imp_b) & (ri < ci))
        cnt = jnp.sum(jnp.where(beats, 1.0, 0.0), axis=0, keepdims=True)
        sel = jnp.where((cnt < n_past_sel) & (imp >= 0.0), 1.0, 0.0)
        sel_col = jnp.sum(jnp.where(diag, jnp.broadcast_to(sel, (nbp, nbp)), 0.0), axis=1, keepdims=True)
        before = jnp.sum(jnp.where(ri < ci, sel_col, 0.0), axis=0, keepdims=True)
        row = jnp.zeros((1, LANES), F32)
        for k in range(n_past_sel):
            hit = (sel > 0.5) & (before == float(k))
            idx_k = jnp.sum(jnp.where(hit, lane.astype(F32), 0.0), axis=1, keepdims=True)
            row = jnp.where(out_lane == k, idx_k, row)
        idx_ref[0, g:g + 1, :] = row.astype(I32)
    oc_ref[0] = o_c


def _nsa_sample_select(q3, means, *, groups, rep, q_pos, n_past_sel):
    db, heads, _ = q3.shape
    nbp = means.shape[2]
    kern = functools.partial(_nsa_sample_select_kernel, nbp=nbp, groups=groups, rep=rep, q_pos=q_pos,
                             n_past_sel=n_past_sel)
    return pl.pallas_call(
        kern,
        grid=(db,),
        in_specs=[
            pl.BlockSpec((1, heads, HEAD_DIM), lambda bi: (bi, 0, 0)),
            pl.BlockSpec((1,) + means.shape[1:], lambda bi: (bi, 0, 0, 0)),
        ],
        out_specs=[
            pl.BlockSpec((1, heads, HEAD_DIM), lambda bi: (bi, 0, 0)),
            pl.BlockSpec((1, groups, LANES), lambda bi: (bi, 0, 0)),
        ],
        out_shape=[
            jax.ShapeDtypeStruct((db, heads, HEAD_DIM), F32),
            jax.ShapeDtypeStruct((db, groups, LANES), I32),
        ],
        compiler_params=_cparams(24),
        name="nsa_sample_select",
    )(q3, means)


def _nsa_sample_attend_kernel(pt_ref, si_ref, q_ref, *refs, groups, rep, n_gather, n_buf):
    kv_refs = refs[:groups]
    (newsel_ref, newwin_ref, oc_ref, win_ref, g_ref, z_ref,
     act_ref, winout_ref, m_ref, l_ref, acc_ref) = refs[groups:]
    kvw = groups * HEAD_DIM
    heads = groups * rep
    n_rows = 2 * groups
    k_id = pl.program_id(1)
    q = q_ref[0]
    qf = q.astype(F32)
    hgrp = lax.broadcasted_iota(I32, (heads, 1), 0) // rep

    def own_key_rows(n_tok):
        lane = lax.broadcasted_iota(I32, (1, n_tok * n_rows), 1)
        return lane % n_rows == hgrp, lane // n_rows

    def by_group(fn):
        out = None
        for g in range(groups):
            val = jnp.where(hgrp == g, fn(g), 0.0)
            out = val if out is None else out + val
        return out

    def new_token_scores(row_ref):
        return by_group(lambda g: jnp.sum(
            qf * row_ref[0, :, g * HEAD_DIM:(g + 1) * HEAD_DIM], axis=-1, keepdims=True))

    def new_token_values(row_ref):
        return by_group(lambda g: jnp.broadcast_to(
            row_ref[0, :, kvw + g * HEAD_DIM:kvw + (g + 1) * HEAD_DIM], (heads, HEAD_DIM)))

    @pl.when(k_id == 0)
    def _():
        m_ref[...] = jnp.full((heads, 1), NEG_INF, F32)
        l_ref[...] = jnp.zeros((heads, 1), F32)
        acc_ref[...] = jnp.zeros((heads, HEAD_DIM), F32)

    xs = [kv_refs[g][...].reshape(CMP_BLOCK * n_rows, HEAD_DIM).astype(BF16) for g in range(groups)]
    own, _ = own_key_rows(CMP_BLOCK)
    s = by_group(lambda g: _dot_nt(q, xs[g])) + jnp.where(own, 0.0, NEG_INF)
    m_p = m_ref[...]
    m_n = jnp.maximum(m_p, jnp.max(s, axis=-1, keepdims=True))
    alpha = jnp.exp2(m_p - m_n)
    p = jnp.exp2(s - m_n)
    pb = pltpu.roll(p, groups, 1).astype(BF16)
    pv = by_group(lambda g: jnp.dot(pb, xs[g], preferred_element_type=F32))
    m_ref[...] = m_n
    l_ref[...] = alpha * l_ref[...] + jnp.sum(p, axis=-1, keepdims=True)
    acc_ref[...] = alpha * acc_ref[...] + pv

    @pl.when(k_id == n_gather - 1)
    def _():
        s_n = new_token_scores(newsel_ref)
        m_p = m_ref[...]
        m_n = jnp.maximum(m_p, s_n)
        alpha = jnp.exp2(m_p - m_n)
        p_n = jnp.exp2(s_n - m_n)
        l_s = alpha * l_ref[...] + p_n
        o_s = (alpha * acc_ref[...] + p_n * new_token_values(newsel_ref)) / l_s

        xw = win_ref[0].reshape(n_buf * n_rows, HEAD_DIM).astype(BF16)
        own_w, tok_w = own_key_rows(n_buf)
        ok_w = own_w & ((n_buf - tok_w) < WINDOW)
        s_w = _dot_nt(q, xw) + jnp.where(ok_w, 0.0, NEG_INF)
        s_wn = new_token_scores(newwin_ref)
        m_w = jnp.maximum(jnp.max(s_w, axis=-1, keepdims=True), s_wn)
        p_w = jnp.exp2(s_w - m_w)
        p_wn = jnp.exp2(s_wn - m_w)
        l_w = jnp.sum(p_w, axis=-1, keepdims=True) + p_wn
        o_w = jnp.dot(pltpu.roll(p_w, groups, 1).astype(BF16), xw, preferred_element_type=F32)
        o_w = (o_w + p_wn * new_token_values(newwin_ref)) / l_w

        winout_ref[0, 0:n_buf - 1] = win_ref[0, 1:n_buf]
        for j in range(n_rows):
            winout_ref[0, n_buf - 1, j:j + 1, :] = newwin_ref[0, :, j * HEAD_DIM:(j + 1) * HEAD_DIM]

        graw = jnp.broadcast_to(g_ref[0], (heads, groups * LANES))
        glane = lax.broadcasted_iota(I32, (heads, groups * LANES), 1)
        hidx = lax.broadcasted_iota(I32, (heads, 1), 0)
        gbase = (hidx // rep) * LANES + 3 * (hidx % rep)
        gate = lambda c: _sigmoid(jnp.sum(jnp.where(glane == gbase + c, graw, 0.0), axis=-1, keepdims=True))
        o = gate(0) * oc_ref[0] + gate(1) * o_s + gate(2) * o_w
        for h in range(heads):
            z = z_ref[0, :, h * HEAD_DIM:(h + 1) * HEAD_DIM]
            act_ref[0, :, h * HEAD_DIM:(h + 1) * HEAD_DIM] = (o[h:h + 1, :] * _silu(z)).astype(BF16)


def _nsa_sample_attend(page_table, sel_idx, q3, cache5, layer, new_sel, new_win, o_c, win_state, gates3,
                       z3, *, groups, rep, n_gather):
    db, heads, _ = q3.shape
    kvw = groups * HEAD_DIM
    n_buf = win_state.shape[2]
    n_rows = 2 * groups
    bpp = PAGE_SIZE // CMP_BLOCK
    assert sel_idx.shape[2] == n_gather

    def gather_spec(g):
        def index_map(bi, k, pt, si):
            blk = si[bi, g, k]
            return (layer, pt[bi, blk // bpp], blk % bpp, 0, 0)
        return pl.BlockSpec((None, None, CMP_BLOCK, n_rows, HEAD_DIM), index_map)

    in_specs = [pl.BlockSpec((1, heads, HEAD_DIM), lambda bi, k, pt, si: (bi, 0, 0))]
    in_specs += [gather_spec(g) for g in range(groups)]
    row3 = lambda w: pl.BlockSpec((1, 1, w), lambda bi, k, pt, si: (bi, 0, 0))
    in_specs += [
        row3(2 * kvw), row3(2 * kvw),
        pl.BlockSpec((1, heads, HEAD_DIM), lambda bi, k, pt, si: (bi, 0, 0)),
        pl.BlockSpec((None, 1, n_buf, n_rows, HEAD_DIM), lambda bi, k, pt, si: (layer, bi, 0, 0, 0)),
        row3(groups * LANES), row3(heads * HEAD_DIM),
    ]
    grid_spec = pltpu.PrefetchScalarGridSpec(
        num_scalar_prefetch=2,
        grid=(db, n_gather),
        in_specs=in_specs,
        out_specs=[
            row3(heads * HEAD_DIM),
            pl.BlockSpec((1, n_buf, n_rows, HEAD_DIM), lambda bi, k, pt, si: (bi, 0, 0, 0)),
        ],
        scratch_shapes=[pltpu.VMEM((heads, 1), F32), pltpu.VMEM((heads, 1), F32),
                        pltpu.VMEM((heads, HEAD_DIM), F32)],
    )
    kern = functools.partial(_nsa_sample_attend_kernel, groups=groups, rep=rep, n_gather=n_gather,
                             n_buf=n_buf)
    return pl.pallas_call(
        kern,
        grid_spec=grid_spec,
        out_shape=[
            jax.ShapeDtypeStruct((db, 1, heads * HEAD_DIM), BF16),
            jax.ShapeDtypeStruct((db, n_buf, n_rows, HEAD_DIM), F32),
        ],
        compiler_params=_cparams(32, ("arbitrary", "arbitrary")),
        name="nsa_sample_attend",
    )(page_table, sel_idx, q3, *([cache5] * groups), new_sel, new_win, o_c, win_state, gates3, z3)


def _rope_tables(pos):
    half = ROT_DIM // 2
    inv_freq = ROPE_THETA ** (-jnp.arange(half, dtype=F32) * (2.0 / ROT_DIM))
    ang = pos.astype(F32)[:, None] * inv_freq[None, :]
    cos, sin = jnp.cos(ang), jnp.sin(ang)
    n = pos.shape[0]
    cos_t = jnp.concatenate([cos, cos, jnp.ones((n, HEAD_DIM - ROT_DIM), F32)], axis=1)
    sin_t = jnp.concatenate([-sin, sin, jnp.zeros((n, HEAD_DIM - ROT_DIM), F32)], axis=1)
    return cos_t, sin_t


def _odd_weight_layout(w, cw, kvw, rep):
    k = w.shape[0]
    groups = kvw // HEAD_DIM
    n_qkv = cw + 6 * kvw
    n_gate = 3 * groups * rep
    gates = w[:, n_qkv:n_qkv + n_gate].reshape(k, groups, 3 * rep)
    gates = jnp.pad(gates, ((0, 0), (0, 0), (0, LANES - 3 * rep))).reshape(k, groups * LANES)
    return jnp.concatenate([w[:, :n_qkv], w[:, n_qkv + n_gate:], gates], axis=1).astype(BF16)


def kernel(x_prompt, x_sample, cache_cmp_kv, cache_sel_kv, state_win_kv, state_pool, page_table,
           norm_even, w_in_even, v_norm, w_spatial, b_spatial, w_pool, pool_scale, w_out_even,
           norm_odd, w_in_odd, q_norm, k_norm, w_out_odd):
    bsz, seq, d_model = x_prompt.shape
    db, dec_t, _ = x_sample.shape
    assert dec_t == 1
    n_even, n_odd = norm_even.shape[0], norm_odd.shape[0]
    depth = n_even + n_odd
    n_pages = page_table.shape[1]
    past_len = n_pages * PAGE_SIZE
    aw = v_norm.shape[1]
    bw = pool_scale.shape[1]
    groups = C_KV_HEADS
    kvw = groups * HEAD_DIM
    cw = w_out_odd.shape[1]
    rep = cw // kvw
    heads = cw // HEAD_DIM
    n_phys = cache_cmp_kv.shape[1]
    n_buf = state_win_kv.shape[2]
    assert past_len % CMP_BLOCK == 0 and seq % CMP_BLOCK == 0
    nb_p = seq // CMP_BLOCK
    nb_past = past_len // CMP_BLOCK
    n_past_sel = min(N_SEL, nb_past + 1) - 1
    assert nb_past >= 2 and n_past_sel >= 2

    xp = x_prompt.reshape(bsz * seq, d_model)
    xs = x_sample.reshape(db, d_model)

    cos_p, sin_p = _rope_tables(jnp.arange(seq, dtype=I32))
    cos_s, sin_s = _rope_tables(jnp.full((db,), past_len, dtype=I32))
    tables = _mask_tables(seq, LANES, WINDOW + LANES, rep)

    cache_cmp5 = cache_cmp_kv.reshape(n_odd, n_phys, PAGE_SIZE, 2 * groups, HEAD_DIM)
    cache_sel5 = cache_sel_kv.reshape(n_odd, n_phys, PAGE_SIZE, 2 * groups, HEAD_DIM)
    win_state5 = state_win_kv.reshape(n_odd, db, n_buf, 2 * groups, HEAD_DIM)

    z_off = cw + 6 * kvw
    g_off = z_off + cw
    tn_in = 2560

    cmp_p, cmp_s, sel_p, sel_s, win_p, win_s = [], [], [], [], [], []
    pool_p, pool_s, gv_s = [], [], []
    for layer in range(depth):
        li = layer // 2
        if layer % 2 == 0:
            w_in = w_in_even[li].astype(BF16)
            w_out = w_out_even[li].astype(BF16)
            g_in = norm_even[li][None, :]
            vg = v_norm[li][None, :]
            ps = pool_scale[li][None, :]
            wp = w_pool[li].astype(BF16)
            tn = tn_in if w_in.shape[1] % tn_in == 0 else w_in.shape[1]
            proj = _norm_matmul(xp, g_in, w_in, tn=tn)
            act, pool16 = _even_mix(proj.reshape(bsz, seq, -1), vg, w_spatial[li], b_spatial[li].T, wp, ps)
            xp = _matmul_residual(act.reshape(bsz * seq, aw + bw), w_out, xp)
            pool_p.append(pool16[:, 16 - POOL_PAD:])
            proj_s = _norm_matmul(xs, g_in, w_in, tn=tn)
            hd = aw // A_HEADS
            w00 = jnp.repeat(w_spatial[li][:, 0, 0], hd)[None, :]
            b0 = jnp.repeat(b_spatial[li][:, 0], hd)[None, :]
            act_s, vn_s, new_state = _even_mix_sample(
                proj_s, vg, w00, b0, wp, ps, jnp.swapaxes(state_pool[li], 0, 1), past_len)
            xs = _matmul_residual(act_s, w_out, xs)
            pool_s.append(jnp.swapaxes(new_state, 0, 1))
            gv_s.append(vn_s.reshape(db, 1, aw))
        else:
            w_in = _odd_weight_layout(w_in_odd[li], cw, kvw, rep)
            w_out = w_out_odd[li].astype(BF16)
            g_in = norm_odd[li][None, :]
            qg = q_norm[li][None, :]
            kg = k_norm[li]
            tn = tn_in if w_in.shape[1] % tn_in == 0 else w_in.shape[1]
            proj = _norm_matmul(xp, g_in, w_in, tn=tn)
            q_b, kv_cmp, kv_sel, kv_win, kvb_sel, kvb_win, means = _odd_post(
                proj, cos_p, sin_p, qg, kg, cw=cw, kvw=kvw, with_means=True)
            r3 = lambda a: a.reshape(bsz, seq, a.shape[-1])
            act = _nsa_prompt(r3(q_b), r3(kvb_sel), r3(kvb_win), means.reshape(bsz, nb_p, 2 * kvw),
                              r3(proj), tables, cw=cw, z_off=z_off, g_off=g_off)
            xp = _matmul_residual(act.reshape(bsz * seq, cw), w_out, xp)
            kv6 = lambda a, n: a.reshape(n, -1, 2, groups, HEAD_DIM)
            cmp_p.append(kv6(kv_cmp, bsz))
            sel_p.append(kv6(kv_sel, bsz))
            win_keep = min(WINDOW, seq)
            win_p.append(kv6(kv_win, bsz)[:, seq - win_keep:])
            proj_s = _norm_matmul(xs, g_in, w_in, tn=tn)
            q_s, kvc_s, kvs_s, kvw_s, _, _ = _odd_post(
                proj_s, cos_s, sin_s, qg, kg, cw=cw, kvw=kvw, with_means=False)
            q3s = q_s.reshape(db, heads, HEAD_DIM)
            means_s = _page_means(cache_cmp5, page_table, li)
            o_c, idx = _nsa_sample_select(q3s, means_s, groups=groups, rep=rep, q_pos=past_len,
                                          n_past_sel=n_past_sel)
            act_s, win_new = _nsa_sample_attend(
                page_table, idx[:, :, :n_past_sel], q3s, cache_sel5, li,
                kvs_s.reshape(db, 1, 2 * kvw), kvw_s.reshape(db, 1, 2 * kvw), o_c, win_state5,
                proj_s[:, g_off:g_off + groups * LANES].reshape(db, 1, groups * LANES),
                proj_s[:, z_off:z_off + cw].reshape(db, 1, cw),
                groups=groups, rep=rep, n_gather=n_past_sel)
            xs = _matmul_residual(act_s.reshape(db, cw), w_out, xs)
            cmp_s.append(kv6(kvc_s, db))
            sel_s.append(kv6(kvs_s, db))
            win_s.append(win_new.reshape(db, n_buf, 2, groups, HEAD_DIM))
    return (xp.reshape(bsz, seq, d_model), xs.reshape(db, 1, d_model),
            jnp.stack(cmp_p), jnp.stack(cmp_s), jnp.stack(sel_p), jnp.stack(sel_s),
            jnp.stack(win_p), jnp.stack(win_s), jnp.stack(pool_p), jnp.stack(pool_s), jnp.stack(gv_s))
```

```python
import functools

import jax
import jax.numpy as jnp
import numpy as np
from jax import lax
from jax.experimental import pallas as pl
from jax.experimental.pallas import tpu as pltpu

F32 = jnp.float32
BF16 = jnp.bfloat16
I32 = jnp.int32

EPS = 1e-6
PAGE_SIZE = 128
A_HEADS = 8
GMLP_CHUNK = 128
POOL_WINDOWS = (2, 4, 8, 16)
POOL_PAD = max(POOL_WINDOWS) - 1
HEAD_DIM = 128
C_KV_HEADS = 4
CMP_BLOCK = 64
N_SEL = 16
WINDOW = 512
SEL_FORCE = 1e4
NEG_INF = -1e30
ROPE_THETA = 500000.0
ROT_DIM = HEAD_DIM // 4
LOG2E = 1.4426950408889634
Q_SCALE = HEAD_DIM ** -0.5 * LOG2E

LANES = 128
SUBLANES = 8
MIB = 1024 * 1024


def _cparams(vmem_mib, semantics=None):
    return pltpu.CompilerParams(vmem_limit_bytes=int(vmem_mib * MIB), dimension_semantics=semantics)


def _silu(x):
    return x * (1.0 / (1.0 + jnp.exp(-x)))


def _sigmoid(x):
    return 1.0 / (1.0 + jnp.exp(-x))


def _dot_nt(a, b):
    return lax.dot_general(a, b, (((1,), (1,)), ((), ())), preferred_element_type=F32)


def _dot(a, b):
    return jnp.dot(a, b, preferred_element_type=F32)


def _norm_mm_kernel(x_ref, g_ref, w_ref, o_ref):
    x = x_ref[...]
    ms = jnp.mean(x * x, axis=-1, keepdims=True)
    h = (x * lax.rsqrt(ms + EPS) * g_ref[...]).astype(BF16)
    o_ref[...] = jnp.dot(h, w_ref[...], preferred_element_type=F32)


def _mm_res_kernel(a_ref, w_ref, r_ref, o_ref):
    o_ref[...] = r_ref[...] + jnp.dot(a_ref[...], w_ref[...], preferred_element_type=F32)


def _row_tile(m, want):
    return want if m % want == 0 else m


def _norm_matmul(x, g, w, *, tn, tm=512):
    m, k = x.shape
    n = w.shape[1]
    tm = _row_tile(m, tm)
    assert n % tn == 0
    vmem = 2 * (tm * k * 4 + k * tn * 2 + tm * tn * 4) / MIB + 8
    return pl.pallas_call(
        _norm_mm_kernel,
        grid=(n // tn, m // tm),
        in_specs=[
            pl.BlockSpec((tm, k), lambda j, i: (i, 0)),
            pl.BlockSpec((1, k), lambda j, i: (0, 0)),
            pl.BlockSpec((k, tn), lambda j, i: (0, j)),
        ],
        out_specs=pl.BlockSpec((tm, tn), lambda j, i: (i, j)),
        out_shape=jax.ShapeDtypeStruct((m, n), F32),
        compiler_params=_cparams(vmem),
        name="norm_matmul",
    )(x, g, w)


def _matmul_residual(a, w, res, *, tm=512):
    m, k = a.shape
    n = w.shape[1]
    tm = _row_tile(m, tm)
    vmem = 2 * (tm * k * 2 + k * n * 2 + 2 * tm * n * 4) / MIB + 8
    return pl.pallas_call(
        _mm_res_kernel,
        grid=(m // tm,),
        in_specs=[
            pl.BlockSpec((tm, k), lambda i: (i, 0)),
            pl.BlockSpec((k, n), lambda i: (0, 0)),
            pl.BlockSpec((tm, n), lambda i: (i, 0)),
        ],
        out_specs=pl.BlockSpec((tm, n), lambda i: (i, 0)),
        out_shape=jax.ShapeDtypeStruct((m, n), F32),
        compiler_params=_cparams(vmem),
        name="matmul_residual",
    )(a, w, res)


def _even_mix_kernel(proj_ref, vg_ref, ws_ref, bst_ref, wp_ref, ps_ref, act_ref, pool_ref, ext_ref,
                     *, tm, aw, bw):
    c = pl.program_id(1)
    n_c = pl.num_programs(1)
    hd = aw // A_HEADS
    pg = bw // len(POOL_WINDOWS)

    row = lax.broadcasted_iota(I32, (tm, tm), 0)
    col = lax.broadcasted_iota(I32, (tm, tm), 1)
    causal = row >= col

    for h in range(A_HEADS):
        sl = slice(h * hd, (h + 1) * hd)
        u = proj_ref[0, :, sl]
        v = proj_ref[0, :, aw + h * hd:aw + (h + 1) * hd]
        za = proj_ref[0, :, 2 * aw + h * hd:2 * aw + (h + 1) * hd]
        ms = jnp.mean(v * v, axis=-1, keepdims=True)
        vn = v * lax.rsqrt(ms + EPS) * vg_ref[:, sl]
        w = jnp.where(causal, ws_ref[h], 0.0).astype(BF16)
        s = jnp.dot(w, vn.astype(BF16), preferred_element_type=F32) + bst_ref[:, h:h + 1]
        act_ref[0, :, sl] = ((u * s) * _silu(za)).astype(BF16)

    p_off = 3 * aw
    zb_off = 3 * aw + bw

    @pl.when(c == 0)
    def _():
        ext_ref[0:16, :] = jnp.zeros((16, bw), F32)

    ext_ref[16:16 + tm, :] = proj_ref[0, :, p_off:p_off + bw]
    pos = c * tm + lax.broadcasted_iota(I32, (tm, 1), 0)
    for g, wnd in enumerate(POOL_WINDOWS):
        sl = slice(g * pg, (g + 1) * pg)
        cur = ext_ref[16:16 + tm, sl]
        acc = cur
        for k in range(1, wnd):
            acc = acc + ext_ref[16 - k:16 - k + tm, sl]
        cnt = jnp.minimum(pos + 1, wnd).astype(F32)
        d = acc / cnt - cur
        y = jnp.dot(d.astype(BF16), wp_ref[g], preferred_element_type=F32)
        zb = proj_ref[0, :, zb_off + g * pg:zb_off + (g + 1) * pg]
        act_ref[0, :, aw + g * pg:aw + (g + 1) * pg] = ((y * ps_ref[:, sl]) * _silu(zb)).astype(BF16)

    tail = ext_ref[tm:tm + 16, :]
    ext_ref[0:16, :] = tail

    @pl.when(c == n_c - 1)
    def _():
        pool_ref[0] = tail


def _even_mix(proj3, v_gain, w_s, b_st, w_pool, pool_scale):
    b, s, _ = proj3.shape
    aw = v_gain.shape[1]
    bw = pool_scale.shape[1]
    tm = GMLP_CHUNK
    assert s % tm == 0 and s >= 16
    n_in = 3 * aw + 2 * bw
    kern = functools.partial(_even_mix_kernel, tm=tm, aw=aw, bw=bw)
    return pl.pallas_call(
        kern,
        grid=(b, s // tm),
        in_specs=[
            pl.BlockSpec((1, tm, n_in), lambda i, c: (i, c, 0)),
            pl.BlockSpec((1, aw), lambda i, c: (0, 0)),
            pl.BlockSpec(w_s.shape, lambda i, c: (0, 0, 0)),
            pl.BlockSpec(b_st.shape, lambda i, c: (0, 0)),
            pl.BlockSpec(w_pool.shape, lambda i, c: (0, 0, 0)),
            pl.BlockSpec((1, bw), lambda i, c: (0, 0)),
        ],
        out_specs=[
            pl.BlockSpec((1, tm, aw + bw), lambda i, c: (i, c, 0)),
            pl.BlockSpec((1, 16, bw), lambda i, c: (i, 0, 0)),
        ],
        out_shape=[
            jax.ShapeDtypeStruct((b, s, aw + bw), BF16),
            jax.ShapeDtypeStruct((b, 16, bw), F32),
        ],
        scratch_shapes=[pltpu.VMEM((16 + tm, bw), F32)],
        compiler_params=_cparams(32, ("arbitrary", "arbitrary")),
        name="even_mix",
    )(proj3, v_gain, w_s, b_st, w_pool, pool_scale)


def _even_mix_sample_kernel(proj_ref, vg_ref, w00_ref, b0_ref, wp_ref, ps_ref, st_ref,
                            act_ref, vn_ref, newst_ref, *, aw, bw, pos):
    hd = aw // A_HEADS
    pg = bw // len(POOL_WINDOWS)
    for h in range(A_HEADS):
        sl = slice(h * hd, (h + 1) * hd)
        u = proj_ref[:, sl]
        v = proj_ref[:, aw + h * hd:aw + (h + 1) * hd]
        za = proj_ref[:, 2 * aw + h * hd:2 * aw + (h + 1) * hd]
        ms = jnp.mean(v * v, axis=-1, keepdims=True)
        vn = v * lax.rsqrt(ms + EPS) * vg_ref[:, sl]
        vn_ref[:, sl] = vn
        s = w00_ref[:, sl] * vn + b0_ref[:, sl]
        act_ref[:, sl] = ((u * s) * _silu(za)).astype(BF16)

    p = proj_ref[:, 3 * aw:3 * aw + bw]
    for g, wnd in enumerate(POOL_WINDOWS):
        sl = slice(g * pg, (g + 1) * pg)
        cur = p[:, sl]
        acc = cur
        for k in range(1, wnd):
            acc = acc + st_ref[POOL_PAD - k, :, sl]
        cnt = float(min(pos + 1, wnd))
        d = acc / cnt - cur
        y = jnp.dot(d.astype(BF16), wp_ref[g], preferred_element_type=F32)
        zb = proj_ref[:, 3 * aw + bw + g * pg:3 * aw + bw + (g + 1) * pg]
        act_ref[:, aw + g * pg:aw + (g + 1) * pg] = ((y * ps_ref[:, sl]) * _silu(zb)).astype(BF16)

    for k in range(POOL_PAD - 1):
        newst_ref[k] = st_ref[k + 1]
    newst_ref[POOL_PAD - 1] = p


def _even_mix_sample(proj, v_gain, w00, b0, w_pool, pool_scale, state_t, pos):
    db = proj.shape[0]
    aw = v_gain.shape[1]
    bw = pool_scale.shape[1]
    kern = functools.partial(_even_mix_sample_kernel, aw=aw, bw=bw, pos=pos)
    return pl.pallas_call(
        kern,
        out_shape=[
            jax.ShapeDtypeStruct((db, aw + bw), BF16),
            jax.ShapeDtypeStruct((db, aw), F32),
            jax.ShapeDtypeStruct((POOL_PAD, db, bw), F32),
        ],
        name="even_mix_sample",
    )(proj, v_gain, w00, b0, w_pool, pool_scale, state_t)


def _odd_post_kernel(proj_ref, cos_ref, sin_ref, qg_ref, kg_ref, *out_refs, tm, cw, kvw, for_prompt):
    q_out, cmp_out, sel_out, win_out = out_refs[:4]
    if for_prompt:
        selkt_out, selv_out, winkt_out, winv_out, means_out = out_refs[4:]
        attn_outs = (None, (selkt_out, selv_out), (winkt_out, winv_out))
    else:
        attn_outs = (None, None, None)
    cosf = cos_ref[...]
    sinf = sin_ref[...]
    half = ROT_DIM // 2
    first = lax.broadcasted_iota(I32, (tm, HEAD_DIM), 1) < half

    def norm_rope(x, gain):
        ms = jnp.mean(x * x, axis=-1, keepdims=True)
        y = x * lax.rsqrt(ms + EPS) * gain
        rot = jnp.where(first, pltpu.roll(y, HEAD_DIM - half, 1), pltpu.roll(y, half, 1))
        return y * cosf + rot * sinf

    qg = qg_ref[...]
    for h in range(cw // HEAD_DIM):
        sl = slice(h * HEAD_DIM, (h + 1) * HEAD_DIM)
        q_out[:, sl] = (norm_rope(proj_ref[:, sl], qg) * Q_SCALE).astype(BF16)

    for br, (o32, attn) in enumerate(zip((cmp_out, sel_out, win_out), attn_outs)):
        k_off = cw + br * 2 * kvw
        v_off = k_off + kvw
        kg = kg_ref[br:br + 1, :]
        pieces = []
        for g in range(kvw // HEAD_DIM):
            sl = slice(g * HEAD_DIM, (g + 1) * HEAD_DIM)
            kk = norm_rope(proj_ref[:, k_off + g * HEAD_DIM:k_off + (g + 1) * HEAD_DIM], kg)
            pieces.append(kk)
            if attn is not None:
                attn[0][sl, :] = kk.T.astype(BF16)
        vv = proj_ref[:, v_off:v_off + kvw]
        if attn is not None:
            attn[1][...] = vv.astype(BF16)
        row = jnp.concatenate(pieces + [vv], axis=1)
        o32[...] = row.reshape(o32.shape)
        if br == 0 and for_prompt:
            for j in range(tm // CMP_BLOCK):
                blk = row[j * CMP_BLOCK:(j + 1) * CMP_BLOCK, :]
                means_out[0, j:j + 1, :] = jnp.mean(blk, axis=0, keepdims=True)


def _odd_post(proj, cos_t, sin_t, q_gain, k_gain, *, cw, kvw, for_prompt, tm=256):
    m = proj.shape[0]
    tm = _row_tile(m, tm)
    n_pos_tiles = cos_t.shape[0] // tm
    n_used = cw + 6 * kvw
    kern = functools.partial(_odd_post_kernel, tm=tm, cw=cw, kvw=kvw, for_prompt=for_prompt)
    row_spec = lambda w: pl.BlockSpec((tm, w), lambda i: (i, 0))
    n_rows = 2 * kvw // HEAD_DIM
    if for_prompt:
        kv_shape = jax.ShapeDtypeStruct((m, n_rows, HEAD_DIM), F32)
        kv_spec = pl.BlockSpec((tm, n_rows, HEAD_DIM), lambda i: (i, 0, 0))
    else:
        kv_shape = jax.ShapeDtypeStruct((m, 2 * kvw), F32)
        kv_spec = row_spec(2 * kvw)
    out_shape = [jax.ShapeDtypeStruct((m, cw), BF16)] + [kv_shape] * 3
    out_specs = [row_spec(cw)] + [kv_spec] * 3
    if for_prompt:
        assert tm % CMP_BLOCK == 0 and tm % LANES == 0
        for _ in range(2):
            out_shape += [jax.ShapeDtypeStruct((kvw, m), BF16), jax.ShapeDtypeStruct((m, kvw), BF16)]
            out_specs += [pl.BlockSpec((kvw, tm), lambda i: (0, i)), row_spec(kvw)]
        out_shape.append(jax.ShapeDtypeStruct((m // tm, tm // CMP_BLOCK, 2 * kvw), F32))
        out_specs.append(pl.BlockSpec((1, tm // CMP_BLOCK, 2 * kvw), lambda i: (i, 0, 0)))
    return pl.pallas_call(
        kern,
        grid=(m // tm,),
        in_specs=[
            pl.BlockSpec((tm, n_used), lambda i: (i, 0)),
            pl.BlockSpec((tm, HEAD_DIM), lambda i: (i % n_pos_tiles, 0)),
            pl.BlockSpec((tm, HEAD_DIM), lambda i: (i % n_pos_tiles, 0)),
            pl.BlockSpec((1, HEAD_DIM), lambda i: (0, 0)),
            pl.BlockSpec((3, HEAD_DIM), lambda i: (0, 0)),
        ],
        out_specs=out_specs,
        out_shape=out_shape,
        compiler_params=_cparams(40),
        name="odd_post",
    )(proj, cos_t, sin_t, q_gain, k_gain)


def _nsa_sub_tile(i, h, q_ref, ks_ref, vs_ref, kw_ref, vw_ref, kc, vc, band_ref, tri_ref, eye_ref,
                  *, tq, nb, rep, lw):
    rows = rep * tq
    r0 = h * tq
    q = jnp.concatenate([q_ref[0, r0:r0 + tq, r * HEAD_DIM:(r + 1) * HEAD_DIM] for r in range(rep)],
                        axis=0)
    t_col = i * tq + lax.broadcasted_iota(I32, (tq, 1), 0)
    t_row = i * tq + lax.broadcasted_iota(I32, (1, tq), 1)

    blk_row = lax.broadcasted_iota(I32, (1, nb), 1)
    s_c = _dot_nt(q, kc).reshape(rep, tq, nb)
    ok_c = ((blk_row + 1) * CMP_BLOCK - 1) <= t_col
    s_c = s_c + jnp.where(ok_c, 0.0, NEG_INF)[None]
    m_c = jnp.max(s_c, axis=-1, keepdims=True)
    e_c = jnp.exp2(s_c - m_c)
    p_c = e_c / jnp.sum(e_c, axis=-1, keepdims=True) * jnp.where(ok_c, 1.0, 0.0)[None]
    o_c = jnp.dot(p_c.reshape(rows, nb).astype(BF16), vc, preferred_element_type=F32)

    blk_col = lax.broadcasted_iota(I32, (nb, 1), 0)
    s_t = _dot_nt(kc, q)
    ok_t = ((blk_col + 1) * CMP_BLOCK - 1) <= t_row
    bias_t = jnp.where(ok_t, 0.0, NEG_INF)
    okf_t = jnp.where(ok_t, 1.0, 0.0)
    imp = jnp.zeros((nb, tq), F32)
    for r in range(rep):
        s_r = s_t[:, r * tq:(r + 1) * tq] + bias_t
        e_r = jnp.exp2(s_r - jnp.max(s_r, axis=0, keepdims=True))
        imp = imp + e_r / jnp.sum(e_r, axis=0, keepdims=True) * okf_t
    cur = t_row // CMP_BLOCK
    forced = (blk_col == 0) | (blk_col == cur) | (blk_col == cur - 1)
    imp = jnp.where(blk_col <= cur, jnp.where(forced, SEL_FORCE, imp), -1.0)

    blk8 = lax.broadcasted_iota(I32, (SUBLANES, 1), 0)
    cnt = jnp.zeros((nb, tq), F32)
    for j in range(nb):
        rj = imp[j:j + 1, :]
        parts = []
        for v0 in range(0, nb, SUBLANES):
            sub = imp[v0:v0 + SUBLANES, :]
            if v0 > j:
                beats = rj >= sub
            elif v0 + SUBLANES - 1 <= j:
                beats = rj > sub
            else:
                beats = (rj > sub) | ((rj == sub) & (blk8 > j - v0))
            parts.append(jnp.where(beats, 1.0, 0.0))
        cnt = cnt + jnp.concatenate(parts, axis=0)
    keep = (cnt < min(N_SEL, nb)) & (imp >= 0.0) & (blk_col < i * (tq // CMP_BLOCK))
    drop_t = jnp.where(keep, 0.0, 1.0)
    if LANES > nb:
        drop_t = jnp.concatenate([drop_t, jnp.zeros((LANES - nb, tq), F32)], axis=0)
    drop = drop_t.T.astype(BF16)
    q_drop = jnp.concatenate([q, jnp.concatenate([drop] * rep, axis=0)], axis=1)

    q_eye = jnp.concatenate([q, eye_ref[...]], axis=1)

    w0 = pl.multiple_of(jnp.maximum(i * tq + tq - lw, 0), tq)
    band = band_ref[jnp.minimum(i, band_ref.shape[0] - 1)]
    k_w = jnp.concatenate([kw_ref[:, pl.ds(w0, lw)], band], axis=0)
    v_w = jnp.concatenate([vw_ref[0, pl.ds(w0, lw), :], jnp.ones((lw, HEAD_DIM), BF16)], axis=1)
    s_w = _dot(q_eye, k_w)
    p_w = jnp.exp2(s_w - jnp.max(s_w, axis=-1, keepdims=True))
    acc_w = _dot(p_w.astype(BF16), v_w)
    o_w = acc_w[:, :HEAD_DIM] / acc_w[:, HEAD_DIM:]

    d0 = pl.multiple_of(i * tq, tq)
    k_d = jnp.concatenate([ks_ref[:, pl.ds(d0, tq)], tri_ref[...]], axis=0)
    v_d = jnp.concatenate([vs_ref[0, pl.ds(d0, tq), :], jnp.ones((tq, HEAD_DIM), BF16)], axis=1)
    s_d = _dot(q_eye, k_d)
    m_d = jnp.max(s_d, axis=-1, keepdims=True)
    acc_d = _dot(jnp.exp2(s_d - m_d).astype(BF16), v_d)
    return q_drop, o_c, o_w, m_d, acc_d


def _nsa_prompt_kernel(q_ref, ks_ref, vs_ref, kw_ref, vw_ref, kc_ref, vc_ref, g_ref, z_ref,
                       eb_ref, band_ref, tri_ref, eye_ref, o_ref, s0_ref, s1_ref,
                       *, tq, nh, tk, seq, nb, rep, lw):
    step = pl.program_id(2)
    kc = kc_ref[0].astype(BF16)
    vc = vc_ref[0].astype(BF16)
    subs = [_nsa_sub_tile(step * nh + h, h, q_ref, ks_ref, vs_ref, kw_ref, vw_ref, kc, vc, band_ref,
                          tri_ref, eye_ref, tq=tq, nb=nb, rep=rep, lw=lw) for h in range(nh)]

    q_drop = jnp.concatenate([s[0] for s in subs], axis=0)
    m_0 = jnp.concatenate([s[3] for s in subs], axis=0)
    acc_0 = jnp.concatenate([s[4] for s in subs], axis=0)

    partial, gate_s, z_act = [], [], []
    for h in range(nh):
        ts = slice(h * tq, (h + 1) * tq)
        gs = _sigmoid(g_ref[0, ts, :])
        o_c, o_w = subs[h][1], subs[h][2]
        for r in range(rep):
            rs = slice(r * tq, (r + 1) * tq)
            partial.append(gs[:, 3 * r:3 * r + 1] * o_c[rs] + gs[:, 3 * r + 2:3 * r + 3] * o_w[rs])
            gate_s.append(gs[:, 3 * r + 1:3 * r + 2])
            z_act.append(_silu(z_ref[0, ts, r * HEAD_DIM:(r + 1) * HEAD_DIM]))

    ones_k = jnp.ones((tk, HEAD_DIM), BF16)
    n_tiles = seq // tk

    def tile_scores(t):
        k0 = pl.multiple_of(t * tk, tk)
        k = jnp.concatenate([ks_ref[:, pl.ds(k0, tk)], eb_ref[:, pl.ds(k0, tk)]], axis=0)
        return _dot(q_drop, k)

    def accumulate(carry, s, t):
        m_p, acc = carry
        k0 = pl.multiple_of(t * tk, tk)
        v = jnp.concatenate([vs_ref[0, pl.ds(k0, tk), :], ones_k], axis=1)
        m_n = jnp.maximum(m_p, jnp.max(s, axis=-1, keepdims=True))
        pv = _dot(jnp.exp2(s - m_n).astype(BF16), v)
        return m_n, jnp.exp2(m_p - m_n) * acc + pv

    def sel_body(j, carry):
        t1 = 2 * j + 1
        s1_ref[...] = tile_scores(t1)
        carry = accumulate(carry, s0_ref[...], 2 * j)
        s0_ref[...] = tile_scores(jnp.minimum(t1 + 1, n_tiles - 1))
        return accumulate(carry, s1_ref[...], t1)

    n_past = ((step * nh + nh - 1) * tq + tk - 1) // tk
    s0_ref[...] = tile_scores(0)
    _, acc_s = lax.fori_loop(0, (n_past + 1) // 2, sel_body, (m_0, acc_0))
    o_s = acc_s[:, :HEAD_DIM] / acc_s[:, HEAD_DIM:]

    for h in range(nh):
        for r in range(rep):
            n = h * rep + r
            o = partial[n] + gate_s[n] * o_s[n * tq:(n + 1) * tq]
            o_ref[0, h * tq:(h + 1) * tq, r * HEAD_DIM:(r + 1) * HEAD_DIM] = (o * z_act[n]).astype(BF16)


def _mask_tables(seq, tq, lw, rep):
    neg = lambda ok: jnp.where(ok, 0.0, NEG_INF).astype(BF16)
    row = jnp.arange(LANES, dtype=I32)[:, None]
    key = jnp.arange(seq, dtype=I32)[None, :]
    block_rows = neg(key // CMP_BLOCK != row)
    t = jnp.arange(tq, dtype=I32)[None, :, None]
    c = jnp.arange(lw, dtype=I32)[None, None, :]
    off = jnp.arange(WINDOW // tq + 1, dtype=I32)[:, None, None] * tq
    rel = off + t - c
    band = neg((rel >= 0) & (rel < WINDOW))
    tri = neg(jnp.arange(tq, dtype=I32)[None, :] <= jnp.arange(tq, dtype=I32)[:, None])
    eye = jnp.tile(jnp.eye(tq, dtype=BF16), (rep, 1))
    return block_rows, band, tri, eye


def _nsa_prompt(q3, kt_sel, v_sel, kt_win, v_win, means3, proj3, tables, *, cw, z_off, g_off, tq=LANES,
                nh=2, tk=512):
    b, s, _ = q3.shape
    kvw = v_sel.shape[2]
    groups = kvw // HEAD_DIM
    rep = cw // kvw
    nb = means3.shape[1]
    gw = rep * HEAD_DIM
    block_rows, band, tri, eye = tables
    tk = min(tk, s)
    lw = band.shape[2]
    ts = nh * tq
    assert tq == LANES and nb <= LANES and nb % SUBLANES == 0 and lw == WINDOW + tq and s >= lw
    assert s % ts == 0 and s % (2 * tk) == 0 and tk % tq == 0 and tq % CMP_BLOCK == 0
    assert nb * CMP_BLOCK == s
    assert z_off % gw == 0 and g_off % LANES == 0
    kern = functools.partial(_nsa_prompt_kernel, tq=tq, nh=nh, tk=tk, seq=s, nb=nb, rep=rep, lw=lw)
    kt_spec = pl.BlockSpec((HEAD_DIM, s), lambda bi, g, i: (g, bi))
    v_spec = pl.BlockSpec((1, s, HEAD_DIM), lambda bi, g, i: (bi, 0, g))
    mean_spec = lambda off: pl.BlockSpec((1, nb, HEAD_DIM), lambda bi, g, i: (bi, 0, off + g))
    return pl.pallas_call(
        kern,
        grid=(b, groups, s // ts),
        in_specs=[
            pl.BlockSpec((1, ts, gw), lambda bi, g, i: (bi, i, g)),
            kt_spec, v_spec, kt_spec, v_spec,
            mean_spec(0), mean_spec(groups),
            pl.BlockSpec((1, ts, LANES), lambda bi, g, i: (bi, i, g_off // LANES + g)),
            pl.BlockSpec((1, ts, gw), lambda bi, g, i: (bi, i, z_off // gw + g)),
            pl.BlockSpec(block_rows.shape, lambda bi, g, i: (0, 0)),
            pl.BlockSpec(band.shape, lambda bi, g, i: (0, 0, 0)),
            pl.BlockSpec(tri.shape, lambda bi, g, i: (0, 0)),
            pl.BlockSpec(eye.shape, lambda bi, g, i: (0, 0)),
        ],
        out_specs=pl.BlockSpec((1, ts, gw), lambda bi, g, i: (bi, i, g)),
        out_shape=jax.ShapeDtypeStruct((b, s, cw), BF16),
        scratch_shapes=[pltpu.VMEM((nh * rep * tq, tk), F32), pltpu.VMEM((nh * rep * tq, tk), F32)],
        compiler_params=_cparams(48),
        name="nsa_prompt",
    )(q3, kt_sel, v_sel, kt_win, v_win, means3, means3, proj3, proj3, block_rows, band, tri, eye)


def _page_means_kernel(pt_ref, *refs, pps):
    out_ref = refs[pps]
    bpp = PAGE_SIZE // CMP_BLOCK
    n_rows = out_ref.shape[1]
    for r in range(pps):
        for h in range(bpp):
            mean = jnp.mean(refs[r][h * CMP_BLOCK:(h + 1) * CMP_BLOCK], axis=0)
            n = r * bpp + h
            for j in range(n_rows):
                out_ref[0, j, n:n + 1, :] = mean[j:j + 1, :]


def _page_means(cache5, page_table, layer, *, pps=8):
    db, n_pages = page_table.shape
    n_rows, d = cache5.shape[3], cache5.shape[4]
    pps = pps if n_pages % pps == 0 else 1
    bpp = PAGE_SIZE // CMP_BLOCK

    def page_spec(r):
        return pl.BlockSpec((None, None, PAGE_SIZE, n_rows, d),
                            lambda bi, j, pt: (layer, pt[bi, j * pps + r], 0, 0, 0))

    grid_spec = pltpu.PrefetchScalarGridSpec(
        num_scalar_prefetch=1,
        grid=(db, n_pages // pps),
        in_specs=[page_spec(r) for r in range(pps)],
        out_specs=pl.BlockSpec((1, n_rows, pps * bpp, d), lambda bi, j, pt: (bi, 0, j, 0)),
    )
    return pl.pallas_call(
        functools.partial(_page_means_kernel, pps=pps),
        grid_spec=grid_spec,
        out_shape=jax.ShapeDtypeStruct((db, n_rows, n_pages * bpp, d), F32),
        compiler_params=_cparams(24),
        name="page_means",
    )(page_table, *([cache5] * pps))


def _nsa_sample_select_kernel(q_ref, mean_ref, oc_ref, idx_ref, *, nbp, groups, rep, q_pos, n_past_sel):
    q = q_ref[0]
    heads = groups * rep
    lane = lax.broadcasted_iota(I32, (1, nbp), 1)
    ri = lax.broadcasted_iota(I32, (nbp, nbp), 0)
    ci = lax.broadcasted_iota(I32, (nbp, nbp), 1)
    diag = ri == ci
    ok = ((lane + 1) * CMP_BLOCK - 1) <= q_pos
    cur = q_pos // CMP_BLOCK
    forced = (lane == 0) | (lane == cur) | (lane == cur - 1)
    hrow = lax.broadcasted_iota(I32, (heads, 1), 0) // rep
    out_lane = lax.broadcasted_iota(I32, (1, LANES), 1)
    o_c = jnp.zeros((heads, HEAD_DIM), F32)
    for g in range(groups):
        kc = mean_ref[0, g].astype(BF16)
        vc = mean_ref[0, groups + g].astype(BF16)
        s = _dot_nt(q, kc) + jnp.where(ok, 0.0, NEG_INF)
        e = jnp.exp2(s - jnp.max(s, axis=-1, keepdims=True))
        p = e / jnp.sum(e, axis=-1, keepdims=True) * jnp.where(ok, 1.0, 0.0)
        in_g = hrow == g
        o_c = o_c + jnp.where(in_g, jnp.dot(p.astype(BF16), vc, preferred_element_type=F32), 0.0)
        imp = jnp.sum(jnp.where(in_g, p, 0.0), axis=0, keepdims=True)
        imp = jnp.where(lane <= cur, jnp.where(forced, SEL_FORCE, imp), -1.0)
        imp_b = jnp.broadcast_to(imp, (nbp, nbp))
        imp_col = jnp.sum(jnp.where(diag, imp_b, 0.0), axis=1, keepdims=True)
        beats = (imp_col > imp_b) | ((imp_col == imp_b) & (ri < ci))
        cnt = jnp.sum(jnp.where(beats, 1.0, 0.0), axis=0, keepdims=True)
        sel = jnp.where((cnt < n_past_sel) & (imp >= 0.0), 1.0, 0.0)
        sel_col = jnp.sum(jnp.where(diag, jnp.broadcast_to(sel, (nbp, nbp)), 0.0), axis=1, keepdims=True)
        before = jnp.sum(jnp.where(ri < ci, sel_col, 0.0), axis=0, keepdims=True)
        row = jnp.zeros((1, LANES), F32)
        for k in range(n_past_sel):
            hit = (sel > 0.5) & (before == float(k))
            idx_k = jnp.sum(jnp.where(hit, lane.astype(F32), 0.0), axis=1, keepdims=True)
            row = jnp.where(out_lane == k, idx_k, row)
        idx_ref[0, g:g + 1, :] = row.astype(I32)
    oc_ref[0] = o_c


def _nsa_sample_select(q3, means, *, groups, rep, q_pos, n_past_sel):
    db, heads, _ = q3.shape
    nbp = means.shape[2]
    kern = functools.partial(_nsa_sample_select_kernel, nbp=nbp, groups=groups, rep=rep, q_pos=q_pos,
                             n_past_sel=n_past_sel)
    return pl.pallas_call(
        kern,
        grid=(db,),
        in_specs=[
            pl.BlockSpec((1, heads, HEAD_DIM), lambda bi: (bi, 0, 0)),
            pl.BlockSpec((1,) + means.shape[1:], lambda bi: (bi, 0, 0, 0)),
        ],
        out_specs=[
            pl.BlockSpec((1, heads, HEAD_DIM), lambda bi: (bi, 0, 0)),
            pl.BlockSpec((1, groups, LANES), lambda bi: (bi, 0, 0)),
        ],
        out_shape=[
            jax.ShapeDtypeStruct((db, heads, HEAD_DIM), F32),
            jax.ShapeDtypeStruct((db, groups, LANES), I32),
        ],
        compiler_params=_cparams(24),
        name="nsa_sample_select",
    )(q3, means)


def _nsa_sample_attend_kernel(pt_ref, si_ref, q_ref, *refs, groups, rep, n_steps, per_step, n_buf):
    kv_refs = refs[:groups * per_step]
    (newsel_ref, newwin_ref, oc_ref, win_ref, g_ref, z_ref,
     act_ref, winout_ref, m_ref, l_ref, acc_ref) = refs[groups * per_step:]
    kvw = groups * HEAD_DIM
    heads = groups * rep
    n_rows = 2 * groups
    k_id = pl.program_id(1)
    q = q_ref[0]
    qf = q.astype(F32)
    hgrp = lax.broadcasted_iota(I32, (heads, 1), 0) // rep

    def own_key_rows(n_tok):
        lane = lax.broadcasted_iota(I32, (1, n_tok * n_rows), 1)
        return lane % n_rows == hgrp, lane // n_rows

    def by_group(fn):
        out = None
        for g in range(groups):
            val = jnp.where(hgrp == g, fn(g), 0.0)
            out = val if out is None else out + val
        return out

    def new_token_scores(row_ref):
        return by_group(lambda g: jnp.sum(
            qf * row_ref[0, :, g * HEAD_DIM:(g + 1) * HEAD_DIM], axis=-1, keepdims=True))

    def new_token_values(row_ref):
        return by_group(lambda g: jnp.broadcast_to(
            row_ref[0, :, kvw + g * HEAD_DIM:kvw + (g + 1) * HEAD_DIM], (heads, HEAD_DIM)))

    @pl.when(k_id == 0)
    def _():
        m_ref[...] = jnp.full((heads, 1), NEG_INF, F32)
        l_ref[...] = jnp.zeros((heads, 1), F32)
        acc_ref[...] = jnp.zeros((heads, HEAD_DIM), F32)

    xs = [jnp.concatenate([kv_refs[g * per_step + u][...].reshape(CMP_BLOCK * n_rows, HEAD_DIM)
                           for u in range(per_step)], axis=0).astype(BF16) for g in range(groups)]
    own, _ = own_key_rows(CMP_BLOCK * per_step)
    s = by_group(lambda g: _dot_nt(q, xs[g])) + jnp.where(own, 0.0, NEG_INF)
    m_p = m_ref[...]
    m_n = jnp.maximum(m_p, jnp.max(s, axis=-1, keepdims=True))
    alpha = jnp.exp2(m_p - m_n)
    p = jnp.exp2(s - m_n)
    pb = pltpu.roll(p, groups, 1).astype(BF16)
    pv = by_group(lambda g: jnp.dot(pb, xs[g], preferred_element_type=F32))
    m_ref[...] = m_n
    l_ref[...] = alpha * l_ref[...] + jnp.sum(p, axis=-1, keepdims=True)
    acc_ref[...] = alpha * acc_ref[...] + pv

    @pl.when(k_id == n_steps - 1)
    def _():
        s_n = new_token_scores(newsel_ref)
        m_p = m_ref[...]
        m_n = jnp.maximum(m_p, s_n)
        alpha = jnp.exp2(m_p - m_n)
        p_n = jnp.exp2(s_n - m_n)
        l_s = alpha * l_ref[...] + p_n
        o_s = (alpha * acc_ref[...] + p_n * new_token_values(newsel_ref)) / l_s

        xw = win_ref[0].reshape(n_buf * n_rows, HEAD_DIM).astype(BF16)
        own_w, tok_w = own_key_rows(n_buf)
        ok_w = own_w & ((n_buf - tok_w) < WINDOW)
        s_w = _dot_nt(q, xw) + jnp.where(ok_w, 0.0, NEG_INF)
        s_wn = new_token_scores(newwin_ref)
        m_w = jnp.maximum(jnp.max(s_w, axis=-1, keepdims=True), s_wn)
        p_w = jnp.exp2(s_w - m_w)
        p_wn = jnp.exp2(s_wn - m_w)
        l_w = jnp.sum(p_w, axis=-1, keepdims=True) + p_wn
        o_w = jnp.dot(pltpu.roll(p_w, groups, 1).astype(BF16), xw, preferred_element_type=F32)
        o_w = (o_w + p_wn * new_token_values(newwin_ref)) / l_w

        winout_ref[0, 0:n_buf - 1] = win_ref[0, 1:n_buf]
        for j in range(n_rows):
            winout_ref[0, n_buf - 1, j:j + 1, :] = newwin_ref[0, :, j * HEAD_DIM:(j + 1) * HEAD_DIM]

        graw = jnp.broadcast_to(g_ref[0], (heads, groups * LANES))
        glane = lax.broadcasted_iota(I32, (heads, groups * LANES), 1)
        hidx = lax.broadcasted_iota(I32, (heads, 1), 0)
        gbase = (hidx // rep) * LANES + 3 * (hidx % rep)
        gate = lambda c: _sigmoid(jnp.sum(jnp.where(glane == gbase + c, graw, 0.0), axis=-1, keepdims=True))
        o = gate(0) * oc_ref[0] + gate(1) * o_s + gate(2) * o_w
        for h in range(heads):
            z = z_ref[0, :, h * HEAD_DIM:(h + 1) * HEAD_DIM]
            act_ref[0, :, h * HEAD_DIM:(h + 1) * HEAD_DIM] = (o[h:h + 1, :] * _silu(z)).astype(BF16)


def _nsa_sample_attend(page_table, sel_idx, q3, cache5, layer, new_sel, new_win, o_c, win_state, gates3,
                       z3, *, groups, rep, n_gather):
    db, heads, _ = q3.shape
    kvw = groups * HEAD_DIM
    n_buf = win_state.shape[2]
    n_rows = 2 * groups
    bpp = PAGE_SIZE // CMP_BLOCK
    assert sel_idx.shape[2] == n_gather
    per_step = max(u for u in range(1, 6) if n_gather % u == 0)
    n_steps = n_gather // per_step

    def gather_spec(g, u):
        def index_map(bi, k, pt, si):
            blk = si[bi, g, k * per_step + u]
            return (layer, pt[bi, blk // bpp], blk % bpp, 0, 0)
        return pl.BlockSpec((None, None, CMP_BLOCK, n_rows, HEAD_DIM), index_map)

    in_specs = [pl.BlockSpec((1, heads, HEAD_DIM), lambda bi, k, pt, si: (bi, 0, 0))]
    in_specs += [gather_spec(g, u) for g in range(groups) for u in range(per_step)]
    row3 = lambda w: pl.BlockSpec((1, 1, w), lambda bi, k, pt, si: (bi, 0, 0))
    in_specs += [
        row3(2 * kvw), row3(2 * kvw),
        pl.BlockSpec((1, heads, HEAD_DIM), lambda bi, k, pt, si: (bi, 0, 0)),
        pl.BlockSpec((None, 1, n_buf, n_rows, HEAD_DIM), lambda bi, k, pt, si: (layer, bi, 0, 0, 0)),
        row3(groups * LANES), row3(heads * HEAD_DIM),
    ]
    grid_spec = pltpu.PrefetchScalarGridSpec(
        num_scalar_prefetch=2,
        grid=(db, n_steps),
        in_specs=in_specs,
        out_specs=[
            row3(heads * HEAD_DIM),
            pl.BlockSpec((1, n_buf, n_rows, HEAD_DIM), lambda bi, k, pt, si: (bi, 0, 0, 0)),
        ],
        scratch_shapes=[pltpu.VMEM((heads, 1), F32), pltpu.VMEM((heads, 1), F32),
                        pltpu.VMEM((heads, HEAD_DIM), F32)],
    )
    kern = functools.partial(_nsa_sample_attend_kernel, groups=groups, rep=rep, n_steps=n_steps,
                             per_step=per_step, n_buf=n_buf)
    return pl.pallas_call(
        kern,
        grid_spec=grid_spec,
        out_shape=[
            jax.ShapeDtypeStruct((db, 1, heads * HEAD_DIM), BF16),
            jax.ShapeDtypeStruct((db, n_buf, n_rows, HEAD_DIM), F32),
        ],
        compiler_params=_cparams(32, ("arbitrary", "arbitrary")),
        name="nsa_sample_attend",
    )(page_table, sel_idx, q3, *([cache5] * (groups * per_step)), new_sel, new_win, o_c, win_state,
      gates3, z3)


def _rope_tables(pos):
    half = ROT_DIM // 2
    inv_freq = ROPE_THETA ** (-jnp.arange(half, dtype=F32) * (2.0 / ROT_DIM))
    ang = pos.astype(F32)[:, None] * inv_freq[None, :]
    cos, sin = jnp.cos(ang), jnp.sin(ang)
    n = pos.shape[0]
    cos_t = jnp.concatenate([cos, cos, jnp.ones((n, HEAD_DIM - ROT_DIM), F32)], axis=1)
    sin_t = jnp.concatenate([-sin, sin, jnp.zeros((n, HEAD_DIM - ROT_DIM), F32)], axis=1)
    return cos_t, sin_t


def _odd_weight_layout(w, cw, kvw, rep):
    k = w.shape[0]
    groups = kvw // HEAD_DIM
    n_qkv = cw + 6 * kvw
    n_gate = 3 * groups * rep
    gates = w[:, n_qkv:n_qkv + n_gate].reshape(k, groups, 3 * rep)
    gates = jnp.pad(gates, ((0, 0), (0, 0), (0, LANES - 3 * rep))).reshape(k, groups * LANES)
    return jnp.concatenate([w[:, :n_qkv], w[:, n_qkv + n_gate:], gates], axis=1).astype(BF16)


def kernel(x_prompt, x_sample, cache_cmp_kv, cache_sel_kv, state_win_kv, state_pool, page_table,
           norm_even, w_in_even, v_norm, w_spatial, b_spatial, w_pool, pool_scale, w_out_even,
           norm_odd, w_in_odd, q_norm, k_norm, w_out_odd):
    bsz, seq, d_model = x_prompt.shape
    db, dec_t, _ = x_sample.shape
    assert dec_t == 1
    n_even, n_odd = norm_even.shape[0], norm_odd.shape[0]
    depth = n_even + n_odd
    n_pages = page_table.shape[1]
    past_len = n_pages * PAGE_SIZE
    aw = v_norm.shape[1]
    bw = pool_scale.shape[1]
    groups = C_KV_HEADS
    kvw = groups * HEAD_DIM
    cw = w_out_odd.shape[1]
    rep = cw // kvw
    heads = cw // HEAD_DIM
    n_phys = cache_cmp_kv.shape[1]
    n_buf = state_win_kv.shape[2]
    assert past_len % CMP_BLOCK == 0 and seq % CMP_BLOCK == 0
    nb_p = seq // CMP_BLOCK
    nb_past = past_len // CMP_BLOCK
    n_past_sel = min(N_SEL, nb_past + 1) - 1
    assert nb_past >= 2 and n_past_sel >= 2

    xp = x_prompt.reshape(bsz * seq, d_model)
    xs = x_sample.reshape(db, d_model)

    cos_p, sin_p = _rope_tables(jnp.arange(seq, dtype=I32))
    cos_s, sin_s = _rope_tables(jnp.full((db,), past_len, dtype=I32))
    tables = _mask_tables(seq, LANES, WINDOW + LANES, rep)

    cache_cmp5 = cache_cmp_kv.reshape(n_odd, n_phys, PAGE_SIZE, 2 * groups, HEAD_DIM)
    cache_sel5 = cache_sel_kv.reshape(n_odd, n_phys, PAGE_SIZE, 2 * groups, HEAD_DIM)
    win_state5 = state_win_kv.reshape(n_odd, db, n_buf, 2 * groups, HEAD_DIM)

    z_off = cw + 6 * kvw
    g_off = z_off + cw
    tn_in = 2560

    cmp_p, cmp_s, sel_p, sel_s, win_p, win_s = [], [], [], [], [], []
    pool_p, pool_s, gv_s = [], [], []
    for layer in range(depth):
        li = layer // 2
        if layer % 2 == 0:
            w_in = w_in_even[li].astype(BF16)
            w_out = w_out_even[li].astype(BF16)
            g_in = norm_even[li][None, :]
            vg = v_norm[li][None, :]
            ps = pool_scale[li][None, :]
            wp = w_pool[li].astype(BF16)
            tn = tn_in if w_in.shape[1] % tn_in == 0 else w_in.shape[1]
            proj = _norm_matmul(xp, g_in, w_in, tn=tn)
            act, pool16 = _even_mix(proj.reshape(bsz, seq, -1), vg, w_spatial[li], b_spatial[li].T, wp, ps)
            xp = _matmul_residual(act.reshape(bsz * seq, aw + bw), w_out, xp)
            pool_p.append(pool16[:, 16 - POOL_PAD:])
            proj_s = _norm_matmul(xs, g_in, w_in, tn=tn)
            hd = aw // A_HEADS
            w00 = jnp.repeat(w_spatial[li][:, 0, 0], hd)[None, :]
            b0 = jnp.repeat(b_spatial[li][:, 0], hd)[None, :]
            act_s, vn_s, new_state = _even_mix_sample(
                proj_s, vg, w00, b0, wp, ps, jnp.swapaxes(state_pool[li], 0, 1), past_len)
            xs = _matmul_residual(act_s, w_out, xs)
            pool_s.append(jnp.swapaxes(new_state, 0, 1))
            gv_s.append(vn_s.reshape(db, 1, aw))
        else:
            w_in = _odd_weight_layout(w_in_odd[li], cw, kvw, rep)
            w_out = w_out_odd[li].astype(BF16)
            g_in = norm_odd[li][None, :]
            qg = q_norm[li][None, :]
            kg = k_norm[li]
            tn = tn_in if w_in.shape[1] % tn_in == 0 else w_in.shape[1]
            proj = _norm_matmul(xp, g_in, w_in, tn=tn)
            q_b, kv_cmp, kv_sel, kv_win, kt_sel, v_sel, kt_win, v_win, means = _odd_post(
                proj, cos_p, sin_p, qg, kg, cw=cw, kvw=kvw, for_prompt=True)
            r3 = lambda a: a.reshape(bsz, seq, a.shape[-1])
            act = _nsa_prompt(r3(q_b), kt_sel, r3(v_sel), kt_win, r3(v_win),
                              means.reshape(bsz, nb_p, 2 * kvw), r3(proj), tables,
                              cw=cw, z_off=z_off, g_off=g_off)
            xp = _matmul_residual(act.reshape(bsz * seq, cw), w_out, xp)
            kv6 = lambda a, n: a.reshape(n, -1, 2, groups, HEAD_DIM)
            cmp_p.append(kv6(kv_cmp, bsz))
            sel_p.append(kv6(kv_sel, bsz))
            win_keep = min(WINDOW, seq)
            win_p.append(kv6(kv_win, bsz)[:, seq - win_keep:])
            proj_s = _norm_matmul(xs, g_in, w_in, tn=tn)
            q_s, kvc_s, kvs_s, kvw_s = _odd_post(
                proj_s, cos_s, sin_s, qg, kg, cw=cw, kvw=kvw, for_prompt=False)
            q3s = q_s.reshape(db, heads, HEAD_DIM)
            means_s = _page_means(cache_cmp5, page_table, li)
            o_c, idx = _nsa_sample_select(q3s, means_s, groups=groups, rep=rep, q_pos=past_len,
                                          n_past_sel=n_past_sel)
            act_s, win_new = _nsa_sample_attend(
                page_table, idx[:, :, :n_past_sel], q3s, cache_sel5, li,
                kvs_s.reshape(db, 1, 2 * kvw), kvw_s.reshape(db, 1, 2 * kvw), o_c, win_state5,
                proj_s[:, g_off:g_off + groups * LANES].reshape(db, 1, groups * LANES),
                proj_s[:, z_off:z_off + cw].reshape(db, 1, cw),
                groups=groups, rep=rep, n_gather=n_past_sel)
            xs = _matmul_residual(act_s.reshape(db, cw), w_out, xs)
            cmp_s.append(kv6(kvc_s, db))
            sel_s.append(kv6(kvs_s, db))
            win_s.append(win_new.reshape(db, n_buf, 2, groups, HEAD_DIM))
    return (xp.reshape(bsz, seq, d_model), xs.reshape(db, 1, d_model),
            jnp.stack(cmp_p), jnp.stack(cmp_s), jnp.stack(sel_p), jnp.stack(sel_s),
            jnp.stack(win_p), jnp.stack(win_s), jnp.stack(pool_p), jnp.stack(pool_s), jnp.stack(gv_s))
```

```python
import functools

import jax
import jax.numpy as jnp
import numpy as np
from jax import lax
from jax.experimental import pallas as pl
from jax.experimental.pallas import tpu as pltpu

F32 = jnp.float32
BF16 = jnp.bfloat16
I32 = jnp.int32

EPS = 1e-6
PAGE_SIZE = 128
A_HEADS = 8
GMLP_CHUNK = 128
POOL_WINDOWS = (2, 4, 8, 16)
POOL_PAD = max(POOL_WINDOWS) - 1
HEAD_DIM = 128
C_KV_HEADS = 4
CMP_BLOCK = 64
N_SEL = 16
WINDOW = 512
SEL_FORCE = 1e4
NEG_INF = -1e30
ROPE_THETA = 500000.0
ROT_DIM = HEAD_DIM // 4
LOG2E = 1.4426950408889634
Q_SCALE = HEAD_DIM ** -0.5 * LOG2E

LANES = 128
SUBLANES = 8
MIB = 1024 * 1024


def _cparams(vmem_mib, semantics=None):
    return pltpu.CompilerParams(vmem_limit_bytes=int(vmem_mib * MIB), dimension_semantics=semantics)


def _silu(x):
    return x * (1.0 / (1.0 + jnp.exp(-x)))


def _sigmoid(x):
    return 1.0 / (1.0 + jnp.exp(-x))


def _dot_nt(a, b):
    return lax.dot_general(a, b, (((1,), (1,)), ((), ())), preferred_element_type=F32)


def _dot(a, b):
    return jnp.dot(a, b, preferred_element_type=F32)


def _norm_mm_kernel(x_ref, g_ref, w_ref, o_ref):
    x = x_ref[...]
    ms = jnp.mean(x * x, axis=-1, keepdims=True)
    h = (x * lax.rsqrt(ms + EPS) * g_ref[...]).astype(BF16)
    o_ref[...] = jnp.dot(h, w_ref[...], preferred_element_type=F32)


def _mm_res_kernel(a_ref, w_ref, r_ref, o_ref):
    o_ref[...] = r_ref[...] + jnp.dot(a_ref[...], w_ref[...], preferred_element_type=F32)


def _row_tile(m, want):
    return want if m % want == 0 else m


def _norm_matmul(x, g, w, *, tn, tm=512):
    m, k = x.shape
    n = w.shape[1]
    tm = _row_tile(m, tm)
    assert n % tn == 0
    vmem = 2 * (tm * k * 4 + k * tn * 2 + tm * tn * 4) / MIB + 8
    return pl.pallas_call(
        _norm_mm_kernel,
        grid=(n // tn, m // tm),
        in_specs=[
            pl.BlockSpec((tm, k), lambda j, i: (i, 0)),
            pl.BlockSpec((1, k), lambda j, i: (0, 0)),
            pl.BlockSpec((k, tn), lambda j, i: (0, j)),
        ],
        out_specs=pl.BlockSpec((tm, tn), lambda j, i: (i, j)),
        out_shape=jax.ShapeDtypeStruct((m, n), F32),
        compiler_params=_cparams(vmem),
        name="norm_matmul",
    )(x, g, w)


def _matmul_residual(a, w, res, *, tm=512):
    m, k = a.shape
    n = w.shape[1]
    tm = _row_tile(m, tm)
    vmem = 2 * (tm * k * 2 + k * n * 2 + 2 * tm * n * 4) / MIB + 8
    return pl.pallas_call(
        _mm_res_kernel,
        grid=(m // tm,),
        in_specs=[
            pl.BlockSpec((tm, k), lambda i: (i, 0)),
            pl.BlockSpec((k, n), lambda i: (0, 0)),
            pl.BlockSpec((tm, n), lambda i: (i, 0)),
        ],
        out_specs=pl.BlockSpec((tm, n), lambda i: (i, 0)),
        out_shape=jax.ShapeDtypeStruct((m, n), F32),
        compiler_params=_cparams(vmem),
        name="matmul_residual",
    )(a, w, res)


def _even_mix_kernel(proj_ref, vg_ref, ws_ref, bst_ref, wp_ref, ps_ref, act_ref, pool_ref, ext_ref,
                     *, tm, aw, bw):
    c = pl.program_id(1)
    n_c = pl.num_programs(1)
    hd = aw // A_HEADS
    pg = bw // len(POOL_WINDOWS)

    cl = GMLP_CHUNK
    row = lax.broadcasted_iota(I32, (cl, cl), 0)
    col = lax.broadcasted_iota(I32, (cl, cl), 1)
    causal = row >= col

    for h in range(A_HEADS):
        sl = slice(h * hd, (h + 1) * hd)
        w = jnp.where(causal, ws_ref[h], 0.0).astype(BF16)
        for c0 in range(0, tm, cl):
            rs = slice(c0, c0 + cl)
            u = proj_ref[0, rs, sl]
            v = proj_ref[0, rs, aw + h * hd:aw + (h + 1) * hd]
            za = proj_ref[0, rs, 2 * aw + h * hd:2 * aw + (h + 1) * hd]
            ms = jnp.mean(v * v, axis=-1, keepdims=True)
            vn = v * lax.rsqrt(ms + EPS) * vg_ref[:, sl]
            s = jnp.dot(w, vn.astype(BF16), preferred_element_type=F32) + bst_ref[:, h:h + 1]
            act_ref[0, rs, sl] = ((u * s) * _silu(za)).astype(BF16)

    p_off = 3 * aw
    zb_off = 3 * aw + bw

    @pl.when(c == 0)
    def _():
        ext_ref[0:16, :] = jnp.zeros((16, bw), F32)

    ext_ref[16:16 + tm, :] = proj_ref[0, :, p_off:p_off + bw]
    pos = c * tm + lax.broadcasted_iota(I32, (tm, 1), 0)
    for g, wnd in enumerate(POOL_WINDOWS):
        sl = slice(g * pg, (g + 1) * pg)
        cur = ext_ref[16:16 + tm, sl]
        acc = cur
        for k in range(1, wnd):
            acc = acc + ext_ref[16 - k:16 - k + tm, sl]
        cnt = jnp.minimum(pos + 1, wnd).astype(F32)
        d = acc / cnt - cur
        y = jnp.dot(d.astype(BF16), wp_ref[g], preferred_element_type=F32)
        zb = proj_ref[0, :, zb_off + g * pg:zb_off + (g + 1) * pg]
        act_ref[0, :, aw + g * pg:aw + (g + 1) * pg] = ((y * ps_ref[:, sl]) * _silu(zb)).astype(BF16)

    tail = ext_ref[tm:tm + 16, :]
    ext_ref[0:16, :] = tail

    @pl.when(c == n_c - 1)
    def _():
        pool_ref[0] = tail


def _even_mix(proj3, v_gain, w_s, b_st, w_pool, pool_scale):
    b, s, _ = proj3.shape
    aw = v_gain.shape[1]
    bw = pool_scale.shape[1]
    tm = 2 * GMLP_CHUNK if s % (2 * GMLP_CHUNK) == 0 else GMLP_CHUNK
    assert s % tm == 0 and s >= 16
    n_in = 3 * aw + 2 * bw
    kern = functools.partial(_even_mix_kernel, tm=tm, aw=aw, bw=bw)
    return pl.pallas_call(
        kern,
        grid=(b, s // tm),
        in_specs=[
            pl.BlockSpec((1, tm, n_in), lambda i, c: (i, c, 0)),
            pl.BlockSpec((1, aw), lambda i, c: (0, 0)),
            pl.BlockSpec(w_s.shape, lambda i, c: (0, 0, 0)),
            pl.BlockSpec(b_st.shape, lambda i, c: (0, 0)),
            pl.BlockSpec(w_pool.shape, lambda i, c: (0, 0, 0)),
            pl.BlockSpec((1, bw), lambda i, c: (0, 0)),
        ],
        out_specs=[
            pl.BlockSpec((1, tm, aw + bw), lambda i, c: (i, c, 0)),
            pl.BlockSpec((1, 16, bw), lambda i, c: (i, 0, 0)),
        ],
        out_shape=[
            jax.ShapeDtypeStruct((b, s, aw + bw), BF16),
            jax.ShapeDtypeStruct((b, 16, bw), F32),
        ],
        scratch_shapes=[pltpu.VMEM((16 + tm, bw), F32)],
        compiler_params=_cparams(32, ("arbitrary", "arbitrary")),
        name="even_mix",
    )(proj3, v_gain, w_s, b_st, w_pool, pool_scale)


def _even_mix_sample_kernel(proj_ref, vg_ref, w00_ref, b0_ref, wp_ref, ps_ref, st_ref,
                            act_ref, vn_ref, newst_ref, *, aw, bw, pos):
    hd = aw // A_HEADS
    pg = bw // len(POOL_WINDOWS)
    for h in range(A_HEADS):
        sl = slice(h * hd, (h + 1) * hd)
        u = proj_ref[:, sl]
        v = proj_ref[:, aw + h * hd:aw + (h + 1) * hd]
        za = proj_ref[:, 2 * aw + h * hd:2 * aw + (h + 1) * hd]
        ms = jnp.mean(v * v, axis=-1, keepdims=True)
        vn = v * lax.rsqrt(ms + EPS) * vg_ref[:, sl]
        vn_ref[:, sl] = vn
        s = w00_ref[:, sl] * vn + b0_ref[:, sl]
        act_ref[:, sl] = ((u * s) * _silu(za)).astype(BF16)

    p = proj_ref[:, 3 * aw:3 * aw + bw]
    for g, wnd in enumerate(POOL_WINDOWS):
        sl = slice(g * pg, (g + 1) * pg)
        cur = p[:, sl]
        acc = cur
        for k in range(1, wnd):
            acc = acc + st_ref[POOL_PAD - k, :, sl]
        cnt = float(min(pos + 1, wnd))
        d = acc / cnt - cur
        y = jnp.dot(d.astype(BF16), wp_ref[g], preferred_element_type=F32)
        zb = proj_ref[:, 3 * aw + bw + g * pg:3 * aw + bw + (g + 1) * pg]
        act_ref[:, aw + g * pg:aw + (g + 1) * pg] = ((y * ps_ref[:, sl]) * _silu(zb)).astype(BF16)

    for k in range(POOL_PAD - 1):
        newst_ref[k] = st_ref[k + 1]
    newst_ref[POOL_PAD - 1] = p


def _even_mix_sample(proj, v_gain, w00, b0, w_pool, pool_scale, state_t, pos):
    db = proj.shape[0]
    aw = v_gain.shape[1]
    bw = pool_scale.shape[1]
    kern = functools.partial(_even_mix_sample_kernel, aw=aw, bw=bw, pos=pos)
    return pl.pallas_call(
        kern,
        out_shape=[
            jax.ShapeDtypeStruct((db, aw + bw), BF16),
            jax.ShapeDtypeStruct((db, aw), F32),
            jax.ShapeDtypeStruct((POOL_PAD, db, bw), F32),
        ],
        name="even_mix_sample",
    )(proj, v_gain, w00, b0, w_pool, pool_scale, state_t)


def _odd_post_kernel(proj_ref, cos_ref, sin_ref, qg_ref, kg_ref, *out_refs, tm, cw, kvw, for_prompt):
    q_out, cmp_out, sel_out, win_out = out_refs[:4]
    if for_prompt:
        selkt_out, selv_out, winkt_out, winv_out, means_out = out_refs[4:]
        attn_outs = (None, (selkt_out, selv_out), (winkt_out, winv_out))
    else:
        attn_outs = (None, None, None)
    cosf = cos_ref[...]
    sinf = sin_ref[...]
    half = ROT_DIM // 2
    first = lax.broadcasted_iota(I32, (tm, HEAD_DIM), 1) < half

    def norm_rope(x, gain):
        ms = jnp.mean(x * x, axis=-1, keepdims=True)
        y = x * lax.rsqrt(ms + EPS) * gain
        rot = jnp.where(first, pltpu.roll(y, HEAD_DIM - half, 1), pltpu.roll(y, half, 1))
        return y * cosf + rot * sinf

    qg = qg_ref[...]
    for h in range(cw // HEAD_DIM):
        sl = slice(h * HEAD_DIM, (h + 1) * HEAD_DIM)
        q_out[:, sl] = (norm_rope(proj_ref[:, sl], qg) * Q_SCALE).astype(BF16)

    for br, (o32, attn) in enumerate(zip((cmp_out, sel_out, win_out), attn_outs)):
        k_off = cw + br * 2 * kvw
        v_off = k_off + kvw
        kg = kg_ref[br:br + 1, :]
        pieces = []
        for g in range(kvw // HEAD_DIM):
            sl = slice(g * HEAD_DIM, (g + 1) * HEAD_DIM)
            kk = norm_rope(proj_ref[:, k_off + g * HEAD_DIM:k_off + (g + 1) * HEAD_DIM], kg)
            pieces.append(kk)
            if attn is not None:
                attn[0][sl, :] = kk.T.astype(BF16)
        vv = proj_ref[:, v_off:v_off + kvw]
        if attn is not None:
            attn[1][...] = vv.astype(BF16)
        row = jnp.concatenate(pieces + [vv], axis=1)
        o32[...] = row.reshape(o32.shape)
        if br == 0 and for_prompt:
            for j in range(tm // CMP_BLOCK):
                blk = row[j * CMP_BLOCK:(j + 1) * CMP_BLOCK, :]
                means_out[0, j:j + 1, :] = jnp.mean(blk, axis=0, keepdims=True)


def _odd_post(proj, cos_t, sin_t, q_gain, k_gain, *, cw, kvw, for_prompt, win_keep=0, tm=256):
    m = proj.shape[0]
    tm = _row_tile(m, tm)
    n_pos_tiles = cos_t.shape[0] // tm
    n_used = cw + 6 * kvw
    kern = functools.partial(_odd_post_kernel, tm=tm, cw=cw, kvw=kvw, for_prompt=for_prompt)
    row_spec = lambda w: pl.BlockSpec((tm, w), lambda i: (i, 0))
    n_rows = 2 * kvw // HEAD_DIM
    if for_prompt:
        assert win_keep % tm == 0 and cos_t.shape[0] >= win_keep
        keep_tiles = win_keep // tm
        skip_tiles = n_pos_tiles - keep_tiles
        kv_shape = jax.ShapeDtypeStruct((m, n_rows, HEAD_DIM), F32)
        kv_spec = pl.BlockSpec((tm, n_rows, HEAD_DIM), lambda i: (i, 0, 0))
        win_shape = jax.ShapeDtypeStruct((m // n_pos_tiles * keep_tiles, n_rows, HEAD_DIM), F32)
        win_spec = pl.BlockSpec(
            (tm, n_rows, HEAD_DIM),
            lambda i: (i // n_pos_tiles * keep_tiles + jnp.maximum(i % n_pos_tiles - skip_tiles, 0), 0, 0))
    else:
        kv_shape = win_shape = jax.ShapeDtypeStruct((m, 2 * kvw), F32)
        kv_spec = win_spec = row_spec(2 * kvw)
    out_shape = [jax.ShapeDtypeStruct((m, cw), BF16), kv_shape, kv_shape, win_shape]
    out_specs = [row_spec(cw), kv_spec, kv_spec, win_spec]
    if for_prompt:
        assert tm % CMP_BLOCK == 0 and tm % LANES == 0
        for _ in range(2):
            out_shape += [jax.ShapeDtypeStruct((kvw, m), BF16), jax.ShapeDtypeStruct((m, kvw), BF16)]
            out_specs += [pl.BlockSpec((kvw, tm), lambda i: (0, i)), row_spec(kvw)]
        out_shape.append(jax.ShapeDtypeStruct((m // tm, tm // CMP_BLOCK, 2 * kvw), F32))
        out_specs.append(pl.BlockSpec((1, tm // CMP_BLOCK, 2 * kvw), lambda i: (i, 0, 0)))
    return pl.pallas_call(
        kern,
        grid=(m // tm,),
        in_specs=[
            pl.BlockSpec((tm, n_used), lambda i: (i, 0)),
            pl.BlockSpec((tm, HEAD_DIM), lambda i: (i % n_pos_tiles, 0)),
            pl.BlockSpec((tm, HEAD_DIM), lambda i: (i % n_pos_tiles, 0)),
            pl.BlockSpec((1, HEAD_DIM), lambda i: (0, 0)),
            pl.BlockSpec((3, HEAD_DIM), lambda i: (0, 0)),
        ],
        out_specs=out_specs,
        out_shape=out_shape,
        compiler_params=_cparams(40),
        name="odd_post",
    )(proj, cos_t, sin_t, q_gain, k_gain)


def _nsa_sub_tile(i, h, q_ref, ks_ref, vs_ref, kw_ref, vw_ref, kc, vc, band_ref, tri_ref, eye_ref,
                  *, tq, nb, rep, lw):
    rows = rep * tq
    r0 = h * tq
    q = jnp.concatenate([q_ref[0, r0:r0 + tq, r * HEAD_DIM:(r + 1) * HEAD_DIM] for r in range(rep)],
                        axis=0)
    t_col = i * tq + lax.broadcasted_iota(I32, (tq, 1), 0)
    t_row = i * tq + lax.broadcasted_iota(I32, (1, tq), 1)

    blk_row = lax.broadcasted_iota(I32, (1, nb), 1)
    s_c = _dot_nt(q, kc).reshape(rep, tq, nb)
    ok_c = ((blk_row + 1) * CMP_BLOCK - 1) <= t_col
    s_c = s_c + jnp.where(ok_c, 0.0, NEG_INF)[None]
    m_c = jnp.max(s_c, axis=-1, keepdims=True)
    e_c = jnp.exp2(s_c - m_c)
    p_c = e_c / jnp.sum(e_c, axis=-1, keepdims=True) * jnp.where(ok_c, 1.0, 0.0)[None]
    o_c = jnp.dot(p_c.reshape(rows, nb).astype(BF16), vc, preferred_element_type=F32)

    blk_col = lax.broadcasted_iota(I32, (nb, 1), 0)
    s_t = _dot_nt(kc, q)
    ok_t = ((blk_col + 1) * CMP_BLOCK - 1) <= t_row
    bias_t = jnp.where(ok_t, 0.0, NEG_INF)
    okf_t = jnp.where(ok_t, 1.0, 0.0)
    imp = jnp.zeros((nb, tq), F32)
    for r in range(rep):
        s_r = s_t[:, r * tq:(r + 1) * tq] + bias_t
        e_r = jnp.exp2(s_r - jnp.max(s_r, axis=0, keepdims=True))
        imp = imp + e_r / jnp.sum(e_r, axis=0, keepdims=True) * okf_t
    cur = t_row // CMP_BLOCK
    forced = (blk_col == 0) | (blk_col == cur) | (blk_col == cur - 1)
    imp = jnp.where(blk_col <= cur, jnp.where(forced, SEL_FORCE, imp), -1.0)

    blk8 = lax.broadcasted_iota(I32, (SUBLANES, 1), 0)
    cnt = jnp.zeros((nb, tq), F32)
    for j in range(nb):
        rj = imp[j:j + 1, :]
        parts = []
        for v0 in range(0, nb, SUBLANES):
            sub = imp[v0:v0 + SUBLANES, :]
            if v0 > j:
                beats = rj >= sub
            elif v0 + SUBLANES - 1 <= j:
                beats = rj > sub
            else:
                beats = (rj > sub) | ((rj == sub) & (blk8 > j - v0))
            parts.append(jnp.where(beats, 1.0, 0.0))
        cnt = cnt + jnp.concatenate(parts, axis=0)
    keep = (cnt < min(N_SEL, nb)) & (imp >= 0.0) & (blk_col < i * (tq // CMP_BLOCK))
    drop_t = jnp.where(keep, 0.0, 1.0)
    if LANES > nb:
        drop_t = jnp.concatenate([drop_t, jnp.zeros((LANES - nb, tq), F32)], axis=0)
    drop = drop_t.T.astype(BF16)
    q_drop = jnp.concatenate([q, jnp.concatenate([drop] * rep, axis=0)], axis=1)

    q_eye = jnp.concatenate([q, eye_ref[...]], axis=1)

    w0 = pl.multiple_of(jnp.maximum(i * tq + tq - lw, 0), tq)
    band = band_ref[jnp.minimum(i, band_ref.shape[0] - 1)]
    k_w = jnp.concatenate([kw_ref[:, pl.ds(w0, lw)], band], axis=0)
    v_w = jnp.concatenate([vw_ref[0, pl.ds(w0, lw), :], jnp.ones((lw, HEAD_DIM), BF16)], axis=1)
    s_w = _dot(q_eye, k_w)
    p_w = jnp.exp2(s_w - jnp.max(s_w, axis=-1, keepdims=True))
    acc_w = _dot(p_w.astype(BF16), v_w)
    o_w = acc_w[:, :HEAD_DIM] / acc_w[:, HEAD_DIM:]

    d0 = pl.multiple_of(i * tq, tq)
    k_d = jnp.concatenate([ks_ref[:, pl.ds(d0, tq)], tri_ref[...]], axis=0)
    v_d = jnp.concatenate([vs_ref[0, pl.ds(d0, tq), :], jnp.ones((tq, HEAD_DIM), BF16)], axis=1)
    s_d = _dot(q_eye, k_d)
    m_d = jnp.max(s_d, axis=-1, keepdims=True)
    acc_d = _dot(jnp.exp2(s_d - m_d).astype(BF16), v_d)
    return q_drop, o_c, o_w, m_d, acc_d


def _nsa_prompt_kernel(pt_ref, q_ref, ks_ref, vs_ref, kw_ref, vw_ref, kc_ref, vc_ref, g_ref, z_ref,
                       eb_ref, band_ref, tri_ref, eye_ref, *rest, tq, nh, tk, seq, nb, rep, lw, pps):
    if pps:
        o_ref, pm_ref, s0_ref, s1_ref = rest[pps:]
        _store_page_means(rest[:pps], pm_ref)
    else:
        o_ref, s0_ref, s1_ref = rest
    step = pl.program_id(2)
    kc = kc_ref[0].astype(BF16)
    vc = vc_ref[0].astype(BF16)
    subs = [_nsa_sub_tile(step * nh + h, h, q_ref, ks_ref, vs_ref, kw_ref, vw_ref, kc, vc, band_ref,
                          tri_ref, eye_ref, tq=tq, nb=nb, rep=rep, lw=lw) for h in range(nh)]

    q_drop = jnp.concatenate([s[0] for s in subs], axis=0)
    m_0 = jnp.concatenate([s[3] for s in subs], axis=0)
    acc_0 = jnp.concatenate([s[4] for s in subs], axis=0)

    partial, gate_s, z_act = [], [], []
    for h in range(nh):
        ts = slice(h * tq, (h + 1) * tq)
        gs = _sigmoid(g_ref[0, ts, :])
        o_c, o_w = subs[h][1], subs[h][2]
        for r in range(rep):
            rs = slice(r * tq, (r + 1) * tq)
            partial.append(gs[:, 3 * r:3 * r + 1] * o_c[rs] + gs[:, 3 * r + 2:3 * r + 3] * o_w[rs])
            gate_s.append(gs[:, 3 * r + 1:3 * r + 2])
            z_act.append(_silu(z_ref[0, ts, r * HEAD_DIM:(r + 1) * HEAD_DIM]))

    ones_k = jnp.ones((tk, HEAD_DIM), BF16)
    n_tiles = seq // tk

    def tile_scores(t):
        k0 = pl.multiple_of(t * tk, tk)
        k = jnp.concatenate([ks_ref[:, pl.ds(k0, tk)], eb_ref[:, pl.ds(k0, tk)]], axis=0)
        return _dot(q_drop, k)

    def accumulate(carry, s, t):
        m_p, acc = carry
        k0 = pl.multiple_of(t * tk, tk)
        v = jnp.concatenate([vs_ref[0, pl.ds(k0, tk), :], ones_k], axis=1)
        m_n = jnp.maximum(m_p, jnp.max(s, axis=-1, keepdims=True))
        pv = _dot(jnp.exp2(s - m_n).astype(BF16), v)
        return m_n, jnp.exp2(m_p - m_n) * acc + pv

    def sel_body(j, carry):
        t1 = 2 * j + 1
        s1_ref[...] = tile_scores(t1)
        carry = accumulate(carry, s0_ref[...], 2 * j)
        s0_ref[...] = tile_scores(jnp.minimum(t1 + 1, n_tiles - 1))
        return accumulate(carry, s1_ref[...], t1)

    n_past = ((step * nh + nh - 1) * tq + tk - 1) // tk
    s0_ref[...] = tile_scores(0)
    _, acc_s = lax.fori_loop(0, (n_past + 1) // 2, sel_body, (m_0, acc_0))
    o_s = acc_s[:, :HEAD_DIM] / acc_s[:, HEAD_DIM:]

    for h in range(nh):
        for r in range(rep):
            n = h * rep + r
            o = partial[n] + gate_s[n] * o_s[n * tq:(n + 1) * tq]
            o_ref[0, h * tq:(h + 1) * tq, r * HEAD_DIM:(r + 1) * HEAD_DIM] = (o * z_act[n]).astype(BF16)


def _mask_tables(seq, tq, lw, rep):
    neg = lambda ok: jnp.where(ok, 0.0, NEG_INF).astype(BF16)
    row = jnp.arange(LANES, dtype=I32)[:, None]
    key = jnp.arange(seq, dtype=I32)[None, :]
    block_rows = neg(key // CMP_BLOCK != row)
    t = jnp.arange(tq, dtype=I32)[None, :, None]
    c = jnp.arange(lw, dtype=I32)[None, None, :]
    off = jnp.arange(WINDOW // tq + 1, dtype=I32)[:, None, None] * tq
    rel = off + t - c
    band = neg((rel >= 0) & (rel < WINDOW))
    tri = neg(jnp.arange(tq, dtype=I32)[None, :] <= jnp.arange(tq, dtype=I32)[:, None])
    eye = jnp.tile(jnp.eye(tq, dtype=BF16), (rep, 1))
    return block_rows, band, tri, eye


def _page_job_split(n_steps, page_table):
    db, n_pages = page_table.shape
    total = db * n_pages
    if total % n_steps:
        return 0
    pps = total // n_steps
    return pps if n_pages % pps == 0 else 0


def _nsa_prompt(q3, kt_sel, v_sel, kt_win, v_win, means3, proj3, tables, cache5, page_table, layer, *,
                cw, z_off, g_off, tq=LANES, nh=2, tk=512):
    b, s, _ = q3.shape
    kvw = v_sel.shape[2]
    groups = kvw // HEAD_DIM
    rep = cw // kvw
    nb = means3.shape[1]
    gw = rep * HEAD_DIM
    block_rows, band, tri, eye = tables
    tk = min(tk, s)
    lw = band.shape[2]
    ts = nh * tq
    assert tq == LANES and nb <= LANES and nb % SUBLANES == 0 and lw == WINDOW + tq and s >= lw
    assert s % ts == 0 and s % (2 * tk) == 0 and tk % tq == 0 and tq % CMP_BLOCK == 0
    assert nb * CMP_BLOCK == s
    assert z_off % gw == 0 and g_off % LANES == 0
    n_i = s // ts
    pps = _page_job_split(b * groups * n_i, page_table)
    kern = functools.partial(_nsa_prompt_kernel, tq=tq, nh=nh, tk=tk, seq=s, nb=nb, rep=rep, lw=lw,
                             pps=pps)
    db, n_pages = page_table.shape
    n_rows, bpp = cache5.shape[3], PAGE_SIZE // CMP_BLOCK
    spp = n_pages // pps if pps else 0
    assert b * groups * n_i * pps in (0, db * n_pages)
    flat = lambda bi, g, i: (bi * groups + g) * n_i + i

    def page_spec(r):
        def index_map(bi, g, i, pt):
            f = flat(bi, g, i)
            return (layer, pt[f // spp, (f % spp) * pps + r], 0, 0, 0)
        return pl.BlockSpec((None, None, PAGE_SIZE, n_rows, HEAD_DIM), index_map)

    kt_spec = pl.BlockSpec((HEAD_DIM, s), lambda bi, g, i, pt: (g, bi))
    v_spec = pl.BlockSpec((1, s, HEAD_DIM), lambda bi, g, i, pt: (bi, 0, g))
    mean_spec = lambda off: pl.BlockSpec((1, nb, HEAD_DIM), lambda bi, g, i, pt: (bi, 0, off + g))
    grid_spec = pltpu.PrefetchScalarGridSpec(
        num_scalar_prefetch=1,
        grid=(b, groups, n_i),
        in_specs=[
            pl.BlockSpec((1, ts, gw), lambda bi, g, i, pt: (bi, i, g)),
            kt_spec, v_spec, kt_spec, v_spec,
            mean_spec(0), mean_spec(groups),
            pl.BlockSpec((1, ts, LANES), lambda bi, g, i, pt: (bi, i, g_off // LANES + g)),
            pl.BlockSpec((1, ts, gw), lambda bi, g, i, pt: (bi, i, z_off // gw + g)),
            pl.BlockSpec(block_rows.shape, lambda bi, g, i, pt: (0, 0)),
            pl.BlockSpec(band.shape, lambda bi, g, i, pt: (0, 0, 0)),
            pl.BlockSpec(tri.shape, lambda bi, g, i, pt: (0, 0)),
            pl.BlockSpec(eye.shape, lambda bi, g, i, pt: (0, 0)),
        ] + [page_spec(r) for r in range(pps)],
        out_specs=[pl.BlockSpec((1, ts, gw), lambda bi, g, i, pt: (bi, i, g))] + ([
            pl.BlockSpec((1, n_rows, pps * bpp, HEAD_DIM),
                         lambda bi, g, i, pt: (flat(bi, g, i) // spp, 0, flat(bi, g, i) % spp, 0)),
        ] if pps else []),
        scratch_shapes=[pltpu.VMEM((nh * rep * tq, tk), F32), pltpu.VMEM((nh * rep * tq, tk), F32)],
    )
    outs = pl.pallas_call(
        kern,
        grid_spec=grid_spec,
        out_shape=[jax.ShapeDtypeStruct((b, s, cw), BF16)] + ([
            jax.ShapeDtypeStruct((db, n_rows, n_pages * bpp, HEAD_DIM), F32)] if pps else []),
        compiler_params=_cparams(56),
        name="nsa_prompt",
    )(page_table, q3, kt_sel, v_sel, kt_win, v_win, means3, means3, proj3, proj3, block_rows, band, tri,
      eye, *([cache5] * pps))
    return outs if pps else (outs[0], _page_means(cache5, page_table, layer))


def _store_page_means(page_refs, out_ref):
    bpp = PAGE_SIZE // CMP_BLOCK
    n_rows = out_ref.shape[1]
    for r, page_ref in enumerate(page_refs):
        for h in range(bpp):
            mean = jnp.mean(page_ref[h * CMP_BLOCK:(h + 1) * CMP_BLOCK], axis=0)
            n = r * bpp + h
            for j in range(n_rows):
                out_ref[0, j, n:n + 1, :] = mean[j:j + 1, :]


def _page_means_kernel(pt_ref, *refs, pps):
    _store_page_means(refs[:pps], refs[pps])


def _page_means(cache5, page_table, layer, *, pps=8):
    db, n_pages = page_table.shape
    n_rows, d = cache5.shape[3], cache5.shape[4]
    pps = pps if n_pages % pps == 0 else 1
    bpp = PAGE_SIZE // CMP_BLOCK

    def page_spec(r):
        return pl.BlockSpec((None, None, PAGE_SIZE, n_rows, d),
                            lambda bi, j, pt: (layer, pt[bi, j * pps + r], 0, 0, 0))

    grid_spec = pltpu.PrefetchScalarGridSpec(
        num_scalar_prefetch=1,
        grid=(db, n_pages // pps),
        in_specs=[page_spec(r) for r in range(pps)],
        out_specs=pl.BlockSpec((1, n_rows, pps * bpp, d), lambda bi, j, pt: (bi, 0, j, 0)),
    )
    return pl.pallas_call(
        functools.partial(_page_means_kernel, pps=pps),
        grid_spec=grid_spec,
        out_shape=jax.ShapeDtypeStruct((db, n_rows, n_pages * bpp, d), F32),
        compiler_params=_cparams(24),
        name="page_means",
    )(page_table, *([cache5] * pps))


def _nsa_sample_select_kernel(q_ref, mean_ref, oc_ref, idx_ref, *, nbp, groups, rep, q_pos, n_past_sel):
    q = q_ref[0]
    heads = groups * rep
    lane = lax.broadcasted_iota(I32, (1, nbp), 1)
    ri = lax.broadcasted_iota(I32, (nbp, nbp), 0)
    ci = lax.broadcasted_iota(I32, (nbp, nbp), 1)
    diag = ri == ci
    ok = ((lane + 1) * CMP_BLOCK - 1) <= q_pos
    cur = q_pos // CMP_BLOCK
    forced = (lane == 0) | (lane == cur) | (lane == cur - 1)
    hrow = lax.broadcasted_iota(I32, (heads, 1), 0) // rep
    out_lane = lax.broadcasted_iota(I32, (1, LANES), 1)
    o_c = jnp.zeros((heads, HEAD_DIM), F32)
    for g in range(groups):
        kc = mean_ref[0, g].astype(BF16)
        vc = mean_ref[0, groups + g].astype(BF16)
        s = _dot_nt(q, kc) + jnp.where(ok, 0.0, NEG_INF)
        e = jnp.exp2(s - jnp.max(s, axis=-1, keepdims=True))
        p = e / jnp.sum(e, axis=-1, keepdims=True) * jnp.where(ok, 1.0, 0.0)
        in_g = hrow == g
        o_c = o_c + jnp.where(in_g, jnp.dot(p.astype(BF16), vc, preferred_element_type=F32), 0.0)
        imp = jnp.sum(jnp.where(in_g, p, 0.0), axis=0, keepdims=True)
        imp = jnp.where(lane <= cur, jnp.where(forced, SEL_FORCE, imp), -1.0)
        imp_b = jnp.broadcast_to(imp, (nbp, nbp))
        imp_col = jnp.sum(jnp.where(diag, imp_b, 0.0), axis=1, keepdims=True)
        beats = (imp_col > imp_b) | ((imp_col == imp_b) & (ri < ci))
        cnt = jnp.sum(jnp.where(beats, 1.0, 0.0), axis=0, keepdims=True)
        sel = jnp.where((cnt < n_past_sel) & (imp >= 0.0), 1.0, 0.0)
        sel_col = jnp.sum(jnp.where(diag, jnp.broadcast_to(sel, (nbp, nbp)), 0.0), axis=1, keepdims=True)
        before = jnp.sum(jnp.where(ri < ci, sel_col, 0.0), axis=0, keepdims=True)
        row = jnp.zeros((1, LANES), F32)
        for k in range(n_past_sel):
            hit = (sel > 0.5) & (before == float(k))
            idx_k = jnp.sum(jnp.where(hit, lane.astype(F32), 0.0), axis=1, keepdims=True)
            row = jnp.where(out_lane == k, idx_k, row)
        idx_ref[0, g:g + 1, :] = row.astype(I32)
    oc_ref[0] = o_c


def _nsa_sample_select(q3, means, *, groups, rep, q_pos, n_past_sel):
    db, heads, _ = q3.shape
    nbp = means.shape[2]
    kern = functools.partial(_nsa_sample_select_kernel, nbp=nbp, groups=groups, rep=rep, q_pos=q_pos,
                             n_past_sel=n_past_sel)
    return pl.pallas_call(
        kern,
        grid=(db,),
        in_specs=[
            pl.BlockSpec((1, heads, HEAD_DIM), lambda bi: (bi, 0, 0)),
            pl.BlockSpec((1,) + means.shape[1:], lambda bi: (bi, 0, 0, 0)),
        ],
        out_specs=[
            pl.BlockSpec((1, heads, HEAD_DIM), lambda bi: (bi, 0, 0)),
            pl.BlockSpec((1, groups, LANES), lambda bi: (bi, 0, 0)),
        ],
        out_shape=[
            jax.ShapeDtypeStruct((db, heads, HEAD_DIM), F32),
            jax.ShapeDtypeStruct((db, groups, LANES), I32),
        ],
        compiler_params=_cparams(24),
        name="nsa_sample_select",
    )(q3, means)


def _nsa_sample_attend_kernel(pt_ref, si_ref, q_ref, *refs, groups, rep, n_steps, per_step, n_buf):
    kv_refs = refs[:groups * per_step]
    (newsel_ref, newwin_ref, oc_ref, win_ref, g_ref, z_ref,
     act_ref, winout_ref, m_ref, l_ref, acc_ref) = refs[groups * per_step:]
    kvw = groups * HEAD_DIM
    heads = groups * rep
    n_rows = 2 * groups
    k_id = pl.program_id(1)
    q = q_ref[0]
    qf = q.astype(F32)
    hgrp = lax.broadcasted_iota(I32, (heads, 1), 0) // rep

    def own_key_rows(n_tok):
        lane = lax.broadcasted_iota(I32, (1, n_tok * n_rows), 1)
        return lane % n_rows == hgrp, lane // n_rows

    def by_group(fn):
        out = None
        for g in range(groups):
            val = jnp.where(hgrp == g, fn(g), 0.0)
            out = val if out is None else out + val
        return out

    def new_token_scores(row_ref):
        return by_group(lambda g: jnp.sum(
            qf * row_ref[0, :, g * HEAD_DIM:(g + 1) * HEAD_DIM], axis=-1, keepdims=True))

    def new_token_values(row_ref):
        return by_group(lambda g: jnp.broadcast_to(
            row_ref[0, :, kvw + g * HEAD_DIM:kvw + (g + 1) * HEAD_DIM], (heads, HEAD_DIM)))

    @pl.when(k_id == 0)
    def _():
        m_ref[...] = jnp.full((heads, 1), NEG_INF, F32)
        l_ref[...] = jnp.zeros((heads, 1), F32)
        acc_ref[...] = jnp.zeros((heads, HEAD_DIM), F32)

    xs = [jnp.concatenate([kv_refs[g * per_step + u][...].reshape(CMP_BLOCK * n_rows, HEAD_DIM)
                           for u in range(per_step)], axis=0).astype(BF16) for g in range(groups)]
    own, _ = own_key_rows(CMP_BLOCK * per_step)
    s = by_group(lambda g: _dot_nt(q, xs[g])) + jnp.where(own, 0.0, NEG_INF)
    m_p = m_ref[...]
    m_n = jnp.maximum(m_p, jnp.max(s, axis=-1, keepdims=True))
    alpha = jnp.exp2(m_p - m_n)
    p = jnp.exp2(s - m_n)
    pb = pltpu.roll(p, groups, 1).astype(BF16)
    pv = by_group(lambda g: jnp.dot(pb, xs[g], preferred_element_type=F32))
    m_ref[...] = m_n
    l_ref[...] = alpha * l_ref[...] + jnp.sum(p, axis=-1, keepdims=True)
    acc_ref[...] = alpha * acc_ref[...] + pv

    @pl.when(k_id == n_steps - 1)
    def _():
        s_n = new_token_scores(newsel_ref)
        m_p = m_ref[...]
        m_n = jnp.maximum(m_p, s_n)
        alpha = jnp.exp2(m_p - m_n)
        p_n = jnp.exp2(s_n - m_n)
        l_s = alpha * l_ref[...] + p_n
        o_s = (alpha * acc_ref[...] + p_n * new_token_values(newsel_ref)) / l_s

        xw = win_ref[0].reshape(n_buf * n_rows, HEAD_DIM).astype(BF16)
        own_w, tok_w = own_key_rows(n_buf)
        ok_w = own_w & ((n_buf - tok_w) < WINDOW)
        s_w = _dot_nt(q, xw) + jnp.where(ok_w, 0.0, NEG_INF)
        s_wn = new_token_scores(newwin_ref)
        m_w = jnp.maximum(jnp.max(s_w, axis=-1, keepdims=True), s_wn)
        p_w = jnp.exp2(s_w - m_w)
        p_wn = jnp.exp2(s_wn - m_w)
        l_w = jnp.sum(p_w, axis=-1, keepdims=True) + p_wn
        o_w = jnp.dot(pltpu.roll(p_w, groups, 1).astype(BF16), xw, preferred_element_type=F32)
        o_w = (o_w + p_wn * new_token_values(newwin_ref)) / l_w

        winout_ref[0, 0:n_buf - 1] = win_ref[0, 1:n_buf]
        for j in range(n_rows):
            winout_ref[0, n_buf - 1, j:j + 1, :] = newwin_ref[0, :, j * HEAD_DIM:(j + 1) * HEAD_DIM]

        graw = jnp.broadcast_to(g_ref[0], (heads, groups * LANES))
        glane = lax.broadcasted_iota(I32, (heads, groups * LANES), 1)
        hidx = lax.broadcasted_iota(I32, (heads, 1), 0)
        gbase = (hidx // rep) * LANES + 3 * (hidx % rep)
        gate = lambda c: _sigmoid(jnp.sum(jnp.where(glane == gbase + c, graw, 0.0), axis=-1, keepdims=True))
        o = gate(0) * oc_ref[0] + gate(1) * o_s + gate(2) * o_w
        for h in range(heads):
            z = z_ref[0, :, h * HEAD_DIM:(h + 1) * HEAD_DIM]
            act_ref[0, :, h * HEAD_DIM:(h + 1) * HEAD_DIM] = (o[h:h + 1, :] * _silu(z)).astype(BF16)


def _nsa_sample_attend(page_table, sel_idx, q3, cache5, layer, new_sel, new_win, o_c, win_state, gates3,
                       z3, *, groups, rep, n_gather):
    db, heads, _ = q3.shape
    kvw = groups * HEAD_DIM
    n_buf = win_state.shape[2]
    n_rows = 2 * groups
    bpp = PAGE_SIZE // CMP_BLOCK
    assert sel_idx.shape[2] == n_gather
    per_step = max(u for u in range(1, 6) if n_gather % u == 0)
    n_steps = n_gather // per_step

    def gather_spec(g, u):
        def index_map(bi, k, pt, si):
            blk = si[bi, g, k * per_step + u]
            return (layer, pt[bi, blk // bpp], blk % bpp, 0, 0)
        return pl.BlockSpec((None, None, CMP_BLOCK, n_rows, HEAD_DIM), index_map)

    in_specs = [pl.BlockSpec((1, heads, HEAD_DIM), lambda bi, k, pt, si: (bi, 0, 0))]
    in_specs += [gather_spec(g, u) for g in range(groups) for u in range(per_step)]
    row3 = lambda w: pl.BlockSpec((1, 1, w), lambda bi, k, pt, si: (bi, 0, 0))
    in_specs += [
        row3(2 * kvw), row3(2 * kvw),
        pl.BlockSpec((1, heads, HEAD_DIM), lambda bi, k, pt, si: (bi, 0, 0)),
        pl.BlockSpec((None, 1, n_buf, n_rows, HEAD_DIM), lambda bi, k, pt, si: (layer, bi, 0, 0, 0)),
        row3(groups * LANES), row3(heads * HEAD_DIM),
    ]
    grid_spec = pltpu.PrefetchScalarGridSpec(
        num_scalar_prefetch=2,
        grid=(db, n_steps),
        in_specs=in_specs,
        out_specs=[
            row3(heads * HEAD_DIM),
            pl.BlockSpec((1, n_buf, n_rows, HEAD_DIM), lambda bi, k, pt, si: (bi, 0, 0, 0)),
        ],
        scratch_shapes=[pltpu.VMEM((heads, 1), F32), pltpu.VMEM((heads, 1), F32),
                        pltpu.VMEM((heads, HEAD_DIM), F32)],
    )
    kern = functools.partial(_nsa_sample_attend_kernel, groups=groups, rep=rep, n_steps=n_steps,
                             per_step=per_step, n_buf=n_buf)
    return pl.pallas_call(
        kern,
        grid_spec=grid_spec,
        out_shape=[
            jax.ShapeDtypeStruct((db, 1, heads * HEAD_DIM), BF16),
            jax.ShapeDtypeStruct((db, n_buf, n_rows, HEAD_DIM), F32),
        ],
        compiler_params=_cparams(32, ("arbitrary", "arbitrary")),
        name="nsa_sample_attend",
    )(page_table, sel_idx, q3, *([cache5] * (groups * per_step)), new_sel, new_win, o_c, win_state,
      gates3, z3)


def _rope_tables(pos):
    half = ROT_DIM // 2
    inv_freq = ROPE_THETA ** (-jnp.arange(half, dtype=F32) * (2.0 / ROT_DIM))
    ang = pos.astype(F32)[:, None] * inv_freq[None, :]
    cos, sin = jnp.cos(ang), jnp.sin(ang)
    n = pos.shape[0]
    cos_t = jnp.concatenate([cos, cos, jnp.ones((n, HEAD_DIM - ROT_DIM), F32)], axis=1)
    sin_t = jnp.concatenate([-sin, sin, jnp.zeros((n, HEAD_DIM - ROT_DIM), F32)], axis=1)
    return cos_t, sin_t


def _odd_weight_layout(w, cw, kvw, rep):
    k = w.shape[0]
    groups = kvw // HEAD_DIM
    n_qkv = cw + 6 * kvw
    n_gate = 3 * groups * rep
    gates = w[:, n_qkv:n_qkv + n_gate].reshape(k, groups, 3 * rep)
    gates = jnp.pad(gates, ((0, 0), (0, 0), (0, LANES - 3 * rep))).reshape(k, groups * LANES)
    return jnp.concatenate([w[:, :n_qkv], w[:, n_qkv + n_gate:], gates], axis=1).astype(BF16)


def kernel(x_prompt, x_sample, cache_cmp_kv, cache_sel_kv, state_win_kv, state_pool, page_table,
           norm_even, w_in_even, v_norm, w_spatial, b_spatial, w_pool, pool_scale, w_out_even,
           norm_odd, w_in_odd, q_norm, k_norm, w_out_odd):
    bsz, seq, d_model = x_prompt.shape
    db, dec_t, _ = x_sample.shape
    assert dec_t == 1
    n_even, n_odd = norm_even.shape[0], norm_odd.shape[0]
    depth = n_even + n_odd
    n_pages = page_table.shape[1]
    past_len = n_pages * PAGE_SIZE
    aw = v_norm.shape[1]
    bw = pool_scale.shape[1]
    groups = C_KV_HEADS
    kvw = groups * HEAD_DIM
    cw = w_out_odd.shape[1]
    rep = cw // kvw
    heads = cw // HEAD_DIM
    n_phys = cache_cmp_kv.shape[1]
    n_buf = state_win_kv.shape[2]
    assert past_len % CMP_BLOCK == 0 and seq % CMP_BLOCK == 0
    nb_p = seq // CMP_BLOCK
    nb_past = past_len // CMP_BLOCK
    n_past_sel = min(N_SEL, nb_past + 1) - 1
    assert nb_past >= 2 and n_past_sel >= 2

    xp = x_prompt.reshape(bsz * seq, d_model)
    xs = x_sample.reshape(db, d_model)

    cos_p, sin_p = _rope_tables(jnp.arange(seq, dtype=I32))
    cos_s, sin_s = _rope_tables(jnp.full((db,), past_len, dtype=I32))
    tables = _mask_tables(seq, LANES, WINDOW + LANES, rep)

    cache_cmp5 = cache_cmp_kv.reshape(n_odd, n_phys, PAGE_SIZE, 2 * groups, HEAD_DIM)
    cache_sel5 = cache_sel_kv.reshape(n_odd, n_phys, PAGE_SIZE, 2 * groups, HEAD_DIM)
    win_state5 = state_win_kv.reshape(n_odd, db, n_buf, 2 * groups, HEAD_DIM)

    z_off = cw + 6 * kvw
    g_off = z_off + cw
    tn_in = 2560

    cmp_p, cmp_s, sel_p, sel_s, win_p, win_s = [], [], [], [], [], []
    pool_p, pool_s, gv_s = [], [], []
    for layer in range(depth):
        li = layer // 2
        if layer % 2 == 0:
            w_in = w_in_even[li].astype(BF16)
            w_out = w_out_even[li].astype(BF16)
            g_in = norm_even[li][None, :]
            vg = v_norm[li][None, :]
            ps = pool_scale[li][None, :]
            wp = w_pool[li].astype(BF16)
            tn = tn_in if w_in.shape[1] % tn_in == 0 else w_in.shape[1]
            proj = _norm_matmul(xp, g_in, w_in, tn=tn)
            act, pool16 = _even_mix(proj.reshape(bsz, seq, -1), vg, w_spatial[li], b_spatial[li].T, wp, ps)
            xp = _matmul_residual(act.reshape(bsz * seq, aw + bw), w_out, xp)
            pool_p.append(pool16[:, 16 - POOL_PAD:])
            proj_s = _norm_matmul(xs, g_in, w_in, tn=tn)
            hd = aw // A_HEADS
            w00 = jnp.repeat(w_spatial[li][:, 0, 0], hd)[None, :]
            b0 = jnp.repeat(b_spatial[li][:, 0], hd)[None, :]
            act_s, vn_s, new_state = _even_mix_sample(
                proj_s, vg, w00, b0, wp, ps, jnp.swapaxes(state_pool[li], 0, 1), past_len)
            xs = _matmul_residual(act_s, w_out, xs)
            pool_s.append(jnp.swapaxes(new_state, 0, 1))
            gv_s.append(vn_s.reshape(db, 1, aw))
        else:
            w_in = _odd_weight_layout(w_in_odd[li], cw, kvw, rep)
            w_out = w_out_odd[li].astype(BF16)
            g_in = norm_odd[li][None, :]
            qg = q_norm[li][None, :]
            kg = k_norm[li]
            tn = tn_in if w_in.shape[1] % tn_in == 0 else w_in.shape[1]
            proj = _norm_matmul(xp, g_in, w_in, tn=tn)
            q_b, kv_cmp, kv_sel, kv_win, kt_sel, v_sel, kt_win, v_win, means = _odd_post(
                proj, cos_p, sin_p, qg, kg, cw=cw, kvw=kvw, for_prompt=True, win_keep=min(WINDOW, seq))
            r3 = lambda a: a.reshape(bsz, seq, a.shape[-1])
            act, means_s = _nsa_prompt(r3(q_b), kt_sel, r3(v_sel), kt_win, r3(v_win),
                                       means.reshape(bsz, nb_p, 2 * kvw), r3(proj), tables,
                                       cache_cmp5, page_table, li, cw=cw, z_off=z_off, g_off=g_off)
            xp = _matmul_residual(act.reshape(bsz * seq, cw), w_out, xp)
            kv6 = lambda a, n: a.reshape(n, -1, 2, groups, HEAD_DIM)
            cmp_p.append(kv6(kv_cmp, bsz))
            sel_p.append(kv6(kv_sel, bsz))
            win_p.append(kv6(kv_win, bsz))
            proj_s = _norm_matmul(xs, g_in, w_in, tn=tn)
            q_s, kvc_s, kvs_s, kvw_s = _odd_post(
                proj_s, cos_s, sin_s, qg, kg, cw=cw, kvw=kvw, for_prompt=False)
            q3s = q_s.reshape(db, heads, HEAD_DIM)
            o_c, idx = _nsa_sample_select(q3s, means_s, groups=groups, rep=rep, q_pos=past_len,
                                          n_past_sel=n_past_sel)
            act_s, win_new = _nsa_sample_attend(
                page_table, idx[:, :, :n_past_sel], q3s, cache_sel5, li,
                kvs_s.reshape(db, 1, 2 * kvw), kvw_s.reshape(db, 1, 2 * kvw), o_c, win_state5,
                proj_s[:, g_off:g_off + groups * LANES].reshape(db, 1, groups * LANES),
                proj_s[:, z_off:z_off + cw].reshape(db, 1, cw),
                groups=groups, rep=rep, n_gather=n_past_sel)
            xs = _matmul_residual(act_s.reshape(db, cw), w_out, xs)
            cmp_s.append(kv6(kvc_s, db))
            sel_s.append(kv6(kvs_s, db))
            win_s.append(win_new.reshape(db, n_buf, 2, groups, HEAD_DIM))
    return (xp.reshape(bsz, seq, d_model), xs.reshape(db, 1, d_model),
            jnp.stack(cmp_p), jnp.stack(cmp_s), jnp.stack(sel_p), jnp.stack(sel_s),
            jnp.stack(win_p), jnp.stack(win_s), jnp.stack(pool_p), jnp.stack(pool_s), jnp.stack(gv_s))
```

```python
import functools

import jax
import jax.numpy as jnp
import numpy as np
from jax import lax
from jax.experimental import pallas as pl
from jax.experimental.pallas import tpu as pltpu

F32 = jnp.float32
BF16 = jnp.bfloat16
I32 = jnp.int32

EPS = 1e-6
PAGE_SIZE = 128
A_HEADS = 8
GMLP_CHUNK = 128
POOL_WINDOWS = (2, 4, 8, 16)
POOL_PAD = max(POOL_WINDOWS) - 1
HEAD_DIM = 128
C_KV_HEADS = 4
CMP_BLOCK = 64
N_SEL = 16
WINDOW = 512
SEL_FORCE = 1e4
NEG_INF = -1e30
ROPE_THETA = 500000.0
ROT_DIM = HEAD_DIM // 4
LOG2E = 1.4426950408889634
Q_SCALE = HEAD_DIM ** -0.5 * LOG2E

LANES = 128
SUBLANES = 8
MIB = 1024 * 1024


def _cparams(vmem_mib, semantics=None):
    return pltpu.CompilerParams(vmem_limit_bytes=int(vmem_mib * MIB), dimension_semantics=semantics)


def _silu(x):
    return x * (1.0 / (1.0 + jnp.exp(-x)))


def _sigmoid(x):
    return 1.0 / (1.0 + jnp.exp(-x))


def _dot_nt(a, b):
    return lax.dot_general(a, b, (((1,), (1,)), ((), ())), preferred_element_type=F32)


def _dot(a, b):
    return jnp.dot(a, b, preferred_element_type=F32)


def _norm_mm_kernel(x_ref, g_ref, w_ref, o_ref):
    x = x_ref[...]
    ms = jnp.mean(x * x, axis=-1, keepdims=True)
    h = (x * lax.rsqrt(ms + EPS) * g_ref[...]).astype(BF16)
    o_ref[...] = jnp.dot(h, w_ref[...], preferred_element_type=F32)


def _mm_res_kernel(a_ref, w_ref, r_ref, o_ref):
    o_ref[...] = r_ref[...] + jnp.dot(a_ref[...], w_ref[...], preferred_element_type=F32)


def _row_tile(m, want):
    return want if m % want == 0 else m


def _norm_matmul(x, g, w, *, tn, tm=512):
    m, k = x.shape
    n = w.shape[1]
    tm = _row_tile(m, tm)
    assert n % tn == 0
    vmem = 2 * (tm * k * 4 + k * tn * 2 + tm * tn * 4) / MIB + 8
    return pl.pallas_call(
        _norm_mm_kernel,
        grid=(n // tn, m // tm),
        in_specs=[
            pl.BlockSpec((tm, k), lambda j, i: (i, 0)),
            pl.BlockSpec((1, k), lambda j, i: (0, 0)),
            pl.BlockSpec((k, tn), lambda j, i: (0, j)),
        ],
        out_specs=pl.BlockSpec((tm, tn), lambda j, i: (i, j)),
        out_shape=jax.ShapeDtypeStruct((m, n), F32),
        compiler_params=_cparams(vmem),
        name="norm_matmul",
    )(x, g, w)


def _matmul_residual(a, w, res, *, tm=512):
    m, k = a.shape
    n = w.shape[1]
    tm = _row_tile(m, tm)
    vmem = 2 * (tm * k * 2 + k * n * 2 + 2 * tm * n * 4) / MIB + 8
    return pl.pallas_call(
        _mm_res_kernel,
        grid=(m // tm,),
        in_specs=[
            pl.BlockSpec((tm, k), lambda i: (i, 0)),
            pl.BlockSpec((k, n), lambda i: (0, 0)),
            pl.BlockSpec((tm, n), lambda i: (i, 0)),
        ],
        out_specs=pl.BlockSpec((tm, n), lambda i: (i, 0)),
        out_shape=jax.ShapeDtypeStruct((m, n), F32),
        compiler_params=_cparams(vmem),
        name="matmul_residual",
    )(a, w, res)


def _even_mix_kernel(proj_ref, vg_ref, ws_ref, bst_ref, wp_ref, ps_ref, act_ref, pool_ref, ext_ref,
                     *, tm, aw, bw):
    c = pl.program_id(1)
    n_c = pl.num_programs(1)
    hd = aw // A_HEADS
    pg = bw // len(POOL_WINDOWS)

    cl = GMLP_CHUNK
    row = lax.broadcasted_iota(I32, (cl, cl), 0)
    col = lax.broadcasted_iota(I32, (cl, cl), 1)
    causal = row >= col

    for h in range(A_HEADS):
        sl = slice(h * hd, (h + 1) * hd)
        w = jnp.where(causal, ws_ref[h], 0.0).astype(BF16)
        for c0 in range(0, tm, cl):
            rs = slice(c0, c0 + cl)
            u = proj_ref[0, rs, sl]
            v = proj_ref[0, rs, aw + h * hd:aw + (h + 1) * hd]
            za = proj_ref[0, rs, 2 * aw + h * hd:2 * aw + (h + 1) * hd]
            ms = jnp.mean(v * v, axis=-1, keepdims=True)
            vn = v * lax.rsqrt(ms + EPS) * vg_ref[:, sl]
            s = jnp.dot(w, vn.astype(BF16), preferred_element_type=F32) + bst_ref[:, h:h + 1]
            act_ref[0, rs, sl] = ((u * s) * _silu(za)).astype(BF16)

    p_off = 3 * aw
    zb_off = 3 * aw + bw

    @pl.when(c == 0)
    def _():
        ext_ref[0:16, :] = jnp.zeros((16, bw), F32)

    ext_ref[16:16 + tm, :] = proj_ref[0, :, p_off:p_off + bw]
    pos = c * tm + lax.broadcasted_iota(I32, (tm, 1), 0)
    for g, wnd in enumerate(POOL_WINDOWS):
        sl = slice(g * pg, (g + 1) * pg)
        cur = ext_ref[16:16 + tm, sl]
        acc = cur
        for k in range(1, wnd):
            acc = acc + ext_ref[16 - k:16 - k + tm, sl]
        cnt = jnp.minimum(pos + 1, wnd).astype(F32)
        d = acc / cnt - cur
        y = jnp.dot(d.astype(BF16), wp_ref[g], preferred_element_type=F32)
        zb = proj_ref[0, :, zb_off + g * pg:zb_off + (g + 1) * pg]
        act_ref[0, :, aw + g * pg:aw + (g + 1) * pg] = ((y * ps_ref[:, sl]) * _silu(zb)).astype(BF16)

    tail = ext_ref[tm:tm + 16, :]
    ext_ref[0:16, :] = tail

    @pl.when(c == n_c - 1)
    def _():
        pool_ref[0] = tail


def _even_mix(proj3, v_gain, w_s, b_st, w_pool, pool_scale):
    b, s, _ = proj3.shape
    aw = v_gain.shape[1]
    bw = pool_scale.shape[1]
    tm = 2 * GMLP_CHUNK if s % (2 * GMLP_CHUNK) == 0 else GMLP_CHUNK
    assert s % tm == 0 and s >= 16
    n_in = 3 * aw + 2 * bw
    kern = functools.partial(_even_mix_kernel, tm=tm, aw=aw, bw=bw)
    return pl.pallas_call(
        kern,
        grid=(b, s // tm),
        in_specs=[
            pl.BlockSpec((1, tm, n_in), lambda i, c: (i, c, 0)),
            pl.BlockSpec((1, aw), lambda i, c: (0, 0)),
            pl.BlockSpec(w_s.shape, lambda i, c: (0, 0, 0)),
            pl.BlockSpec(b_st.shape, lambda i, c: (0, 0)),
            pl.BlockSpec(w_pool.shape, lambda i, c: (0, 0, 0)),
            pl.BlockSpec((1, bw), lambda i, c: (0, 0)),
        ],
        out_specs=[
            pl.BlockSpec((1, tm, aw + bw), lambda i, c: (i, c, 0)),
            pl.BlockSpec((1, 16, bw), lambda i, c: (i, 0, 0)),
        ],
        out_shape=[
            jax.ShapeDtypeStruct((b, s, aw + bw), BF16),
            jax.ShapeDtypeStruct((b, 16, bw), F32),
        ],
        scratch_shapes=[pltpu.VMEM((16 + tm, bw), F32)],
        compiler_params=_cparams(32, ("arbitrary", "arbitrary")),
        name="even_mix",
    )(proj3, v_gain, w_s, b_st, w_pool, pool_scale)


def _even_mix_sample_kernel(proj_ref, vg_ref, w00_ref, b0_ref, wp_ref, ps_ref, st_ref,
                            act_ref, vn_ref, newst_ref, *, aw, bw, pos):
    hd = aw // A_HEADS
    pg = bw // len(POOL_WINDOWS)
    for h in range(A_HEADS):
        sl = slice(h * hd, (h + 1) * hd)
        u = proj_ref[:, sl]
        v = proj_ref[:, aw + h * hd:aw + (h + 1) * hd]
        za = proj_ref[:, 2 * aw + h * hd:2 * aw + (h + 1) * hd]
        ms = jnp.mean(v * v, axis=-1, keepdims=True)
        vn = v * lax.rsqrt(ms + EPS) * vg_ref[:, sl]
        vn_ref[:, sl] = vn
        s = w00_ref[:, sl] * vn + b0_ref[:, sl]
        act_ref[:, sl] = ((u * s) * _silu(za)).astype(BF16)

    p = proj_ref[:, 3 * aw:3 * aw + bw]
    for g, wnd in enumerate(POOL_WINDOWS):
        sl = slice(g * pg, (g + 1) * pg)
        cur = p[:, sl]
        acc = cur
        for k in range(1, wnd):
            acc = acc + st_ref[POOL_PAD - k, :, sl]
        cnt = float(min(pos + 1, wnd))
        d = acc / cnt - cur
        y = jnp.dot(d.astype(BF16), wp_ref[g], preferred_element_type=F32)
        zb = proj_ref[:, 3 * aw + bw + g * pg:3 * aw + bw + (g + 1) * pg]
        act_ref[:, aw + g * pg:aw + (g + 1) * pg] = ((y * ps_ref[:, sl]) * _silu(zb)).astype(BF16)

    for k in range(POOL_PAD - 1):
        newst_ref[k] = st_ref[k + 1]
    newst_ref[POOL_PAD - 1] = p


def _even_mix_sample(proj, v_gain, w00, b0, w_pool, pool_scale, state_t, pos):
    db = proj.shape[0]
    aw = v_gain.shape[1]
    bw = pool_scale.shape[1]
    kern = functools.partial(_even_mix_sample_kernel, aw=aw, bw=bw, pos=pos)
    return pl.pallas_call(
        kern,
        out_shape=[
            jax.ShapeDtypeStruct((db, aw + bw), BF16),
            jax.ShapeDtypeStruct((db, aw), F32),
            jax.ShapeDtypeStruct((POOL_PAD, db, bw), F32),
        ],
        name="even_mix_sample",
    )(proj, v_gain, w00, b0, w_pool, pool_scale, state_t)


def _odd_post_kernel(proj_ref, cos_ref, sin_ref, qg_ref, kg_ref, *out_refs, tm, cw, kvw, for_prompt):
    q_out, cmp_out, sel_out, win_out = out_refs[:4]
    if for_prompt:
        selkt_out, selv_out, winkt_out, winv_out, means_out = out_refs[4:]
        attn_outs = (None, (selkt_out, selv_out), (winkt_out, winv_out))
    else:
        attn_outs = (None, None, None)
    cosf = cos_ref[...]
    sinf = sin_ref[...]
    half = ROT_DIM // 2
    first = lax.broadcasted_iota(I32, (tm, HEAD_DIM), 1) < half

    def norm_rope(x, gain):
        ms = jnp.mean(x * x, axis=-1, keepdims=True)
        y = x * lax.rsqrt(ms + EPS) * gain
        rot = jnp.where(first, pltpu.roll(y, HEAD_DIM - half, 1), pltpu.roll(y, half, 1))
        return y * cosf + rot * sinf

    qg = qg_ref[...]
    for h in range(cw // HEAD_DIM):
        sl = slice(h * HEAD_DIM, (h + 1) * HEAD_DIM)
        q_out[:, sl] = (norm_rope(proj_ref[:, sl], qg) * Q_SCALE).astype(BF16)

    for br, (o32, attn) in enumerate(zip((cmp_out, sel_out, win_out), attn_outs)):
        k_off = cw + br * 2 * kvw
        v_off = k_off + kvw
        kg = kg_ref[br:br + 1, :]
        pieces = []
        for g in range(kvw // HEAD_DIM):
            sl = slice(g * HEAD_DIM, (g + 1) * HEAD_DIM)
            kk = norm_rope(proj_ref[:, k_off + g * HEAD_DIM:k_off + (g + 1) * HEAD_DIM], kg)
            pieces.append(kk)
            if attn is not None:
                attn[0][sl, :] = kk.T.astype(BF16)
        vv = proj_ref[:, v_off:v_off + kvw]
        if attn is not None:
            attn[1][...] = vv.astype(BF16)
        row = jnp.concatenate(pieces + [vv], axis=1)
        o32[...] = row.reshape(o32.shape)
        if br == 0 and for_prompt:
            for j in range(tm // CMP_BLOCK):
                blk = row[j * CMP_BLOCK:(j + 1) * CMP_BLOCK, :]
                means_out[0, j:j + 1, :] = jnp.mean(blk, axis=0, keepdims=True)


def _odd_post(proj, cos_t, sin_t, q_gain, k_gain, *, cw, kvw, for_prompt, win_keep=0, tm=512):
    m = proj.shape[0]
    tm = _row_tile(m, tm)
    n_pos_tiles = cos_t.shape[0] // tm
    n_used = cw + 6 * kvw
    kern = functools.partial(_odd_post_kernel, tm=tm, cw=cw, kvw=kvw, for_prompt=for_prompt)
    row_spec = lambda w: pl.BlockSpec((tm, w), lambda i: (i, 0))
    n_rows = 2 * kvw // HEAD_DIM
    if for_prompt:
        assert win_keep % tm == 0 and cos_t.shape[0] >= win_keep
        keep_tiles = win_keep // tm
        skip_tiles = n_pos_tiles - keep_tiles
        kv_shape = jax.ShapeDtypeStruct((m, n_rows, HEAD_DIM), F32)
        kv_spec = pl.BlockSpec((tm, n_rows, HEAD_DIM), lambda i: (i, 0, 0))
        win_shape = jax.ShapeDtypeStruct((m // n_pos_tiles * keep_tiles, n_rows, HEAD_DIM), F32)
        win_spec = pl.BlockSpec(
            (tm, n_rows, HEAD_DIM),
            lambda i: (i // n_pos_tiles * keep_tiles + jnp.maximum(i % n_pos_tiles - skip_tiles, 0), 0, 0))
    else:
        kv_shape = win_shape = jax.ShapeDtypeStruct((m, 2 * kvw), F32)
        kv_spec = win_spec = row_spec(2 * kvw)
    out_shape = [jax.ShapeDtypeStruct((m, cw), BF16), kv_shape, kv_shape, win_shape]
    out_specs = [row_spec(cw), kv_spec, kv_spec, win_spec]
    if for_prompt:
        assert tm % CMP_BLOCK == 0 and tm % LANES == 0
        for _ in range(2):
            out_shape += [jax.ShapeDtypeStruct((kvw, m), BF16), jax.ShapeDtypeStruct((m, kvw), BF16)]
            out_specs += [pl.BlockSpec((kvw, tm), lambda i: (0, i)), row_spec(kvw)]
        out_shape.append(jax.ShapeDtypeStruct((m // tm, tm // CMP_BLOCK, 2 * kvw), F32))
        out_specs.append(pl.BlockSpec((1, tm // CMP_BLOCK, 2 * kvw), lambda i: (i, 0, 0)))
    return pl.pallas_call(
        kern,
        grid=(m // tm,),
        in_specs=[
            pl.BlockSpec((tm, n_used), lambda i: (i, 0)),
            pl.BlockSpec((tm, HEAD_DIM), lambda i: (i % n_pos_tiles, 0)),
            pl.BlockSpec((tm, HEAD_DIM), lambda i: (i % n_pos_tiles, 0)),
            pl.BlockSpec((1, HEAD_DIM), lambda i: (0, 0)),
            pl.BlockSpec((3, HEAD_DIM), lambda i: (0, 0)),
        ],
        out_specs=out_specs,
        out_shape=out_shape,
        compiler_params=_cparams(56),
        name="odd_post",
    )(proj, cos_t, sin_t, q_gain, k_gain)


def _nsa_sub_tile(i, h, q_ref, ks_ref, vs_ref, kw_ref, vw_ref, kc, vc, band_ref, tri_ref, eye_ref,
                  *, tq, nb, rep, lw):
    rows = rep * tq
    r0 = h * tq
    q = jnp.concatenate([q_ref[0, r0:r0 + tq, r * HEAD_DIM:(r + 1) * HEAD_DIM] for r in range(rep)],
                        axis=0)
    t_col = i * tq + lax.broadcasted_iota(I32, (tq, 1), 0)
    t_row = i * tq + lax.broadcasted_iota(I32, (1, tq), 1)

    blk_row = lax.broadcasted_iota(I32, (1, nb), 1)
    s_c = _dot_nt(q, kc).reshape(rep, tq, nb)
    ok_c = ((blk_row + 1) * CMP_BLOCK - 1) <= t_col
    s_c = s_c + jnp.where(ok_c, 0.0, NEG_INF)[None]
    m_c = jnp.max(s_c, axis=-1, keepdims=True)
    e_c = jnp.exp2(s_c - m_c)
    p_c = e_c / jnp.sum(e_c, axis=-1, keepdims=True) * jnp.where(ok_c, 1.0, 0.0)[None]
    o_c = jnp.dot(p_c.reshape(rows, nb).astype(BF16), vc, preferred_element_type=F32)

    blk_col = lax.broadcasted_iota(I32, (nb, 1), 0)
    s_t = _dot_nt(kc, q)
    ok_t = ((blk_col + 1) * CMP_BLOCK - 1) <= t_row
    bias_t = jnp.where(ok_t, 0.0, NEG_INF)
    okf_t = jnp.where(ok_t, 1.0, 0.0)
    imp = jnp.zeros((nb, tq), F32)
    for r in range(rep):
        s_r = s_t[:, r * tq:(r + 1) * tq] + bias_t
        e_r = jnp.exp2(s_r - jnp.max(s_r, axis=0, keepdims=True))
        imp = imp + e_r / jnp.sum(e_r, axis=0, keepdims=True) * okf_t
    cur = t_row // CMP_BLOCK
    forced = (blk_col == 0) | (blk_col == cur) | (blk_col == cur - 1)
    imp = jnp.where(blk_col <= cur, jnp.where(forced, SEL_FORCE, imp), -1.0)

    blk8 = lax.broadcasted_iota(I32, (SUBLANES, 1), 0)
    cnt = jnp.zeros((nb, tq), F32)
    for j in range(nb):
        rj = imp[j:j + 1, :]
        parts = []
        for v0 in range(0, nb, SUBLANES):
            sub = imp[v0:v0 + SUBLANES, :]
            if v0 > j:
                beats = rj >= sub
            elif v0 + SUBLANES - 1 <= j:
                beats = rj > sub
            else:
                beats = (rj > sub) | ((rj == sub) & (blk8 > j - v0))
            parts.append(jnp.where(beats, 1.0, 0.0))
        cnt = cnt + jnp.concatenate(parts, axis=0)
    keep = (cnt < min(N_SEL, nb)) & (imp >= 0.0) & (blk_col < i * (tq // CMP_BLOCK))
    drop_t = jnp.where(keep, 0.0, 1.0)
    if LANES > nb:
        drop_t = jnp.concatenate([drop_t, jnp.zeros((LANES - nb, tq), F32)], axis=0)
    drop = drop_t.T.astype(BF16)
    q_drop = jnp.concatenate([q, jnp.concatenate([drop] * rep, axis=0)], axis=1)

    q_eye = jnp.concatenate([q, eye_ref[...]], axis=1)

    w0 = pl.multiple_of(jnp.maximum(i * tq + tq - lw, 0), tq)
    band = band_ref[jnp.minimum(i, band_ref.shape[0] - 1)]
    k_w = jnp.concatenate([kw_ref[:, pl.ds(w0, lw)], band], axis=0)
    v_w = jnp.concatenate([vw_ref[0, pl.ds(w0, lw), :], jnp.ones((lw, HEAD_DIM), BF16)], axis=1)
    s_w = _dot(q_eye, k_w)
    p_w = jnp.exp2(s_w - jnp.max(s_w, axis=-1, keepdims=True))
    acc_w = _dot(p_w.astype(BF16), v_w)
    o_w = acc_w[:, :HEAD_DIM] / acc_w[:, HEAD_DIM:]

    d0 = pl.multiple_of(i * tq, tq)
    k_d = jnp.concatenate([ks_ref[:, pl.ds(d0, tq)], tri_ref[...]], axis=0)
    v_d = jnp.concatenate([vs_ref[0, pl.ds(d0, tq), :], jnp.ones((tq, HEAD_DIM), BF16)], axis=1)
    s_d = _dot(q_eye, k_d)
    m_d = jnp.max(s_d, axis=-1, keepdims=True)
    acc_d = _dot(jnp.exp2(s_d - m_d).astype(BF16), v_d)
    return q_drop, o_c, o_w, m_d, acc_d


def _nsa_prompt_kernel(pt_ref, q_ref, ks_ref, vs_ref, kw_ref, vw_ref, kc_ref, vc_ref, g_ref, z_ref,
                       eb_ref, band_ref, tri_ref, eye_ref, *rest, tq, nh, tk, seq, nb, rep, lw, pps):
    if pps:
        o_ref, pm_ref, s0_ref, s1_ref = rest[pps:]
        _store_page_means(rest[:pps], pm_ref)
    else:
        o_ref, s0_ref, s1_ref = rest
    step = pl.program_id(2)
    kc = kc_ref[0].astype(BF16)
    vc = vc_ref[0].astype(BF16)
    subs = [_nsa_sub_tile(step * nh + h, h, q_ref, ks_ref, vs_ref, kw_ref, vw_ref, kc, vc, band_ref,
                          tri_ref, eye_ref, tq=tq, nb=nb, rep=rep, lw=lw) for h in range(nh)]

    q_drop = jnp.concatenate([s[0] for s in subs], axis=0)
    m_0 = jnp.concatenate([s[3] for s in subs], axis=0)
    acc_0 = jnp.concatenate([s[4] for s in subs], axis=0)

    partial, gate_s, z_act = [], [], []
    for h in range(nh):
        ts = slice(h * tq, (h + 1) * tq)
        gs = _sigmoid(g_ref[0, ts, :])
        o_c, o_w = subs[h][1], subs[h][2]
        for r in range(rep):
            rs = slice(r * tq, (r + 1) * tq)
            partial.append(gs[:, 3 * r:3 * r + 1] * o_c[rs] + gs[:, 3 * r + 2:3 * r + 3] * o_w[rs])
            gate_s.append(gs[:, 3 * r + 1:3 * r + 2])
            z_act.append(_silu(z_ref[0, ts, r * HEAD_DIM:(r + 1) * HEAD_DIM]))

    ones_k = jnp.ones((tk, HEAD_DIM), BF16)
    n_tiles = seq // tk

    def tile_scores(t):
        k0 = pl.multiple_of(t * tk, tk)
        k = jnp.concatenate([ks_ref[:, pl.ds(k0, tk)], eb_ref[:, pl.ds(k0, tk)]], axis=0)
        return _dot(q_drop, k)

    def accumulate(carry, s, t):
        m_p, acc = carry
        k0 = pl.multiple_of(t * tk, tk)
        v = jnp.concatenate([vs_ref[0, pl.ds(k0, tk), :], ones_k], axis=1)
        m_n = jnp.maximum(m_p, jnp.max(s, axis=-1, keepdims=True))
        pv = _dot(jnp.exp2(s - m_n).astype(BF16), v)
        return m_n, jnp.exp2(m_p - m_n) * acc + pv

    def sel_body(j, carry):
        t1 = 2 * j + 1
        s1_ref[...] = tile_scores(t1)
        carry = accumulate(carry, s0_ref[...], 2 * j)
        s0_ref[...] = tile_scores(jnp.minimum(t1 + 1, n_tiles - 1))
        return accumulate(carry, s1_ref[...], t1)

    n_past = ((step * nh + nh - 1) * tq + tk - 1) // tk
    s0_ref[...] = tile_scores(0)
    _, acc_s = lax.fori_loop(0, (n_past + 1) // 2, sel_body, (m_0, acc_0))
    o_s = acc_s[:, :HEAD_DIM] / acc_s[:, HEAD_DIM:]

    for h in range(nh):
        for r in range(rep):
            n = h * rep + r
            o = partial[n] + gate_s[n] * o_s[n * tq:(n + 1) * tq]
            o_ref[0, h * tq:(h + 1) * tq, r * HEAD_DIM:(r + 1) * HEAD_DIM] = (o * z_act[n]).astype(BF16)


def _mask_tables(seq, tq, lw, rep):
    neg = lambda ok: jnp.where(ok, 0.0, NEG_INF).astype(BF16)
    row = jnp.arange(LANES, dtype=I32)[:, None]
    key = jnp.arange(seq, dtype=I32)[None, :]
    block_rows = neg(key // CMP_BLOCK != row)
    t = jnp.arange(tq, dtype=I32)[None, :, None]
    c = jnp.arange(lw, dtype=I32)[None, None, :]
    off = jnp.arange(WINDOW // tq + 1, dtype=I32)[:, None, None] * tq
    rel = off + t - c
    band = neg((rel >= 0) & (rel < WINDOW))
    tri = neg(jnp.arange(tq, dtype=I32)[None, :] <= jnp.arange(tq, dtype=I32)[:, None])
    eye = jnp.tile(jnp.eye(tq, dtype=BF16), (rep, 1))
    return block_rows, band, tri, eye


def _page_job_split(n_steps, page_table):
    db, n_pages = page_table.shape
    total = db * n_pages
    if total % n_steps:
        return 0
    pps = total // n_steps
    return pps if n_pages % pps == 0 else 0


def _nsa_prompt(q3, kt_sel, v_sel, kt_win, v_win, means3, proj3, tables, cache5, page_table, layer, *,
                cw, z_off, g_off, tq=LANES, nh=2, tk=512):
    b, s, _ = q3.shape
    kvw = v_sel.shape[2]
    groups = kvw // HEAD_DIM
    rep = cw // kvw
    nb = means3.shape[1]
    gw = rep * HEAD_DIM
    block_rows, band, tri, eye = tables
    tk = min(tk, s)
    lw = band.shape[2]
    ts = nh * tq
    assert tq == LANES and nb <= LANES and nb % SUBLANES == 0 and lw == WINDOW + tq and s >= lw
    assert s % ts == 0 and s % (2 * tk) == 0 and tk % tq == 0 and tq % CMP_BLOCK == 0
    assert nb * CMP_BLOCK == s
    assert z_off % gw == 0 and g_off % LANES == 0
    n_i = s // ts
    pps = _page_job_split(b * groups * n_i, page_table)
    kern = functools.partial(_nsa_prompt_kernel, tq=tq, nh=nh, tk=tk, seq=s, nb=nb, rep=rep, lw=lw,
                             pps=pps)
    db, n_pages = page_table.shape
    n_rows, bpp = cache5.shape[3], PAGE_SIZE // CMP_BLOCK
    spp = n_pages // pps if pps else 0
    assert b * groups * n_i * pps in (0, db * n_pages)
    flat = lambda bi, g, i: (bi * groups + g) * n_i + i

    def page_spec(r):
        def index_map(bi, g, i, pt):
            f = flat(bi, g, i)
            return (layer, pt[f // spp, (f % spp) * pps + r], 0, 0, 0)
        return pl.BlockSpec((None, None, PAGE_SIZE, n_rows, HEAD_DIM), index_map)

    kt_spec = pl.BlockSpec((HEAD_DIM, s), lambda bi, g, i, pt: (g, bi))
    v_spec = pl.BlockSpec((1, s, HEAD_DIM), lambda bi, g, i, pt: (bi, 0, g))
    mean_spec = lambda off: pl.BlockSpec((1, nb, HEAD_DIM), lambda bi, g, i, pt: (bi, 0, off + g))
    grid_spec = pltpu.PrefetchScalarGridSpec(
        num_scalar_prefetch=1,
        grid=(b, groups, n_i),
        in_specs=[
            pl.BlockSpec((1, ts, gw), lambda bi, g, i, pt: (bi, i, g)),
            kt_spec, v_spec, kt_spec, v_spec,
            mean_spec(0), mean_spec(groups),
            pl.BlockSpec((1, ts, LANES), lambda bi, g, i, pt: (bi, i, g_off // LANES + g)),
            pl.BlockSpec((1, ts, gw), lambda bi, g, i, pt: (bi, i, z_off // gw + g)),
            pl.BlockSpec(block_rows.shape, lambda bi, g, i, pt: (0, 0)),
            pl.BlockSpec(band.shape, lambda bi, g, i, pt: (0, 0, 0)),
            pl.BlockSpec(tri.shape, lambda bi, g, i, pt: (0, 0)),
            pl.BlockSpec(eye.shape, lambda bi, g, i, pt: (0, 0)),
        ] + [page_spec(r) for r in range(pps)],
        out_specs=[pl.BlockSpec((1, ts, gw), lambda bi, g, i, pt: (bi, i, g))] + ([
            pl.BlockSpec((1, n_rows, pps * bpp, HEAD_DIM),
                         lambda bi, g, i, pt: (flat(bi, g, i) // spp, 0, flat(bi, g, i) % spp, 0)),
        ] if pps else []),
        scratch_shapes=[pltpu.VMEM((nh * rep * tq, tk), F32), pltpu.VMEM((nh * rep * tq, tk), F32)],
    )
    outs = pl.pallas_call(
        kern,
        grid_spec=grid_spec,
        out_shape=[jax.ShapeDtypeStruct((b, s, cw), BF16)] + ([
            jax.ShapeDtypeStruct((db, n_rows, n_pages * bpp, HEAD_DIM), F32)] if pps else []),
        compiler_params=_cparams(56),
        name="nsa_prompt",
    )(page_table, q3, kt_sel, v_sel, kt_win, v_win, means3, means3, proj3, proj3, block_rows, band, tri,
      eye, *([cache5] * pps))
    return outs if pps else (outs[0], _page_means(cache5, page_table, layer))


def _store_page_means(page_refs, out_ref):
    bpp = PAGE_SIZE // CMP_BLOCK
    n_rows = out_ref.shape[1]
    for r, page_ref in enumerate(page_refs):
        for h in range(bpp):
            mean = jnp.mean(page_ref[h * CMP_BLOCK:(h + 1) * CMP_BLOCK], axis=0)
            n = r * bpp + h
            for j in range(n_rows):
                out_ref[0, j, n:n + 1, :] = mean[j:j + 1, :]


def _page_means_kernel(pt_ref, *refs, pps):
    _store_page_means(refs[:pps], refs[pps])


def _page_means(cache5, page_table, layer, *, pps=8):
    db, n_pages = page_table.shape
    n_rows, d = cache5.shape[3], cache5.shape[4]
    pps = pps if n_pages % pps == 0 else 1
    bpp = PAGE_SIZE // CMP_BLOCK

    def page_spec(r):
        return pl.BlockSpec((None, None, PAGE_SIZE, n_rows, d),
                            lambda bi, j, pt: (layer, pt[bi, j * pps + r], 0, 0, 0))

    grid_spec = pltpu.PrefetchScalarGridSpec(
        num_scalar_prefetch=1,
        grid=(db, n_pages // pps),
        in_specs=[page_spec(r) for r in range(pps)],
        out_specs=pl.BlockSpec((1, n_rows, pps * bpp, d), lambda bi, j, pt: (bi, 0, j, 0)),
    )
    return pl.pallas_call(
        functools.partial(_page_means_kernel, pps=pps),
        grid_spec=grid_spec,
        out_shape=jax.ShapeDtypeStruct((db, n_rows, n_pages * bpp, d), F32),
        compiler_params=_cparams(24),
        name="page_means",
    )(page_table, *([cache5] * pps))


def _nsa_sample_select_kernel(q_ref, mean_ref, oc_ref, idx_ref, *, nbp, groups, rep, q_pos, n_past_sel):
    q = q_ref[0]
    heads = groups * rep
    lane = lax.broadcasted_iota(I32, (1, nbp), 1)
    ri = lax.broadcasted_iota(I32, (nbp, nbp), 0)
    ci = lax.broadcasted_iota(I32, (nbp, nbp), 1)
    diag = ri == ci
    ok = ((lane + 1) * CMP_BLOCK - 1) <= q_pos
    cur = q_pos // CMP_BLOCK
    forced = (lane == 0) | (lane == cur) | (lane == cur - 1)
    hrow = lax.broadcasted_iota(I32, (heads, 1), 0) // rep
    out_lane = lax.broadcasted_iota(I32, (1, LANES), 1)
    o_c = jnp.zeros((heads, HEAD_DIM), F32)
    for g in range(groups):
        kc = mean_ref[0, g].astype(BF16)
        vc = mean_ref[0, groups + g].astype(BF16)
        s = _dot_nt(q, kc) + jnp.where(ok, 0.0, NEG_INF)
        e = jnp.exp2(s - jnp.max(s, axis=-1, keepdims=True))
        p = e / jnp.sum(e, axis=-1, keepdims=True) * jnp.where(ok, 1.0, 0.0)
        in_g = hrow == g
        o_c = o_c + jnp.where(in_g, jnp.dot(p.astype(BF16), vc, preferred_element_type=F32), 0.0)
        imp = jnp.sum(jnp.where(in_g, p, 0.0), axis=0, keepdims=True)
        imp = jnp.where(lane <= cur, jnp.where(forced, SEL_FORCE, imp), -1.0)
        imp_b = jnp.broadcast_to(imp, (nbp, nbp))
        imp_col = jnp.sum(jnp.where(diag, imp_b, 0.0), axis=1, keepdims=True)
        beats = (imp_col > imp_b) | ((imp_col == imp_b) & (ri < ci))
        cnt = jnp.sum(jnp.where(beats, 1.0, 0.0), axis=0, keepdims=True)
        sel = jnp.where((cnt < n_past_sel) & (imp >= 0.0), 1.0, 0.0)
        sel_col = jnp.sum(jnp.where(diag, jnp.broadcast_to(sel, (nbp, nbp)), 0.0), axis=1, keepdims=True)
        before = jnp.sum(jnp.where(ri < ci, sel_col, 0.0), axis=0, keepdims=True)
        row = jnp.zeros((1, LANES), F32)
        for k in range(n_past_sel):
            hit = (sel > 0.5) & (before == float(k))
            idx_k = jnp.sum(jnp.where(hit, lane.astype(F32), 0.0), axis=1, keepdims=True)
            row = jnp.where(out_lane == k, idx_k, row)
        idx_ref[0, g:g + 1, :] = row.astype(I32)
    oc_ref[0] = o_c


def _nsa_sample_select(q3, means, *, groups, rep, q_pos, n_past_sel):
    db, heads, _ = q3.shape
    nbp = means.shape[2]
    kern = functools.partial(_nsa_sample_select_kernel, nbp=nbp, groups=groups, rep=rep, q_pos=q_pos,
                             n_past_sel=n_past_sel)
    return pl.pallas_call(
        kern,
        grid=(db,),
        in_specs=[
            pl.BlockSpec((1, heads, HEAD_DIM), lambda bi: (bi, 0, 0)),
            pl.BlockSpec((1,) + means.shape[1:], lambda bi: (bi, 0, 0, 0)),
        ],
        out_specs=[
            pl.BlockSpec((1, heads, HEAD_DIM), lambda bi: (bi, 0, 0)),
            pl.BlockSpec((1, groups, LANES), lambda bi: (bi, 0, 0)),
        ],
        out_shape=[
            jax.ShapeDtypeStruct((db, heads, HEAD_DIM), F32),
            jax.ShapeDtypeStruct((db, groups, LANES), I32),
        ],
        compiler_params=_cparams(24),
        name="nsa_sample_select",
    )(q3, means)


def _nsa_sample_attend_kernel(pt_ref, si_ref, q_ref, *refs, groups, rep, n_steps, per_step, n_buf):
    kv_refs = refs[:groups * per_step]
    (newsel_ref, newwin_ref, oc_ref, win_ref, g_ref, z_ref,
     act_ref, winout_ref, m_ref, l_ref, acc_ref) = refs[groups * per_step:]
    kvw = groups * HEAD_DIM
    heads = groups * rep
    n_rows = 2 * groups
    k_id = pl.program_id(1)
    q = q_ref[0]
    qf = q.astype(F32)
    hgrp = lax.broadcasted_iota(I32, (heads, 1), 0) // rep

    def own_key_rows(n_tok):
        lane = lax.broadcasted_iota(I32, (1, n_tok * n_rows), 1)
        return lane % n_rows == hgrp, lane // n_rows

    def by_group(fn):
        out = None
        for g in range(groups):
            val = jnp.where(hgrp == g, fn(g), 0.0)
            out = val if out is None else out + val
        return out

    def new_token_scores(row_ref):
        return by_group(lambda g: jnp.sum(
            qf * row_ref[0, :, g * HEAD_DIM:(g + 1) * HEAD_DIM], axis=-1, keepdims=True))

    def new_token_values(row_ref):
        return by_group(lambda g: jnp.broadcast_to(
            row_ref[0, :, kvw + g * HEAD_DIM:kvw + (g + 1) * HEAD_DIM], (heads, HEAD_DIM)))

    @pl.when(k_id == 0)
    def _():
        m_ref[...] = jnp.full((heads, 1), NEG_INF, F32)
        l_ref[...] = jnp.zeros((heads, 1), F32)
        acc_ref[...] = jnp.zeros((heads, HEAD_DIM), F32)

    xs = [jnp.concatenate([kv_refs[g * per_step + u][...].reshape(CMP_BLOCK * n_rows, HEAD_DIM)
                           for u in range(per_step)], axis=0).astype(BF16) for g in range(groups)]
    own, _ = own_key_rows(CMP_BLOCK * per_step)
    s = by_group(lambda g: _dot_nt(q, xs[g])) + jnp.where(own, 0.0, NEG_INF)
    m_p = m_ref[...]
    m_n = jnp.maximum(m_p, jnp.max(s, axis=-1, keepdims=True))
    alpha = jnp.exp2(m_p - m_n)
    p = jnp.exp2(s - m_n)
    pb = pltpu.roll(p, groups, 1).astype(BF16)
    pv = by_group(lambda g: jnp.dot(pb, xs[g], preferred_element_type=F32))
    m_ref[...] = m_n
    l_ref[...] = alpha * l_ref[...] + jnp.sum(p, axis=-1, keepdims=True)
    acc_ref[...] = alpha * acc_ref[...] + pv

    @pl.when(k_id == n_steps - 1)
    def _():
        s_n = new_token_scores(newsel_ref)
        m_p = m_ref[...]
        m_n = jnp.maximum(m_p, s_n)
        alpha = jnp.exp2(m_p - m_n)
        p_n = jnp.exp2(s_n - m_n)
        l_s = alpha * l_ref[...] + p_n
        o_s = (alpha * acc_ref[...] + p_n * new_token_values(newsel_ref)) / l_s

        xw = win_ref[0].reshape(n_buf * n_rows, HEAD_DIM).astype(BF16)
        own_w, tok_w = own_key_rows(n_buf)
        ok_w = own_w & ((n_buf - tok_w) < WINDOW)
        s_w = _dot_nt(q, xw) + jnp.where(ok_w, 0.0, NEG_INF)
        s_wn = new_token_scores(newwin_ref)
        m_w = jnp.maximum(jnp.max(s_w, axis=-1, keepdims=True), s_wn)
        p_w = jnp.exp2(s_w - m_w)
        p_wn = jnp.exp2(s_wn - m_w)
        l_w = jnp.sum(p_w, axis=-1, keepdims=True) + p_wn
        o_w = jnp.dot(pltpu.roll(p_w, groups, 1).astype(BF16), xw, preferred_element_type=F32)
        o_w = (o_w + p_wn * new_token_values(newwin_ref)) / l_w

        winout_ref[0, 0:n_buf - 1] = win_ref[0, 1:n_buf]
        for j in range(n_rows):
            winout_ref[0, n_buf - 1, j:j + 1, :] = newwin_ref[0, :, j * HEAD_DIM:(j + 1) * HEAD_DIM]

        graw = jnp.broadcast_to(g_ref[0], (heads, groups * LANES))
        glane = lax.broadcasted_iota(I32, (heads, groups * LANES), 1)
        hidx = lax.broadcasted_iota(I32, (heads, 1), 0)
        gbase = (hidx // rep) * LANES + 3 * (hidx % rep)
        gate = lambda c: _sigmoid(jnp.sum(jnp.where(glane == gbase + c, graw, 0.0), axis=-1, keepdims=True))
        o = gate(0) * oc_ref[0] + gate(1) * o_s + gate(2) * o_w
        for h in range(heads):
            z = z_ref[0, :, h * HEAD_DIM:(h + 1) * HEAD_DIM]
            act_ref[0, :, h * HEAD_DIM:(h + 1) * HEAD_DIM] = (o[h:h + 1, :] * _silu(z)).astype(BF16)


def _nsa_sample_attend(page_table, sel_idx, q3, cache5, layer, new_sel, new_win, o_c, win_state, gates3,
                       z3, *, groups, rep, n_gather):
    db, heads, _ = q3.shape
    kvw = groups * HEAD_DIM
    n_buf = win_state.shape[2]
    n_rows = 2 * groups
    bpp = PAGE_SIZE // CMP_BLOCK
    assert sel_idx.shape[2] == n_gather
    per_step = max(u for u in range(1, 6) if n_gather % u == 0)
    n_steps = n_gather // per_step

    def gather_spec(g, u):
        def index_map(bi, k, pt, si):
            blk = si[bi, g, k * per_step + u]
            return (layer, pt[bi, blk // bpp], blk % bpp, 0, 0)
        return pl.BlockSpec((None, None, CMP_BLOCK, n_rows, HEAD_DIM), index_map)

    in_specs = [pl.BlockSpec((1, heads, HEAD_DIM), lambda bi, k, pt, si: (bi, 0, 0))]
    in_specs += [gather_spec(g, u) for g in range(groups) for u in range(per_step)]
    row3 = lambda w: pl.BlockSpec((1, 1, w), lambda bi, k, pt, si: (bi, 0, 0))
    in_specs += [
        row3(2 * kvw), row3(2 * kvw),
        pl.BlockSpec((1, heads, HEAD_DIM), lambda bi, k, pt, si: (bi, 0, 0)),
        pl.BlockSpec((None, 1, n_buf, n_rows, HEAD_DIM), lambda bi, k, pt, si: (layer, bi, 0, 0, 0)),
        row3(groups * LANES), row3(heads * HEAD_DIM),
    ]
    grid_spec = pltpu.PrefetchScalarGridSpec(
        num_scalar_prefetch=2,
        grid=(db, n_steps),
        in_specs=in_specs,
        out_specs=[
            row3(heads * HEAD_DIM),
            pl.BlockSpec((1, n_buf, n_rows, HEAD_DIM), lambda bi, k, pt, si: (bi, 0, 0, 0)),
        ],
        scratch_shapes=[pltpu.VMEM((heads, 1), F32), pltpu.VMEM((heads, 1), F32),
                        pltpu.VMEM((heads, HEAD_DIM), F32)],
    )
    kern = functools.partial(_nsa_sample_attend_kernel, groups=groups, rep=rep, n_steps=n_steps,
                             per_step=per_step, n_buf=n_buf)
    return pl.pallas_call(
        kern,
        grid_spec=grid_spec,
        out_shape=[
            jax.ShapeDtypeStruct((db, 1, heads * HEAD_DIM), BF16),
            jax.ShapeDtypeStruct((db, n_buf, n_rows, HEAD_DIM), F32),
        ],
        compiler_params=_cparams(32, ("arbitrary", "arbitrary")),
        name="nsa_sample_attend",
    )(page_table, sel_idx, q3, *([cache5] * (groups * per_step)), new_sel, new_win, o_c, win_state,
      gates3, z3)


def _rope_tables(pos):
    half = ROT_DIM // 2
    inv_freq = ROPE_THETA ** (-jnp.arange(half, dtype=F32) * (2.0 / ROT_DIM))
    ang = pos.astype(F32)[:, None] * inv_freq[None, :]
    cos, sin = jnp.cos(ang), jnp.sin(ang)
    n = pos.shape[0]
    cos_t = jnp.concatenate([cos, cos, jnp.ones((n, HEAD_DIM - ROT_DIM), F32)], axis=1)
    sin_t = jnp.concatenate([-sin, sin, jnp.zeros((n, HEAD_DIM - ROT_DIM), F32)], axis=1)
    return cos_t, sin_t


def _odd_weight_layout(w, cw, kvw, rep):
    k = w.shape[0]
    groups = kvw // HEAD_DIM
    n_qkv = cw + 6 * kvw
    n_gate = 3 * groups * rep
    gates = w[:, n_qkv:n_qkv + n_gate].astype(BF16).reshape(k, groups, 3 * rep)
    gates = jnp.pad(gates, ((0, 0), (0, 0), (0, LANES - 3 * rep))).reshape(k, groups * LANES)
    w_zg = jnp.concatenate([w[:, n_qkv + n_gate:].astype(BF16), gates], axis=1)
    return w[:, :n_qkv].astype(BF16), w_zg


def kernel(x_prompt, x_sample, cache_cmp_kv, cache_sel_kv, state_win_kv, state_pool, page_table,
           norm_even, w_in_even, v_norm, w_spatial, b_spatial, w_pool, pool_scale, w_out_even,
           norm_odd, w_in_odd, q_norm, k_norm, w_out_odd):
    bsz, seq, d_model = x_prompt.shape
    db, dec_t, _ = x_sample.shape
    assert dec_t == 1
    n_even, n_odd = norm_even.shape[0], norm_odd.shape[0]
    depth = n_even + n_odd
    n_pages = page_table.shape[1]
    past_len = n_pages * PAGE_SIZE
    aw = v_norm.shape[1]
    bw = pool_scale.shape[1]
    groups = C_KV_HEADS
    kvw = groups * HEAD_DIM
    cw = w_out_odd.shape[1]
    rep = cw // kvw
    heads = cw // HEAD_DIM
    n_phys = cache_cmp_kv.shape[1]
    n_buf = state_win_kv.shape[2]
    assert past_len % CMP_BLOCK == 0 and seq % CMP_BLOCK == 0
    nb_p = seq // CMP_BLOCK
    nb_past = past_len // CMP_BLOCK
    n_past_sel = min(N_SEL, nb_past + 1) - 1
    assert nb_past >= 2 and n_past_sel >= 2

    xp = x_prompt.reshape(bsz * seq, d_model)
    xs = x_sample.reshape(db, d_model)

    cos_p, sin_p = _rope_tables(jnp.arange(seq, dtype=I32))
    cos_s, sin_s = _rope_tables(jnp.full((db,), past_len, dtype=I32))
    tables = _mask_tables(seq, LANES, WINDOW + LANES, rep)

    cache_cmp5 = cache_cmp_kv.reshape(n_odd, n_phys, PAGE_SIZE, 2 * groups, HEAD_DIM)
    cache_sel5 = cache_sel_kv.reshape(n_odd, n_phys, PAGE_SIZE, 2 * groups, HEAD_DIM)
    win_state5 = state_win_kv.reshape(n_odd, db, n_buf, 2 * groups, HEAD_DIM)

    tn_in = 2560

    cmp_p, cmp_s, sel_p, sel_s, win_p, win_s = [], [], [], [], [], []
    pool_p, pool_s, gv_s = [], [], []
    for layer in range(depth):
        li = layer // 2
        if layer % 2 == 0:
            w_in = w_in_even[li].astype(BF16)
            w_out = w_out_even[li].astype(BF16)
            g_in = norm_even[li][None, :]
            vg = v_norm[li][None, :]
            ps = pool_scale[li][None, :]
            wp = w_pool[li].astype(BF16)
            tn = tn_in if w_in.shape[1] % tn_in == 0 else w_in.shape[1]
            proj = _norm_matmul(xp, g_in, w_in, tn=tn)
            act, pool16 = _even_mix(proj.reshape(bsz, seq, -1), vg, w_spatial[li], b_spatial[li].T, wp, ps)
            xp = _matmul_residual(act.reshape(bsz * seq, aw + bw), w_out, xp)
            pool_p.append(pool16[:, 16 - POOL_PAD:])
            proj_s = _norm_matmul(xs, g_in, w_in, tn=tn)
            hd = aw // A_HEADS
            w00 = jnp.repeat(w_spatial[li][:, 0, 0], hd)[None, :]
            b0 = jnp.repeat(b_spatial[li][:, 0], hd)[None, :]
            act_s, vn_s, new_state = _even_mix_sample(
                proj_s, vg, w00, b0, wp, ps, jnp.swapaxes(state_pool[li], 0, 1), past_len)
            xs = _matmul_residual(act_s, w_out, xs)
            pool_s.append(jnp.swapaxes(new_state, 0, 1))
            gv_s.append(vn_s.reshape(db, 1, aw))
        else:
            w_qkv, w_zg = _odd_weight_layout(w_in_odd[li], cw, kvw, rep)
            w_out = w_out_odd[li].astype(BF16)
            g_in = norm_odd[li][None, :]
            qg = q_norm[li][None, :]
            kg = k_norm[li]
            tn = tn_in if w_qkv.shape[1] % tn_in == 0 else w_qkv.shape[1]
            proj = _norm_matmul(xp, g_in, w_qkv, tn=tn)
            proj_zg = _norm_matmul(xp, g_in, w_zg, tn=w_zg.shape[1])
            q_b, kv_cmp, kv_sel, kv_win, kt_sel, v_sel, kt_win, v_win, means = _odd_post(
                proj, cos_p, sin_p, qg, kg, cw=cw, kvw=kvw, for_prompt=True, win_keep=min(WINDOW, seq))
            r3 = lambda a: a.reshape(bsz, seq, a.shape[-1])
            act, means_s = _nsa_prompt(r3(q_b), kt_sel, r3(v_sel), kt_win, r3(v_win),
                                       means.reshape(bsz, nb_p, 2 * kvw), r3(proj_zg), tables,
                                       cache_cmp5, page_table, li, cw=cw, z_off=0, g_off=cw)
            xp = _matmul_residual(act.reshape(bsz * seq, cw), w_out, xp)
            kv6 = lambda a, n: a.reshape(n, -1, 2, groups, HEAD_DIM)
            cmp_p.append(kv6(kv_cmp, bsz))
            sel_p.append(kv6(kv_sel, bsz))
            win_p.append(kv6(kv_win, bsz))
            proj_s = _norm_matmul(xs, g_in, w_qkv, tn=tn)
            proj_zg_s = _norm_matmul(xs, g_in, w_zg, tn=w_zg.shape[1])
            q_s, kvc_s, kvs_s, kvw_s = _odd_post(
                proj_s, cos_s, sin_s, qg, kg, cw=cw, kvw=kvw, for_prompt=False)
            q3s = q_s.reshape(db, heads, HEAD_DIM)
            o_c, idx = _nsa_sample_select(q3s, means_s, groups=groups, rep=rep, q_pos=past_len,
                                          n_past_sel=n_past_sel)
            act_s, win_new = _nsa_sample_attend(
                page_table, idx[:, :, :n_past_sel], q3s, cache_sel5, li,
                kvs_s.reshape(db, 1, 2 * kvw), kvw_s.reshape(db, 1, 2 * kvw), o_c, win_state5,
                proj_zg_s[:, cw:].reshape(db, 1, groups * LANES),
                proj_zg_s[:, :cw].reshape(db, 1, cw),
                groups=groups, rep=rep, n_gather=n_past_sel)
            xs = _matmul_residual(act_s.reshape(db, cw), w_out, xs)
            cmp_s.append(kv6(kvc_s, db))
            sel_s.append(kv6(kvs_s, db))
            win_s.append(win_new.reshape(db, n_buf, 2, groups, HEAD_DIM))
    return (xp.reshape(bsz, seq, d_model), xs.reshape(db, 1, d_model),
            jnp.stack(cmp_p), jnp.stack(cmp_s), jnp.stack(sel_p), jnp.stack(sel_s),
            jnp.stack(win_p), jnp.stack(win_s), jnp.stack(pool_p), jnp.stack(pool_s), jnp.stack(gv_s))
```

```python
import functools

import jax
import jax.numpy as jnp
import numpy as np
from jax import lax
from jax.experimental import pallas as pl
from jax.experimental.pallas import tpu as pltpu

F32 = jnp.float32
BF16 = jnp.bfloat16
I32 = jnp.int32

EPS = 1e-6
PAGE_SIZE = 128
A_HEADS = 8
GMLP_CHUNK = 128
POOL_WINDOWS = (2, 4, 8, 16)
POOL_PAD = max(POOL_WINDOWS) - 1
HEAD_DIM = 128
C_KV_HEADS = 4
CMP_BLOCK = 64
N_SEL = 16
WINDOW = 512
SEL_FORCE = 1e4
NEG_INF = -1e30
ROPE_THETA = 500000.0
ROT_DIM = HEAD_DIM // 4
LOG2E = 1.4426950408889634
Q_SCALE = HEAD_DIM ** -0.5 * LOG2E

LANES = 128
SUBLANES = 8
MIB = 1024 * 1024


def _cparams(vmem_mib, semantics=None):
    return pltpu.CompilerParams(vmem_limit_bytes=int(vmem_mib * MIB), dimension_semantics=semantics)


def _silu(x):
    return x * (1.0 / (1.0 + jnp.exp(-x)))


def _sigmoid(x):
    return 1.0 / (1.0 + jnp.exp(-x))


def _dot_nt(a, b):
    return lax.dot_general(a, b, (((1,), (1,)), ((), ())), preferred_element_type=F32)


def _dot(a, b):
    return jnp.dot(a, b, preferred_element_type=F32)


def _norm_mm_kernel(x_ref, g_ref, w_ref, o_ref):
    x = x_ref[...]
    ms = jnp.mean(x * x, axis=-1, keepdims=True)
    h = (x * lax.rsqrt(ms + EPS) * g_ref[...]).astype(BF16)
    o_ref[...] = jnp.dot(h, w_ref[...], preferred_element_type=F32)


def _mm_res_kernel(a_ref, w_ref, r_ref, o_ref):
    o_ref[...] = r_ref[...] + jnp.dot(a_ref[...], w_ref[...], preferred_element_type=F32)


def _row_tile(m, want):
    return want if m % want == 0 else m


def _norm_matmul(x, g, w, layer, n, *, tn, tm=512):
    m, k = x.shape
    tm = _row_tile(m, tm)
    assert n % tn == 0 and n <= w.shape[2]
    vmem = 2 * (tm * k * 4 + k * tn * 2 + tm * tn * 4) / MIB + 8
    return pl.pallas_call(
        _norm_mm_kernel,
        grid=(n // tn, m // tm),
        in_specs=[
            pl.BlockSpec((tm, k), lambda j, i: (i, 0)),
            pl.BlockSpec((1, k), lambda j, i: (0, 0)),
            pl.BlockSpec((None, k, tn), lambda j, i: (layer, 0, j)),
        ],
        out_specs=pl.BlockSpec((tm, tn), lambda j, i: (i, j)),
        out_shape=jax.ShapeDtypeStruct((m, n), F32),
        compiler_params=_cparams(vmem),
        name="norm_matmul",
    )(x, g, w)


def _matmul_residual(a, w, layer, res, *, tm=512):
    m, k = a.shape
    n = w.shape[2]
    tm = _row_tile(m, tm)
    vmem = 2 * (tm * k * 2 + k * n * 2 + 2 * tm * n * 4) / MIB + 8
    return pl.pallas_call(
        _mm_res_kernel,
        grid=(m // tm,),
        in_specs=[
            pl.BlockSpec((tm, k), lambda i: (i, 0)),
            pl.BlockSpec((None, k, n), lambda i: (layer, 0, 0)),
            pl.BlockSpec((tm, n), lambda i: (i, 0)),
        ],
        out_specs=pl.BlockSpec((tm, n), lambda i: (i, 0)),
        out_shape=jax.ShapeDtypeStruct((m, n), F32),
        compiler_params=_cparams(vmem),
        name="matmul_residual",
    )(a, w, res)


def _even_mix_kernel(proj_ref, vg_ref, ws_ref, bst_ref, wp_ref, ps_ref, act_ref, pool_ref, ext_ref,
                     *, tm, aw, bw):
    c = pl.program_id(1)
    n_c = pl.num_programs(1)
    hd = aw // A_HEADS
    pg = bw // len(POOL_WINDOWS)

    cl = GMLP_CHUNK
    row = lax.broadcasted_iota(I32, (cl, cl), 0)
    col = lax.broadcasted_iota(I32, (cl, cl), 1)
    causal = row >= col

    for h in range(A_HEADS):
        sl = slice(h * hd, (h + 1) * hd)
        w = jnp.where(causal, ws_ref[h], 0.0).astype(BF16)
        for c0 in range(0, tm, cl):
            rs = slice(c0, c0 + cl)
            u = proj_ref[0, rs, sl]
            v = proj_ref[0, rs, aw + h * hd:aw + (h + 1) * hd]
            za = proj_ref[0, rs, 2 * aw + h * hd:2 * aw + (h + 1) * hd]
            ms = jnp.mean(v * v, axis=-1, keepdims=True)
            vn = v * lax.rsqrt(ms + EPS) * vg_ref[:, sl]
            s = jnp.dot(w, vn.astype(BF16), preferred_element_type=F32) + bst_ref[:, h:h + 1]
            act_ref[0, rs, sl] = ((u * s) * _silu(za)).astype(BF16)

    p_off = 3 * aw
    zb_off = 3 * aw + bw

    @pl.when(c == 0)
    def _():
        ext_ref[0:16, :] = jnp.zeros((16, bw), F32)

    ext_ref[16:16 + tm, :] = proj_ref[0, :, p_off:p_off + bw]
    pos = c * tm + lax.broadcasted_iota(I32, (tm, 1), 0)
    for g, wnd in enumerate(POOL_WINDOWS):
        sl = slice(g * pg, (g + 1) * pg)
        cur = ext_ref[16:16 + tm, sl]
        acc = cur
        for k in range(1, wnd):
            acc = acc + ext_ref[16 - k:16 - k + tm, sl]
        cnt = jnp.minimum(pos + 1, wnd).astype(F32)
        d = acc / cnt - cur
        y = jnp.dot(d.astype(BF16), wp_ref[g], preferred_element_type=F32)
        zb = proj_ref[0, :, zb_off + g * pg:zb_off + (g + 1) * pg]
        act_ref[0, :, aw + g * pg:aw + (g + 1) * pg] = ((y * ps_ref[:, sl]) * _silu(zb)).astype(BF16)

    tail = ext_ref[tm:tm + 16, :]
    ext_ref[0:16, :] = tail

    @pl.when(c == n_c - 1)
    def _():
        pool_ref[0] = tail


def _even_mix(proj3, v_gain, w_s, b_st, w_pool, pool_scale):
    b, s, _ = proj3.shape
    aw = v_gain.shape[1]
    bw = pool_scale.shape[1]
    tm = 2 * GMLP_CHUNK if s % (2 * GMLP_CHUNK) == 0 else GMLP_CHUNK
    assert s % tm == 0 and s >= 16
    n_in = 3 * aw + 2 * bw
    kern = functools.partial(_even_mix_kernel, tm=tm, aw=aw, bw=bw)
    return pl.pallas_call(
        kern,
        grid=(b, s // tm),
        in_specs=[
            pl.BlockSpec((1, tm, n_in), lambda i, c: (i, c, 0)),
            pl.BlockSpec((1, aw), lambda i, c: (0, 0)),
            pl.BlockSpec(w_s.shape, lambda i, c: (0, 0, 0)),
            pl.BlockSpec(b_st.shape, lambda i, c: (0, 0)),
            pl.BlockSpec(w_pool.shape, lambda i, c: (0, 0, 0)),
            pl.BlockSpec((1, bw), lambda i, c: (0, 0)),
        ],
        out_specs=[
            pl.BlockSpec((1, tm, aw + bw), lambda i, c: (i, c, 0)),
            pl.BlockSpec((1, 16, bw), lambda i, c: (i, 0, 0)),
        ],
        out_shape=[
            jax.ShapeDtypeStruct((b, s, aw + bw), BF16),
            jax.ShapeDtypeStruct((b, 16, bw), F32),
        ],
        scratch_shapes=[pltpu.VMEM((16 + tm, bw), F32)],
        compiler_params=_cparams(32, ("arbitrary", "arbitrary")),
        name="even_mix",
    )(proj3, v_gain, w_s, b_st, w_pool, pool_scale)


def _even_mix_sample_kernel(proj_ref, vg_ref, w00_ref, b0_ref, wp_ref, ps_ref, st_ref,
                            act_ref, vn_ref, newst_ref, *, aw, bw, pos):
    hd = aw // A_HEADS
    pg = bw // len(POOL_WINDOWS)
    for h in range(A_HEADS):
        sl = slice(h * hd, (h + 1) * hd)
        u = proj_ref[:, sl]
        v = proj_ref[:, aw + h * hd:aw + (h + 1) * hd]
        za = proj_ref[:, 2 * aw + h * hd:2 * aw + (h + 1) * hd]
        ms = jnp.mean(v * v, axis=-1, keepdims=True)
        vn = v * lax.rsqrt(ms + EPS) * vg_ref[:, sl]
        vn_ref[:, sl] = vn
        s = w00_ref[:, sl] * vn + b0_ref[:, sl]
        act_ref[:, sl] = ((u * s) * _silu(za)).astype(BF16)

    p = proj_ref[:, 3 * aw:3 * aw + bw]
    for g, wnd in enumerate(POOL_WINDOWS):
        sl = slice(g * pg, (g + 1) * pg)
        cur = p[:, sl]
        acc = cur
        for k in range(1, wnd):
            acc = acc + st_ref[POOL_PAD - k, :, sl]
        cnt = float(min(pos + 1, wnd))
        d = acc / cnt - cur
        y = jnp.dot(d.astype(BF16), wp_ref[g], preferred_element_type=F32)
        zb = proj_ref[:, 3 * aw + bw + g * pg:3 * aw + bw + (g + 1) * pg]
        act_ref[:, aw + g * pg:aw + (g + 1) * pg] = ((y * ps_ref[:, sl]) * _silu(zb)).astype(BF16)

    for k in range(POOL_PAD - 1):
        newst_ref[k] = st_ref[k + 1]
    newst_ref[POOL_PAD - 1] = p


def _even_mix_sample(proj, v_gain, w00, b0, w_pool, pool_scale, state_t, pos):
    db = proj.shape[0]
    aw = v_gain.shape[1]
    bw = pool_scale.shape[1]
    kern = functools.partial(_even_mix_sample_kernel, aw=aw, bw=bw, pos=pos)
    return pl.pallas_call(
        kern,
        out_shape=[
            jax.ShapeDtypeStruct((db, aw + bw), BF16),
            jax.ShapeDtypeStruct((db, aw), F32),
            jax.ShapeDtypeStruct((POOL_PAD, db, bw), F32),
        ],
        name="even_mix_sample",
    )(proj, v_gain, w00, b0, w_pool, pool_scale, state_t)


def _odd_post_kernel(proj_ref, cos_ref, sin_ref, qg_ref, kg_ref, *out_refs, tm, cw, kvw, for_prompt):
    q_out, cmp_out, sel_out, win_out = out_refs[:4]
    if for_prompt:
        selkt_out, selv_out, winkt_out, winv_out, means_out = out_refs[4:]
        attn_outs = (None, (selkt_out, selv_out), (winkt_out, winv_out))
    else:
        attn_outs = (None, None, None)
    cosf = cos_ref[...]
    sinf = sin_ref[...]
    half = ROT_DIM // 2
    first = lax.broadcasted_iota(I32, (tm, HEAD_DIM), 1) < half

    def norm_rope(x, gain):
        ms = jnp.mean(x * x, axis=-1, keepdims=True)
        y = x * lax.rsqrt(ms + EPS) * gain
        rot = jnp.where(first, pltpu.roll(y, HEAD_DIM - half, 1), pltpu.roll(y, half, 1))
        return y * cosf + rot * sinf

    qg = qg_ref[...]
    for h in range(cw // HEAD_DIM):
        sl = slice(h * HEAD_DIM, (h + 1) * HEAD_DIM)
        q_out[:, sl] = (norm_rope(proj_ref[:, sl], qg) * Q_SCALE).astype(BF16)

    for br, (o32, attn) in enumerate(zip((cmp_out, sel_out, win_out), attn_outs)):
        k_off = cw + br * 2 * kvw
        v_off = k_off + kvw
        kg = kg_ref[br:br + 1, :]
        pieces = []
        for g in range(kvw // HEAD_DIM):
            sl = slice(g * HEAD_DIM, (g + 1) * HEAD_DIM)
            kk = norm_rope(proj_ref[:, k_off + g * HEAD_DIM:k_off + (g + 1) * HEAD_DIM], kg)
            pieces.append(kk)
            if attn is not None:
                attn[0][sl, :] = kk.T.astype(BF16)
        vv = proj_ref[:, v_off:v_off + kvw]
        if attn is not None:
            attn[1][...] = vv.astype(BF16)
        row = jnp.concatenate(pieces + [vv], axis=1)
        o32[...] = row.reshape(o32.shape)
        if br == 0 and for_prompt:
            for j in range(tm // CMP_BLOCK):
                blk = row[j * CMP_BLOCK:(j + 1) * CMP_BLOCK, :]
                means_out[0, j:j + 1, :] = jnp.mean(blk, axis=0, keepdims=True)


def _odd_post(proj, cos_t, sin_t, q_gain, k_gain, *, cw, kvw, for_prompt, win_keep=0, tm=512):
    m = proj.shape[0]
    tm = _row_tile(m, tm)
    n_pos_tiles = cos_t.shape[0] // tm
    n_used = cw + 6 * kvw
    kern = functools.partial(_odd_post_kernel, tm=tm, cw=cw, kvw=kvw, for_prompt=for_prompt)
    row_spec = lambda w: pl.BlockSpec((tm, w), lambda i: (i, 0))
    n_rows = 2 * kvw // HEAD_DIM
    if for_prompt:
        assert win_keep % tm == 0 and cos_t.shape[0] >= win_keep
        keep_tiles = win_keep // tm
        skip_tiles = n_pos_tiles - keep_tiles
        kv_shape = jax.ShapeDtypeStruct((m, n_rows, HEAD_DIM), F32)
        kv_spec = pl.BlockSpec((tm, n_rows, HEAD_DIM), lambda i: (i, 0, 0))
        win_shape = jax.ShapeDtypeStruct((m // n_pos_tiles * keep_tiles, n_rows, HEAD_DIM), F32)
        win_spec = pl.BlockSpec(
            (tm, n_rows, HEAD_DIM),
            lambda i: (i // n_pos_tiles * keep_tiles + jnp.maximum(i % n_pos_tiles - skip_tiles, 0), 0, 0))
    else:
        kv_shape = win_shape = jax.ShapeDtypeStruct((m, 2 * kvw), F32)
        kv_spec = win_spec = row_spec(2 * kvw)
    out_shape = [jax.ShapeDtypeStruct((m, cw), BF16), kv_shape, kv_shape, win_shape]
    out_specs = [row_spec(cw), kv_spec, kv_spec, win_spec]
    if for_prompt:
        assert tm % CMP_BLOCK == 0 and tm % LANES == 0
        for _ in range(2):
            out_shape += [jax.ShapeDtypeStruct((kvw, m), BF16), jax.ShapeDtypeStruct((m, kvw), BF16)]
            out_specs += [pl.BlockSpec((kvw, tm), lambda i: (0, i)), row_spec(kvw)]
        out_shape.append(jax.ShapeDtypeStruct((m // tm, tm // CMP_BLOCK, 2 * kvw), F32))
        out_specs.append(pl.BlockSpec((1, tm // CMP_BLOCK, 2 * kvw), lambda i: (i, 0, 0)))
    return pl.pallas_call(
        kern,
        grid=(m // tm,),
        in_specs=[
            pl.BlockSpec((tm, n_used), lambda i: (i, 0)),
            pl.BlockSpec((tm, HEAD_DIM), lambda i: (i % n_pos_tiles, 0)),
            pl.BlockSpec((tm, HEAD_DIM), lambda i: (i % n_pos_tiles, 0)),
            pl.BlockSpec((1, HEAD_DIM), lambda i: (0, 0)),
            pl.BlockSpec((3, HEAD_DIM), lambda i: (0, 0)),
        ],
        out_specs=out_specs,
        out_shape=out_shape,
        compiler_params=_cparams(56),
        name="odd_post",
    )(proj, cos_t, sin_t, q_gain, k_gain)


def _nsa_sub_tile(i, h, q_ref, ks_ref, vs_ref, kw_ref, vw_ref, kc, vc, band_ref, tri_ref, eye_ref,
                  *, tq, nb, rep, lw):
    rows = rep * tq
    r0 = h * tq
    q = jnp.concatenate([q_ref[0, r0:r0 + tq, r * HEAD_DIM:(r + 1) * HEAD_DIM] for r in range(rep)],
                        axis=0)
    t_col = i * tq + lax.broadcasted_iota(I32, (tq, 1), 0)
    t_row = i * tq + lax.broadcasted_iota(I32, (1, tq), 1)

    blk_row = lax.broadcasted_iota(I32, (1, nb), 1)
    s_c = _dot_nt(q, kc).reshape(rep, tq, nb)
    ok_c = ((blk_row + 1) * CMP_BLOCK - 1) <= t_col
    s_c = s_c + jnp.where(ok_c, 0.0, NEG_INF)[None]
    m_c = jnp.max(s_c, axis=-1, keepdims=True)
    e_c = jnp.exp2(s_c - m_c)
    p_c = e_c / jnp.sum(e_c, axis=-1, keepdims=True) * jnp.where(ok_c, 1.0, 0.0)[None]
    o_c = jnp.dot(p_c.reshape(rows, nb).astype(BF16), vc, preferred_element_type=F32)

    blk_col = lax.broadcasted_iota(I32, (nb, 1), 0)
    s_t = _dot_nt(kc, q)
    ok_t = ((blk_col + 1) * CMP_BLOCK - 1) <= t_row
    bias_t = jnp.where(ok_t, 0.0, NEG_INF)
    okf_t = jnp.where(ok_t, 1.0, 0.0)
    imp = jnp.zeros((nb, tq), F32)
    for r in range(rep):
        s_r = s_t[:, r * tq:(r + 1) * tq] + bias_t
        e_r = jnp.exp2(s_r - jnp.max(s_r, axis=0, keepdims=True))
        imp = imp + e_r / jnp.sum(e_r, axis=0, keepdims=True) * okf_t
    cur = t_row // CMP_BLOCK
    forced = (blk_col == 0) | (blk_col == cur) | (blk_col == cur - 1)
    imp = jnp.where(blk_col <= cur, jnp.where(forced, SEL_FORCE, imp), -1.0)

    blk8 = lax.broadcasted_iota(I32, (SUBLANES, 1), 0)
    cnt = jnp.zeros((nb, tq), F32)
    for j in range(nb):
        rj = imp[j:j + 1, :]
        parts = []
        for v0 in range(0, nb, SUBLANES):
            sub = imp[v0:v0 + SUBLANES, :]
            if v0 > j:
                beats = rj >= sub
            elif v0 + SUBLANES - 1 <= j:
                beats = rj > sub
            else:
                beats = (rj > sub) | ((rj == sub) & (blk8 > j - v0))
            parts.append(jnp.where(beats, 1.0, 0.0))
        cnt = cnt + jnp.concatenate(parts, axis=0)
    keep = (cnt < min(N_SEL, nb)) & (imp >= 0.0) & (blk_col < i * (tq // CMP_BLOCK))
    drop_t = jnp.where(keep, 0.0, 1.0)
    if LANES > nb:
        drop_t = jnp.concatenate([drop_t, jnp.zeros((LANES - nb, tq), F32)], axis=0)
    drop = drop_t.T.astype(BF16)
    q_drop = jnp.concatenate([q, jnp.concatenate([drop] * rep, axis=0)], axis=1)

    q_eye = jnp.concatenate([q, eye_ref[...]], axis=1)

    w0 = pl.multiple_of(jnp.maximum(i * tq + tq - lw, 0), tq)
    band = band_ref[jnp.minimum(i, band_ref.shape[0] - 1)]
    k_w = jnp.concatenate([kw_ref[:, pl.ds(w0, lw)], band], axis=0)
    v_w = jnp.concatenate([vw_ref[0, pl.ds(w0, lw), :], jnp.ones((lw, HEAD_DIM), BF16)], axis=1)
    s_w = _dot(q_eye, k_w)
    p_w = jnp.exp2(s_w - jnp.max(s_w, axis=-1, keepdims=True))
    acc_w = _dot(p_w.astype(BF16), v_w)
    o_w = acc_w[:, :HEAD_DIM] / acc_w[:, HEAD_DIM:]

    d0 = pl.multiple_of(i * tq, tq)
    k_d = jnp.concatenate([ks_ref[:, pl.ds(d0, tq)], tri_ref[...]], axis=0)
    v_d = jnp.concatenate([vs_ref[0, pl.ds(d0, tq), :], jnp.ones((tq, HEAD_DIM), BF16)], axis=1)
    s_d = _dot(q_eye, k_d)
    m_d = jnp.max(s_d, axis=-1, keepdims=True)
    acc_d = _dot(jnp.exp2(s_d - m_d).astype(BF16), v_d)
    return q_drop, o_c, o_w, m_d, acc_d


def _nsa_prompt_kernel(pt_ref, q_ref, ks_ref, vs_ref, kw_ref, vw_ref, kc_ref, vc_ref, g_ref, z_ref,
                       eb_ref, band_ref, tri_ref, eye_ref, *rest, tq, nh, tk, seq, nb, rep, lw, pps):
    if pps:
        o_ref, pm_ref, s0_ref, s1_ref = rest[pps:]
        _store_page_means(rest[:pps], pm_ref)
    else:
        o_ref, s0_ref, s1_ref = rest
    step = pl.program_id(2)
    kc = kc_ref[0].astype(BF16)
    vc = vc_ref[0].astype(BF16)
    subs = [_nsa_sub_tile(step * nh + h, h, q_ref, ks_ref, vs_ref, kw_ref, vw_ref, kc, vc, band_ref,
                          tri_ref, eye_ref, tq=tq, nb=nb, rep=rep, lw=lw) for h in range(nh)]

    q_drop = jnp.concatenate([s[0] for s in subs], axis=0)
    m_0 = jnp.concatenate([s[3] for s in subs], axis=0)
    acc_0 = jnp.concatenate([s[4] for s in subs], axis=0)

    partial, gate_s, z_act = [], [], []
    for h in range(nh):
        ts = slice(h * tq, (h + 1) * tq)
        gs = _sigmoid(g_ref[0, ts, :])
        o_c, o_w = subs[h][1], subs[h][2]
        for r in range(rep):
            rs = slice(r * tq, (r + 1) * tq)
            partial.append(gs[:, 3 * r:3 * r + 1] * o_c[rs] + gs[:, 3 * r + 2:3 * r + 3] * o_w[rs])
            gate_s.append(gs[:, 3 * r + 1:3 * r + 2])
            z_act.append(_silu(z_ref[0, ts, r * HEAD_DIM:(r + 1) * HEAD_DIM]))

    ones_k = jnp.ones((tk, HEAD_DIM), BF16)
    n_tiles = seq // tk

    def tile_scores(t):
        k0 = pl.multiple_of(t * tk, tk)
        k = jnp.concatenate([ks_ref[:, pl.ds(k0, tk)], eb_ref[:, pl.ds(k0, tk)]], axis=0)
        return _dot(q_drop, k)

    def accumulate(carry, s, t):
        m_p, acc = carry
        k0 = pl.multiple_of(t * tk, tk)
        v = jnp.concatenate([vs_ref[0, pl.ds(k0, tk), :], ones_k], axis=1)
        m_n = jnp.maximum(m_p, jnp.max(s, axis=-1, keepdims=True))
        pv = _dot(jnp.exp2(s - m_n).astype(BF16), v)
        return m_n, jnp.exp2(m_p - m_n) * acc + pv

    def sel_body(j, carry):
        t1 = 2 * j + 1
        s1_ref[...] = tile_scores(t1)
        carry = accumulate(carry, s0_ref[...], 2 * j)
        s0_ref[...] = tile_scores(jnp.minimum(t1 + 1, n_tiles - 1))
        return accumulate(carry, s1_ref[...], t1)

    n_past = ((step * nh + nh - 1) * tq + tk - 1) // tk
    s0_ref[...] = tile_scores(0)
    _, acc_s = lax.fori_loop(0, (n_past + 1) // 2, sel_body, (m_0, acc_0))
    o_s = acc_s[:, :HEAD_DIM] / acc_s[:, HEAD_DIM:]

    for h in range(nh):
        for r in range(rep):
            n = h * rep + r
            o = partial[n] + gate_s[n] * o_s[n * tq:(n + 1) * tq]
            o_ref[0, h * tq:(h + 1) * tq, r * HEAD_DIM:(r + 1) * HEAD_DIM] = (o * z_act[n]).astype(BF16)


def _mask_tables(seq, tq, lw, rep):
    neg = lambda ok: jnp.where(ok, 0.0, NEG_INF).astype(BF16)
    row = jnp.arange(LANES, dtype=I32)[:, None]
    key = jnp.arange(seq, dtype=I32)[None, :]
    block_rows = neg(key // CMP_BLOCK != row)
    t = jnp.arange(tq, dtype=I32)[None, :, None]
    c = jnp.arange(lw, dtype=I32)[None, None, :]
    off = jnp.arange(WINDOW // tq + 1, dtype=I32)[:, None, None] * tq
    rel = off + t - c
    band = neg((rel >= 0) & (rel < WINDOW))
    tri = neg(jnp.arange(tq, dtype=I32)[None, :] <= jnp.arange(tq, dtype=I32)[:, None])
    eye = jnp.tile(jnp.eye(tq, dtype=BF16), (rep, 1))
    return block_rows, band, tri, eye


def _page_job_split(n_steps, page_table):
    db, n_pages = page_table.shape
    total = db * n_pages
    if total % n_steps:
        return 0
    pps = total // n_steps
    return pps if n_pages % pps == 0 else 0


def _nsa_prompt(q3, kt_sel, v_sel, kt_win, v_win, means3, proj3, tables, cache5, page_table, layer, *,
                cw, z_off, g_off, tq=LANES, nh=2, tk=512):
    b, s, _ = q3.shape
    kvw = v_sel.shape[2]
    groups = kvw // HEAD_DIM
    rep = cw // kvw
    nb = means3.shape[1]
    gw = rep * HEAD_DIM
    block_rows, band, tri, eye = tables
    tk = min(tk, s)
    lw = band.shape[2]
    ts = nh * tq
    assert tq == LANES and nb <= LANES and nb % SUBLANES == 0 and lw == WINDOW + tq and s >= lw
    assert s % ts == 0 and s % (2 * tk) == 0 and tk % tq == 0 and tq % CMP_BLOCK == 0
    assert nb * CMP_BLOCK == s
    assert z_off % gw == 0 and g_off % LANES == 0
    n_i = s // ts
    pps = _page_job_split(b * groups * n_i, page_table)
    kern = functools.partial(_nsa_prompt_kernel, tq=tq, nh=nh, tk=tk, seq=s, nb=nb, rep=rep, lw=lw,
                             pps=pps)
    db, n_pages = page_table.shape
    n_rows, bpp = cache5.shape[3], PAGE_SIZE // CMP_BLOCK
    spp = n_pages // pps if pps else 0
    assert b * groups * n_i * pps in (0, db * n_pages)
    flat = lambda bi, g, i: (bi * groups + g) * n_i + i

    def page_spec(r):
        def index_map(bi, g, i, pt):
            f = flat(bi, g, i)
            return (layer, pt[f // spp, (f % spp) * pps + r], 0, 0, 0)
        return pl.BlockSpec((None, None, PAGE_SIZE, n_rows, HEAD_DIM), index_map)

    kt_spec = pl.BlockSpec((HEAD_DIM, s), lambda bi, g, i, pt: (g, bi))
    v_spec = pl.BlockSpec((1, s, HEAD_DIM), lambda bi, g, i, pt: (bi, 0, g))
    mean_spec = lambda off: pl.BlockSpec((1, nb, HEAD_DIM), lambda bi, g, i, pt: (bi, 0, off + g))
    grid_spec = pltpu.PrefetchScalarGridSpec(
        num_scalar_prefetch=1,
        grid=(b, groups, n_i),
        in_specs=[
            pl.BlockSpec((1, ts, gw), lambda bi, g, i, pt: (bi, i, g)),
            kt_spec, v_spec, kt_spec, v_spec,
            mean_spec(0), mean_spec(groups),
            pl.BlockSpec((1, ts, LANES), lambda bi, g, i, pt: (bi, i, g_off // LANES + g)),
            pl.BlockSpec((1, ts, gw), lambda bi, g, i, pt: (bi, i, z_off // gw + g)),
            pl.BlockSpec(block_rows.shape, lambda bi, g, i, pt: (0, 0)),
            pl.BlockSpec(band.shape, lambda bi, g, i, pt: (0, 0, 0)),
            pl.BlockSpec(tri.shape, lambda bi, g, i, pt: (0, 0)),
            pl.BlockSpec(eye.shape, lambda bi, g, i, pt: (0, 0)),
        ] + [page_spec(r) for r in range(pps)],
        out_specs=[pl.BlockSpec((1, ts, gw), lambda bi, g, i, pt: (bi, i, g))] + ([
            pl.BlockSpec((1, n_rows, pps * bpp, HEAD_DIM),
                         lambda bi, g, i, pt: (flat(bi, g, i) // spp, 0, flat(bi, g, i) % spp, 0)),
        ] if pps else []),
        scratch_shapes=[pltpu.VMEM((nh * rep * tq, tk), F32), pltpu.VMEM((nh * rep * tq, tk), F32)],
    )
    outs = pl.pallas_call(
        kern,
        grid_spec=grid_spec,
        out_shape=[jax.ShapeDtypeStruct((b, s, cw), BF16)] + ([
            jax.ShapeDtypeStruct((db, n_rows, n_pages * bpp, HEAD_DIM), F32)] if pps else []),
        compiler_params=_cparams(56),
        name="nsa_prompt",
    )(page_table, q3, kt_sel, v_sel, kt_win, v_win, means3, means3, proj3, proj3, block_rows, band, tri,
      eye, *([cache5] * pps))
    return outs if pps else (outs[0], _page_means(cache5, page_table, layer))


def _store_page_means(page_refs, out_ref):
    bpp = PAGE_SIZE // CMP_BLOCK
    n_rows = out_ref.shape[1]
    for r, page_ref in enumerate(page_refs):
        for h in range(bpp):
            mean = jnp.mean(page_ref[h * CMP_BLOCK:(h + 1) * CMP_BLOCK], axis=0)
            n = r * bpp + h
            for j in range(n_rows):
                out_ref[0, j, n:n + 1, :] = mean[j:j + 1, :]


def _page_means_kernel(pt_ref, *refs, pps):
    _store_page_means(refs[:pps], refs[pps])


def _page_means(cache5, page_table, layer, *, pps=8):
    db, n_pages = page_table.shape
    n_rows, d = cache5.shape[3], cache5.shape[4]
    pps = pps if n_pages % pps == 0 else 1
    bpp = PAGE_SIZE // CMP_BLOCK

    def page_spec(r):
        return pl.BlockSpec((None, None, PAGE_SIZE, n_rows, d),
                            lambda bi, j, pt: (layer, pt[bi, j * pps + r], 0, 0, 0))

    grid_spec = pltpu.PrefetchScalarGridSpec(
        num_scalar_prefetch=1,
        grid=(db, n_pages // pps),
        in_specs=[page_spec(r) for r in range(pps)],
        out_specs=pl.BlockSpec((1, n_rows, pps * bpp, d), lambda bi, j, pt: (bi, 0, j, 0)),
    )
    return pl.pallas_call(
        functools.partial(_page_means_kernel, pps=pps),
        grid_spec=grid_spec,
        out_shape=jax.ShapeDtypeStruct((db, n_rows, n_pages * bpp, d), F32),
        compiler_params=_cparams(24),
        name="page_means",
    )(page_table, *([cache5] * pps))


def _nsa_sample_select_kernel(q_ref, mean_ref, oc_ref, idx_ref, *, nbp, groups, rep, q_pos, n_past_sel):
    q = q_ref[0]
    heads = groups * rep
    lane = lax.broadcasted_iota(I32, (1, nbp), 1)
    ri = lax.broadcasted_iota(I32, (nbp, nbp), 0)
    ci = lax.broadcasted_iota(I32, (nbp, nbp), 1)
    diag = ri == ci
    ok = ((lane + 1) * CMP_BLOCK - 1) <= q_pos
    cur = q_pos // CMP_BLOCK
    forced = (lane == 0) | (lane == cur) | (lane == cur - 1)
    hrow = lax.broadcasted_iota(I32, (heads, 1), 0) // rep
    out_lane = lax.broadcasted_iota(I32, (1, LANES), 1)
    o_c = jnp.zeros((heads, HEAD_DIM), F32)
    for g in range(groups):
        kc = mean_ref[0, g].astype(BF16)
        vc = mean_ref[0, groups + g].astype(BF16)
        s = _dot_nt(q, kc) + jnp.where(ok, 0.0, NEG_INF)
        e = jnp.exp2(s - jnp.max(s, axis=-1, keepdims=True))
        p = e / jnp.sum(e, axis=-1, keepdims=True) * jnp.where(ok, 1.0, 0.0)
        in_g = hrow == g
        o_c = o_c + jnp.where(in_g, jnp.dot(p.astype(BF16), vc, preferred_element_type=F32), 0.0)
        imp = jnp.sum(jnp.where(in_g, p, 0.0), axis=0, keepdims=True)
        imp = jnp.where(lane <= cur, jnp.where(forced, SEL_FORCE, imp), -1.0)
        imp_b = jnp.broadcast_to(imp, (nbp, nbp))
        imp_col = jnp.sum(jnp.where(diag, imp_b, 0.0), axis=1, keepdims=True)
        beats = (imp_col > imp_b) | ((imp_col == imp_b) & (ri < ci))
        cnt = jnp.sum(jnp.where(beats, 1.0, 0.0), axis=0, keepdims=True)
        sel = jnp.where((cnt < n_past_sel) & (imp >= 0.0), 1.0, 0.0)
        sel_col = jnp.sum(jnp.where(diag, jnp.broadcast_to(sel, (nbp, nbp)), 0.0), axis=1, keepdims=True)
        before = jnp.sum(jnp.where(ri < ci, sel_col, 0.0), axis=0, keepdims=True)
        row = jnp.zeros((1, LANES), F32)
        for k in range(n_past_sel):
            hit = (sel > 0.5) & (before == float(k))
            idx_k = jnp.sum(jnp.where(hit, lane.astype(F32), 0.0), axis=1, keepdims=True)
            row = jnp.where(out_lane == k, idx_k, row)
        idx_ref[0, g:g + 1, :] = row.astype(I32)
    oc_ref[0] = o_c


def _nsa_sample_select(q3, means, *, groups, rep, q_pos, n_past_sel):
    db, heads, _ = q3.shape
    nbp = means.shape[2]
    kern = functools.partial(_nsa_sample_select_kernel, nbp=nbp, groups=groups, rep=rep, q_pos=q_pos,
                             n_past_sel=n_past_sel)
    return pl.pallas_call(
        kern,
        grid=(db,),
        in_specs=[
            pl.BlockSpec((1, heads, HEAD_DIM), lambda bi: (bi, 0, 0)),
            pl.BlockSpec((1,) + means.shape[1:], lambda bi: (bi, 0, 0, 0)),
        ],
        out_specs=[
            pl.BlockSpec((1, heads, HEAD_DIM), lambda bi: (bi, 0, 0)),
            pl.BlockSpec((1, groups, LANES), lambda bi: (bi, 0, 0)),
        ],
        out_shape=[
            jax.ShapeDtypeStruct((db, heads, HEAD_DIM), F32),
            jax.ShapeDtypeStruct((db, groups, LANES), I32),
        ],
        compiler_params=_cparams(24),
        name="nsa_sample_select",
    )(q3, means)


def _nsa_sample_attend_kernel(pt_ref, si_ref, q_ref, *refs, groups, rep, n_steps, per_step, n_buf):
    kv_refs = refs[:groups * per_step]
    (newsel_ref, newwin_ref, oc_ref, win_ref, g_ref, z_ref,
     act_ref, winout_ref, m_ref, l_ref, acc_ref) = refs[groups * per_step:]
    kvw = groups * HEAD_DIM
    heads = groups * rep
    n_rows = 2 * groups
    k_id = pl.program_id(1)
    q = q_ref[0]
    qf = q.astype(F32)
    hgrp = lax.broadcasted_iota(I32, (heads, 1), 0) // rep

    def own_key_rows(n_tok):
        lane = lax.broadcasted_iota(I32, (1, n_tok * n_rows), 1)
        return lane % n_rows == hgrp, lane // n_rows

    def by_group(fn):
        out = None
        for g in range(groups):
            val = jnp.where(hgrp == g, fn(g), 0.0)
            out = val if out is None else out + val
        return out

    def new_token_scores(row_ref):
        return by_group(lambda g: jnp.sum(
            qf * row_ref[0, :, g * HEAD_DIM:(g + 1) * HEAD_DIM], axis=-1, keepdims=True))

    def new_token_values(row_ref):
        return by_group(lambda g: jnp.broadcast_to(
            row_ref[0, :, kvw + g * HEAD_DIM:kvw + (g + 1) * HEAD_DIM], (heads, HEAD_DIM)))

    @pl.when(k_id == 0)
    def _():
        m_ref[...] = jnp.full((heads, 1), NEG_INF, F32)
        l_ref[...] = jnp.zeros((heads, 1), F32)
        acc_ref[...] = jnp.zeros((heads, HEAD_DIM), F32)

    xs = [jnp.concatenate([kv_refs[g * per_step + u][...].reshape(CMP_BLOCK * n_rows, HEAD_DIM)
                           for u in range(per_step)], axis=0).astype(BF16) for g in range(groups)]
    own, _ = own_key_rows(CMP_BLOCK * per_step)
    s = by_group(lambda g: _dot_nt(q, xs[g])) + jnp.where(own, 0.0, NEG_INF)
    m_p = m_ref[...]
    m_n = jnp.maximum(m_p, jnp.max(s, axis=-1, keepdims=True))
    alpha = jnp.exp2(m_p - m_n)
    p = jnp.exp2(s - m_n)
    pb = pltpu.roll(p, groups, 1).astype(BF16)
    pv = by_group(lambda g: jnp.dot(pb, xs[g], preferred_element_type=F32))
    m_ref[...] = m_n
    l_ref[...] = alpha * l_ref[...] + jnp.sum(p, axis=-1, keepdims=True)
    acc_ref[...] = alpha * acc_ref[...] + pv

    @pl.when(k_id == n_steps - 1)
    def _():
        s_n = new_token_scores(newsel_ref)
        m_p = m_ref[...]
        m_n = jnp.maximum(m_p, s_n)
        alpha = jnp.exp2(m_p - m_n)
        p_n = jnp.exp2(s_n - m_n)
        l_s = alpha * l_ref[...] + p_n
        o_s = (alpha * acc_ref[...] + p_n * new_token_values(newsel_ref)) / l_s

        xw = win_ref[0].reshape(n_buf * n_rows, HEAD_DIM).astype(BF16)
        own_w, tok_w = own_key_rows(n_buf)
        ok_w = own_w & ((n_buf - tok_w) < WINDOW)
        s_w = _dot_nt(q, xw) + jnp.where(ok_w, 0.0, NEG_INF)
        s_wn = new_token_scores(newwin_ref)
        m_w = jnp.maximum(jnp.max(s_w, axis=-1, keepdims=True), s_wn)
        p_w = jnp.exp2(s_w - m_w)
        p_wn = jnp.exp2(s_wn - m_w)
        l_w = jnp.sum(p_w, axis=-1, keepdims=True) + p_wn
        o_w = jnp.dot(pltpu.roll(p_w, groups, 1).astype(BF16), xw, preferred_element_type=F32)
        o_w = (o_w + p_wn * new_token_values(newwin_ref)) / l_w

        winout_ref[0, 0:n_buf - 1] = win_ref[0, 1:n_buf]
        for j in range(n_rows):
            winout_ref[0, n_buf - 1, j:j + 1, :] = newwin_ref[0, :, j * HEAD_DIM:(j + 1) * HEAD_DIM]

        graw = jnp.broadcast_to(g_ref[0], (heads, groups * LANES))
        glane = lax.broadcasted_iota(I32, (heads, groups * LANES), 1)
        hidx = lax.broadcasted_iota(I32, (heads, 1), 0)
        gbase = (hidx // rep) * LANES + 3 * (hidx % rep)
        gate = lambda c: _sigmoid(jnp.sum(jnp.where(glane == gbase + c, graw, 0.0), axis=-1, keepdims=True))
        o = gate(0) * oc_ref[0] + gate(1) * o_s + gate(2) * o_w
        for h in range(heads):
            z = z_ref[0, :, h * HEAD_DIM:(h + 1) * HEAD_DIM]
            act_ref[0, :, h * HEAD_DIM:(h + 1) * HEAD_DIM] = (o[h:h + 1, :] * _silu(z)).astype(BF16)


def _nsa_sample_attend(page_table, sel_idx, q3, cache5, layer, new_sel, new_win, o_c, win_state, gates3,
                       z3, *, groups, rep, n_gather):
    db, heads, _ = q3.shape
    kvw = groups * HEAD_DIM
    n_buf = win_state.shape[2]
    n_rows = 2 * groups
    bpp = PAGE_SIZE // CMP_BLOCK
    assert sel_idx.shape[2] == n_gather
    per_step = max(u for u in range(1, 6) if n_gather % u == 0)
    n_steps = n_gather // per_step

    def gather_spec(g, u):
        def index_map(bi, k, pt, si):
            blk = si[bi, g, k * per_step + u]
            return (layer, pt[bi, blk // bpp], blk % bpp, 0, 0)
        return pl.BlockSpec((None, None, CMP_BLOCK, n_rows, HEAD_DIM), index_map)

    in_specs = [pl.BlockSpec((1, heads, HEAD_DIM), lambda bi, k, pt, si: (bi, 0, 0))]
    in_specs += [gather_spec(g, u) for g in range(groups) for u in range(per_step)]
    row3 = lambda w: pl.BlockSpec((1, 1, w), lambda bi, k, pt, si: (bi, 0, 0))
    in_specs += [
        row3(2 * kvw), row3(2 * kvw),
        pl.BlockSpec((1, heads, HEAD_DIM), lambda bi, k, pt, si: (bi, 0, 0)),
        pl.BlockSpec((None, 1, n_buf, n_rows, HEAD_DIM), lambda bi, k, pt, si: (layer, bi, 0, 0, 0)),
        row3(groups * LANES), row3(heads * HEAD_DIM),
    ]
    grid_spec = pltpu.PrefetchScalarGridSpec(
        num_scalar_prefetch=2,
        grid=(db, n_steps),
        in_specs=in_specs,
        out_specs=[
            row3(heads * HEAD_DIM),
            pl.BlockSpec((1, n_buf, n_rows, HEAD_DIM), lambda bi, k, pt, si: (bi, 0, 0, 0)),
        ],
        scratch_shapes=[pltpu.VMEM((heads, 1), F32), pltpu.VMEM((heads, 1), F32),
                        pltpu.VMEM((heads, HEAD_DIM), F32)],
    )
    kern = functools.partial(_nsa_sample_attend_kernel, groups=groups, rep=rep, n_steps=n_steps,
                             per_step=per_step, n_buf=n_buf)
    return pl.pallas_call(
        kern,
        grid_spec=grid_spec,
        out_shape=[
            jax.ShapeDtypeStruct((db, 1, heads * HEAD_DIM), BF16),
            jax.ShapeDtypeStruct((db, n_buf, n_rows, HEAD_DIM), F32),
        ],
        compiler_params=_cparams(32, ("arbitrary", "arbitrary")),
        name="nsa_sample_attend",
    )(page_table, sel_idx, q3, *([cache5] * (groups * per_step)), new_sel, new_win, o_c, win_state,
      gates3, z3)


def _rope_tables(pos):
    half = ROT_DIM // 2
    inv_freq = ROPE_THETA ** (-jnp.arange(half, dtype=F32) * (2.0 / ROT_DIM))
    ang = pos.astype(F32)[:, None] * inv_freq[None, :]
    cos, sin = jnp.cos(ang), jnp.sin(ang)
    n = pos.shape[0]
    cos_t = jnp.concatenate([cos, cos, jnp.ones((n, HEAD_DIM - ROT_DIM), F32)], axis=1)
    sin_t = jnp.concatenate([-sin, sin, jnp.zeros((n, HEAD_DIM - ROT_DIM), F32)], axis=1)
    return cos_t, sin_t


def _odd_weight_layout(w, cw, kvw, rep):
    n_layers, k, _ = w.shape
    groups = kvw // HEAD_DIM
    n_qkv = cw + 6 * kvw
    n_gate = 3 * groups * rep
    gates = w[:, :, n_qkv:n_qkv + n_gate].reshape(n_layers, k, groups, 3 * rep)
    gates = jnp.pad(gates, ((0, 0), (0, 0), (0, 0), (0, LANES - 3 * rep))).reshape(n_layers, k, groups * LANES)
    return jnp.concatenate([w[:, :, n_qkv + n_gate:], gates], axis=2)


def kernel(x_prompt, x_sample, cache_cmp_kv, cache_sel_kv, state_win_kv, state_pool, page_table,
           norm_even, w_in_even, v_norm, w_spatial, b_spatial, w_pool, pool_scale, w_out_even,
           norm_odd, w_in_odd, q_norm, k_norm, w_out_odd):
    bsz, seq, d_model = x_prompt.shape
    db, dec_t, _ = x_sample.shape
    assert dec_t == 1
    n_even, n_odd = norm_even.shape[0], norm_odd.shape[0]
    depth = n_even + n_odd
    n_pages = page_table.shape[1]
    past_len = n_pages * PAGE_SIZE
    aw = v_norm.shape[1]
    bw = pool_scale.shape[1]
    groups = C_KV_HEADS
    kvw = groups * HEAD_DIM
    cw = w_out_odd.shape[1]
    rep = cw // kvw
    heads = cw // HEAD_DIM
    n_phys = cache_cmp_kv.shape[1]
    n_buf = state_win_kv.shape[2]
    assert past_len % CMP_BLOCK == 0 and seq % CMP_BLOCK == 0
    nb_p = seq // CMP_BLOCK
    nb_past = past_len // CMP_BLOCK
    n_past_sel = min(N_SEL, nb_past + 1) - 1
    assert nb_past >= 2 and n_past_sel >= 2

    xp = x_prompt.reshape(bsz * seq, d_model)
    xs = x_sample.reshape(db, d_model)

    cos_p, sin_p = _rope_tables(jnp.arange(seq, dtype=I32))
    cos_s, sin_s = _rope_tables(jnp.full((db,), past_len, dtype=I32))
    tables = _mask_tables(seq, LANES, WINDOW + LANES, rep)

    cache_cmp5 = cache_cmp_kv.reshape(n_odd, n_phys, PAGE_SIZE, 2 * groups, HEAD_DIM)
    cache_sel5 = cache_sel_kv.reshape(n_odd, n_phys, PAGE_SIZE, 2 * groups, HEAD_DIM)
    win_state5 = state_win_kv.reshape(n_odd, db, n_buf, 2 * groups, HEAD_DIM)

    tn_in = 2560
    n_even_in = w_in_even.shape[2]
    n_qkv = cw + 6 * kvw
    assert n_even_in % tn_in == 0 and n_qkv % tn_in == 0
    w_in_even_b = w_in_even.astype(BF16)
    w_out_even_b = w_out_even.astype(BF16)
    w_in_odd_b = w_in_odd.astype(BF16)
    w_out_odd_b = w_out_odd.astype(BF16)
    w_zg_b = _odd_weight_layout(w_in_odd_b, cw, kvw, rep)
    n_zg = w_zg_b.shape[2]

    cmp_p, cmp_s, sel_p, sel_s, win_p, win_s = [], [], [], [], [], []
    pool_p, pool_s, gv_s = [], [], []
    for layer in range(depth):
        li = layer // 2
        if layer % 2 == 0:
            g_in = norm_even[li][None, :]
            vg = v_norm[li][None, :]
            ps = pool_scale[li][None, :]
            wp = w_pool[li].astype(BF16)
            proj = _norm_matmul(xp, g_in, w_in_even_b, li, n_even_in, tn=tn_in)
            act, pool16 = _even_mix(proj.reshape(bsz, seq, -1), vg, w_spatial[li], b_spatial[li].T, wp, ps)
            xp = _matmul_residual(act.reshape(bsz * seq, aw + bw), w_out_even_b, li, xp)
            pool_p.append(pool16[:, 16 - POOL_PAD:])
            proj_s = _norm_matmul(xs, g_in, w_in_even_b, li, n_even_in, tn=tn_in)
            hd = aw // A_HEADS
            w00 = jnp.repeat(w_spatial[li][:, 0, 0], hd)[None, :]
            b0 = jnp.repeat(b_spatial[li][:, 0], hd)[None, :]
            act_s, vn_s, new_state = _even_mix_sample(
                proj_s, vg, w00, b0, wp, ps, jnp.swapaxes(state_pool[li], 0, 1), past_len)
            xs = _matmul_residual(act_s, w_out_even_b, li, xs)
            pool_s.append(jnp.swapaxes(new_state, 0, 1))
            gv_s.append(vn_s.reshape(db, 1, aw))
        else:
            g_in = norm_odd[li][None, :]
            qg = q_norm[li][None, :]
            kg = k_norm[li]
            proj = _norm_matmul(xp, g_in, w_in_odd_b, li, n_qkv, tn=tn_in)
            proj_zg = _norm_matmul(xp, g_in, w_zg_b, li, n_zg, tn=n_zg)
            q_b, kv_cmp, kv_sel, kv_win, kt_sel, v_sel, kt_win, v_win, means = _odd_post(
                proj, cos_p, sin_p, qg, kg, cw=cw, kvw=kvw, for_prompt=True, win_keep=min(WINDOW, seq))
            r3 = lambda a: a.reshape(bsz, seq, a.shape[-1])
            act, means_s = _nsa_prompt(r3(q_b), kt_sel, r3(v_sel), kt_win, r3(v_win),
                                       means.reshape(bsz, nb_p, 2 * kvw), r3(proj_zg), tables,
                                       cache_cmp5, page_table, li, cw=cw, z_off=0, g_off=cw)
            xp = _matmul_residual(act.reshape(bsz * seq, cw), w_out_odd_b, li, xp)
            kv6 = lambda a, n: a.reshape(n, -1, 2, groups, HEAD_DIM)
            cmp_p.append(kv6(kv_cmp, bsz))
            sel_p.append(kv6(kv_sel, bsz))
            win_p.append(kv6(kv_win, bsz))
            proj_s = _norm_matmul(xs, g_in, w_in_odd_b, li, n_qkv, tn=tn_in)
            proj_zg_s = _norm_matmul(xs, g_in, w_zg_b, li, n_zg, tn=n_zg)
            q_s, kvc_s, kvs_s, kvw_s = _odd_post(
                proj_s, cos_s, sin_s, qg, kg, cw=cw, kvw=kvw, for_prompt=False)
            q3s = q_s.reshape(db, heads, HEAD_DIM)
            o_c, idx = _nsa_sample_select(q3s, means_s, groups=groups, rep=rep, q_pos=past_len,
                                          n_past_sel=n_past_sel)
            act_s, win_new = _nsa_sample_attend(
                page_table, idx[:, :, :n_past_sel], q3s, cache_sel5, li,
                kvs_s.reshape(db, 1, 2 * kvw), kvw_s.reshape(db, 1, 2 * kvw), o_c, win_state5,
                proj_zg_s[:, cw:].reshape(db, 1, groups * LANES),
                proj_zg_s[:, :cw].reshape(db, 1, cw),
                groups=groups, rep=rep, n_gather=n_past_sel)
            xs = _matmul_residual(act_s.reshape(db, cw), w_out_odd_b, li, xs)
            cmp_s.append(kv6(kvc_s, db))
            sel_s.append(kv6(kvs_s, db))
            win_s.append(win_new.reshape(db, n_buf, 2, groups, HEAD_DIM))
    return (xp.reshape(bsz, seq, d_model), xs.reshape(db, 1, d_model),
            jnp.stack(cmp_p), jnp.stack(cmp_s), jnp.stack(sel_p), jnp.stack(sel_s),
            jnp.stack(win_p), jnp.stack(win_s), jnp.stack(pool_p), jnp.stack(pool_s), jnp.stack(gv_s))
```

```python
import functools

import jax
import jax.numpy as jnp
import numpy as np
from jax import lax
from jax.experimental import pallas as pl
from jax.experimental.pallas import tpu as pltpu

F32 = jnp.float32
BF16 = jnp.bfloat16
I32 = jnp.int32

EPS = 1e-6
PAGE_SIZE = 128
A_HEADS = 8
GMLP_CHUNK = 128
POOL_WINDOWS = (2, 4, 8, 16)
POOL_PAD = max(POOL_WINDOWS) - 1
HEAD_DIM = 128
C_KV_HEADS = 4
CMP_BLOCK = 64
N_SEL = 16
WINDOW = 512
SEL_FORCE = 1e4
NEG_INF = -1e30
ROPE_THETA = 500000.0
ROT_DIM = HEAD_DIM // 4
LOG2E = 1.4426950408889634
Q_SCALE = HEAD_DIM ** -0.5 * LOG2E

LANES = 128
SUBLANES = 8
MIB = 1024 * 1024


def _cparams(vmem_mib, semantics=None):
    return pltpu.CompilerParams(vmem_limit_bytes=int(vmem_mib * MIB), dimension_semantics=semantics)


def _silu(x):
    return x * (1.0 / (1.0 + jnp.exp(-x)))


def _sigmoid(x):
    return 1.0 / (1.0 + jnp.exp(-x))


def _dot_nt(a, b):
    return lax.dot_general(a, b, (((1,), (1,)), ((), ())), preferred_element_type=F32)


def _dot(a, b):
    return jnp.dot(a, b, preferred_element_type=F32)


def _norm_mm_kernel(x_ref, g_ref, w_ref, o_ref):
    x = x_ref[...]
    ms = jnp.mean(x * x, axis=-1, keepdims=True)
    h = (x * lax.rsqrt(ms + EPS) * g_ref[...]).astype(BF16)
    o_ref[...] = jnp.dot(h, w_ref[...], preferred_element_type=F32)


def _mm_res_kernel(a_ref, w_ref, r_ref, o_ref):
    o_ref[...] = r_ref[...] + jnp.dot(a_ref[...], w_ref[...], preferred_element_type=F32)


def _row_tile(m, want):
    return want if m % want == 0 else m


def _norm_matmul(x, g, w, layer, n, *, tn, tm=512):
    m, k = x.shape
    tm = _row_tile(m, tm)
    assert n % tn == 0 and n <= w.shape[2]
    vmem = 2 * (tm * k * 4 + k * tn * 2 + tm * tn * 4) / MIB + 8
    return pl.pallas_call(
        _norm_mm_kernel,
        grid=(n // tn, m // tm),
        in_specs=[
            pl.BlockSpec((tm, k), lambda j, i: (i, 0)),
            pl.BlockSpec((1, k), lambda j, i: (0, 0)),
            pl.BlockSpec((None, k, tn), lambda j, i: (layer, 0, j)),
        ],
        out_specs=pl.BlockSpec((tm, tn), lambda j, i: (i, j)),
        out_shape=jax.ShapeDtypeStruct((m, n), F32),
        compiler_params=_cparams(vmem),
        name="norm_matmul",
    )(x, g, w)


def _matmul_residual(a, w, layer, res, *, tm=512):
    m, k = a.shape
    n = w.shape[2]
    tm = _row_tile(m, tm)
    vmem = 2 * (tm * k * 2 + k * n * 2 + 2 * tm * n * 4) / MIB + 8
    return pl.pallas_call(
        _mm_res_kernel,
        grid=(m // tm,),
        in_specs=[
            pl.BlockSpec((tm, k), lambda i: (i, 0)),
            pl.BlockSpec((None, k, n), lambda i: (layer, 0, 0)),
            pl.BlockSpec((tm, n), lambda i: (i, 0)),
        ],
        out_specs=pl.BlockSpec((tm, n), lambda i: (i, 0)),
        out_shape=jax.ShapeDtypeStruct((m, n), F32),
        compiler_params=_cparams(vmem),
        name="matmul_residual",
    )(a, w, res)


def _even_mix_kernel(proj_ref, x_ref, vg_ref, ws_ref, bst_ref, wp_ref, ps_ref, wout_ref,
                     y_ref, pool_ref, ext_ref, act_ref, *, tm, aw, bw):
    c = pl.program_id(1)
    n_c = pl.num_programs(1)
    hd = aw // A_HEADS
    pg = bw // len(POOL_WINDOWS)

    cl = GMLP_CHUNK
    row = lax.broadcasted_iota(I32, (cl, cl), 0)
    col = lax.broadcasted_iota(I32, (cl, cl), 1)
    causal = row >= col

    for h in range(A_HEADS):
        sl = slice(h * hd, (h + 1) * hd)
        w = jnp.where(causal, ws_ref[h], 0.0).astype(BF16)
        for c0 in range(0, tm, cl):
            rs = slice(c0, c0 + cl)
            u = proj_ref[0, rs, sl]
            v = proj_ref[0, rs, aw + h * hd:aw + (h + 1) * hd]
            za = proj_ref[0, rs, 2 * aw + h * hd:2 * aw + (h + 1) * hd]
            ms = jnp.mean(v * v, axis=-1, keepdims=True)
            vn = v * lax.rsqrt(ms + EPS) * vg_ref[:, sl]
            s = jnp.dot(w, vn.astype(BF16), preferred_element_type=F32) + bst_ref[:, h:h + 1]
            act_ref[0, rs, sl] = ((u * s) * _silu(za)).astype(BF16)

    p_off = 3 * aw
    zb_off = 3 * aw + bw

    @pl.when(c == 0)
    def _():
        ext_ref[0:16, :] = jnp.zeros((16, bw), F32)

    ext_ref[16:16 + tm, :] = proj_ref[0, :, p_off:p_off + bw]
    pos = c * tm + lax.broadcasted_iota(I32, (tm, 1), 0)
    for g, wnd in enumerate(POOL_WINDOWS):
        sl = slice(g * pg, (g + 1) * pg)
        cur = ext_ref[16:16 + tm, sl]
        acc = cur
        for k in range(1, wnd):
            acc = acc + ext_ref[16 - k:16 - k + tm, sl]
        cnt = jnp.minimum(pos + 1, wnd).astype(F32)
        d = acc / cnt - cur
        y = jnp.dot(d.astype(BF16), wp_ref[g], preferred_element_type=F32)
        zb = proj_ref[0, :, zb_off + g * pg:zb_off + (g + 1) * pg]
        act_ref[0, :, aw + g * pg:aw + (g + 1) * pg] = ((y * ps_ref[:, sl]) * _silu(zb)).astype(BF16)

    tail = ext_ref[tm:tm + 16, :]
    ext_ref[0:16, :] = tail

    @pl.when(c == n_c - 1)
    def _():
        pool_ref[0] = tail

    y_ref[0] = x_ref[0] + jnp.dot(act_ref[0], wout_ref[...], preferred_element_type=F32)


def _even_mix(proj3, x3, v_gain, w_s, b_st, w_pool, pool_scale, w_out, layer):
    b, s, _ = proj3.shape
    d = x3.shape[2]
    aw = v_gain.shape[1]
    bw = pool_scale.shape[1]
    tm = 2 * GMLP_CHUNK if s % (2 * GMLP_CHUNK) == 0 else GMLP_CHUNK
    assert s % tm == 0 and s >= 16
    n_in = 3 * aw + 2 * bw
    kern = functools.partial(_even_mix_kernel, tm=tm, aw=aw, bw=bw)
    return pl.pallas_call(
        kern,
        grid=(b, s // tm),
        in_specs=[
            pl.BlockSpec((1, tm, n_in), lambda i, c: (i, c, 0)),
            pl.BlockSpec((1, tm, d), lambda i, c: (i, c, 0)),
            pl.BlockSpec((1, aw), lambda i, c: (0, 0)),
            pl.BlockSpec(w_s.shape, lambda i, c: (0, 0, 0)),
            pl.BlockSpec(b_st.shape, lambda i, c: (0, 0)),
            pl.BlockSpec(w_pool.shape, lambda i, c: (0, 0, 0)),
            pl.BlockSpec((1, bw), lambda i, c: (0, 0)),
            pl.BlockSpec((None, aw + bw, d), lambda i, c: (layer, 0, 0)),
        ],
        out_specs=[
            pl.BlockSpec((1, tm, d), lambda i, c: (i, c, 0)),
            pl.BlockSpec((1, 16, bw), lambda i, c: (i, 0, 0)),
        ],
        out_shape=[
            jax.ShapeDtypeStruct((b, s, d), F32),
            jax.ShapeDtypeStruct((b, 16, bw), F32),
        ],
        scratch_shapes=[pltpu.VMEM((16 + tm, bw), F32), pltpu.VMEM((1, tm, aw + bw), BF16)],
        compiler_params=_cparams(48, ("arbitrary", "arbitrary")),
        name="even_mix",
    )(proj3, x3, v_gain, w_s, b_st, w_pool, pool_scale, w_out)


def _even_mix_sample_kernel(proj_ref, vg_ref, w00_ref, b0_ref, wp_ref, ps_ref, st_ref,
                            act_ref, vn_ref, newst_ref, *, aw, bw, pos):
    hd = aw // A_HEADS
    pg = bw // len(POOL_WINDOWS)
    for h in range(A_HEADS):
        sl = slice(h * hd, (h + 1) * hd)
        u = proj_ref[:, sl]
        v = proj_ref[:, aw + h * hd:aw + (h + 1) * hd]
        za = proj_ref[:, 2 * aw + h * hd:2 * aw + (h + 1) * hd]
        ms = jnp.mean(v * v, axis=-1, keepdims=True)
        vn = v * lax.rsqrt(ms + EPS) * vg_ref[:, sl]
        vn_ref[:, sl] = vn
        s = w00_ref[:, sl] * vn + b0_ref[:, sl]
        act_ref[:, sl] = ((u * s) * _silu(za)).astype(BF16)

    p = proj_ref[:, 3 * aw:3 * aw + bw]
    for g, wnd in enumerate(POOL_WINDOWS):
        sl = slice(g * pg, (g + 1) * pg)
        cur = p[:, sl]
        acc = cur
        for k in range(1, wnd):
            acc = acc + st_ref[POOL_PAD - k, :, sl]
        cnt = float(min(pos + 1, wnd))
        d = acc / cnt - cur
        y = jnp.dot(d.astype(BF16), wp_ref[g], preferred_element_type=F32)
        zb = proj_ref[:, 3 * aw + bw + g * pg:3 * aw + bw + (g + 1) * pg]
        act_ref[:, aw + g * pg:aw + (g + 1) * pg] = ((y * ps_ref[:, sl]) * _silu(zb)).astype(BF16)

    for k in range(POOL_PAD - 1):
        newst_ref[k] = st_ref[k + 1]
    newst_ref[POOL_PAD - 1] = p


def _even_mix_sample(proj, v_gain, w00, b0, w_pool, pool_scale, state_t, pos):
    db = proj.shape[0]
    aw = v_gain.shape[1]
    bw = pool_scale.shape[1]
    kern = functools.partial(_even_mix_sample_kernel, aw=aw, bw=bw, pos=pos)
    return pl.pallas_call(
        kern,
        out_shape=[
            jax.ShapeDtypeStruct((db, aw + bw), BF16),
            jax.ShapeDtypeStruct((db, aw), F32),
            jax.ShapeDtypeStruct((POOL_PAD, db, bw), F32),
        ],
        name="even_mix_sample",
    )(proj, v_gain, w00, b0, w_pool, pool_scale, state_t)


def _odd_post_kernel(proj_ref, cos_ref, sin_ref, qg_ref, kg_ref, *out_refs, tm, cw, kvw, for_prompt):
    q_out, cmp_out, sel_out, win_out = out_refs[:4]
    if for_prompt:
        selkt_out, selv_out, winkt_out, winv_out, means_out = out_refs[4:]
        attn_outs = (None, (selkt_out, selv_out), (winkt_out, winv_out))
    else:
        attn_outs = (None, None, None)
    cosf = cos_ref[...]
    sinf = sin_ref[...]
    half = ROT_DIM // 2
    first = lax.broadcasted_iota(I32, (tm, HEAD_DIM), 1) < half

    def norm_rope(x, gain):
        ms = jnp.mean(x * x, axis=-1, keepdims=True)
        y = x * lax.rsqrt(ms + EPS) * gain
        rot = jnp.where(first, pltpu.roll(y, HEAD_DIM - half, 1), pltpu.roll(y, half, 1))
        return y * cosf + rot * sinf

    qg = qg_ref[...]
    for h in range(cw // HEAD_DIM):
        sl = slice(h * HEAD_DIM, (h + 1) * HEAD_DIM)
        q_out[:, sl] = (norm_rope(proj_ref[:, sl], qg) * Q_SCALE).astype(BF16)

    for br, (o32, attn) in enumerate(zip((cmp_out, sel_out, win_out), attn_outs)):
        k_off = cw + br * 2 * kvw
        v_off = k_off + kvw
        kg = kg_ref[br:br + 1, :]
        pieces = []
        for g in range(kvw // HEAD_DIM):
            sl = slice(g * HEAD_DIM, (g + 1) * HEAD_DIM)
            kk = norm_rope(proj_ref[:, k_off + g * HEAD_DIM:k_off + (g + 1) * HEAD_DIM], kg)
            pieces.append(kk)
            if attn is not None:
                attn[0][sl, :] = kk.T.astype(BF16)
        vv = proj_ref[:, v_off:v_off + kvw]
        if attn is not None:
            attn[1][...] = vv.astype(BF16)
        row = jnp.concatenate(pieces + [vv], axis=1)
        o32[...] = row.reshape(o32.shape)
        if br == 0 and for_prompt:
            for j in range(tm // CMP_BLOCK):
                blk = row[j * CMP_BLOCK:(j + 1) * CMP_BLOCK, :]
                means_out[0, j:j + 1, :] = jnp.mean(blk, axis=0, keepdims=True)


def _odd_post(proj, cos_t, sin_t, q_gain, k_gain, *, cw, kvw, for_prompt, win_keep=0, tm=512):
    m = proj.shape[0]
    tm = _row_tile(m, tm)
    n_pos_tiles = cos_t.shape[0] // tm
    n_used = cw + 6 * kvw
    kern = functools.partial(_odd_post_kernel, tm=tm, cw=cw, kvw=kvw, for_prompt=for_prompt)
    row_spec = lambda w: pl.BlockSpec((tm, w), lambda i: (i, 0))
    n_rows = 2 * kvw // HEAD_DIM
    if for_prompt:
        assert win_keep % tm == 0 and cos_t.shape[0] >= win_keep
        keep_tiles = win_keep // tm
        skip_tiles = n_pos_tiles - keep_tiles
        kv_shape = jax.ShapeDtypeStruct((m, n_rows, HEAD_DIM), F32)
        kv_spec = pl.BlockSpec((tm, n_rows, HEAD_DIM), lambda i: (i, 0, 0))
        win_shape = jax.ShapeDtypeStruct((m // n_pos_tiles * keep_tiles, n_rows, HEAD_DIM), F32)
        win_spec = pl.BlockSpec(
            (tm, n_rows, HEAD_DIM),
            lambda i: (i // n_pos_tiles * keep_tiles + jnp.maximum(i % n_pos_tiles - skip_tiles, 0), 0, 0))
    else:
        kv_shape = win_shape = jax.ShapeDtypeStruct((m, 2 * kvw), F32)
        kv_spec = win_spec = row_spec(2 * kvw)
    out_shape = [jax.ShapeDtypeStruct((m, cw), BF16), kv_shape, kv_shape, win_shape]
    out_specs = [row_spec(cw), kv_spec, kv_spec, win_spec]
    if for_prompt:
        assert tm % CMP_BLOCK == 0 and tm % LANES == 0
        for _ in range(2):
            out_shape += [jax.ShapeDtypeStruct((kvw, m), BF16), jax.ShapeDtypeStruct((m, kvw), BF16)]
            out_specs += [pl.BlockSpec((kvw, tm), lambda i: (0, i)), row_spec(kvw)]
        out_shape.append(jax.ShapeDtypeStruct((m // tm, tm // CMP_BLOCK, 2 * kvw), F32))
        out_specs.append(pl.BlockSpec((1, tm // CMP_BLOCK, 2 * kvw), lambda i: (i, 0, 0)))
    return pl.pallas_call(
        kern,
        grid=(m // tm,),
        in_specs=[
            pl.BlockSpec((tm, n_used), lambda i: (i, 0)),
            pl.BlockSpec((tm, HEAD_DIM), lambda i: (i % n_pos_tiles, 0)),
            pl.BlockSpec((tm, HEAD_DIM), lambda i: (i % n_pos_tiles, 0)),
            pl.BlockSpec((1, HEAD_DIM), lambda i: (0, 0)),
            pl.BlockSpec((3, HEAD_DIM), lambda i: (0, 0)),
        ],
        out_specs=out_specs,
        out_shape=out_shape,
        compiler_params=_cparams(56),
        name="odd_post",
    )(proj, cos_t, sin_t, q_gain, k_gain)


def _nsa_sub_tile(i, h, q_ref, ks_ref, vs_ref, kw_ref, vw_ref, kc, vc, band_ref, tri_ref, eye_ref,
                  *, tq, nb, rep, lw):
    rows = rep * tq
    r0 = h * tq
    q = jnp.concatenate([q_ref[0, r0:r0 + tq, r * HEAD_DIM:(r + 1) * HEAD_DIM] for r in range(rep)],
                        axis=0)
    t_col = i * tq + lax.broadcasted_iota(I32, (tq, 1), 0)
    t_row = i * tq + lax.broadcasted_iota(I32, (1, tq), 1)

    blk_row = lax.broadcasted_iota(I32, (1, nb), 1)
    s_c = _dot_nt(q, kc).reshape(rep, tq, nb)
    ok_c = ((blk_row + 1) * CMP_BLOCK - 1) <= t_col
    s_c = s_c + jnp.where(ok_c, 0.0, NEG_INF)[None]
    m_c = jnp.max(s_c, axis=-1, keepdims=True)
    e_c = jnp.exp2(s_c - m_c)
    p_c = e_c / jnp.sum(e_c, axis=-1, keepdims=True) * jnp.where(ok_c, 1.0, 0.0)[None]
    o_c = jnp.dot(p_c.reshape(rows, nb).astype(BF16), vc, preferred_element_type=F32)

    blk_col = lax.broadcasted_iota(I32, (nb, 1), 0)
    s_t = _dot_nt(kc, q)
    ok_t = ((blk_col + 1) * CMP_BLOCK - 1) <= t_row
    bias_t = jnp.where(ok_t, 0.0, NEG_INF)
    okf_t = jnp.where(ok_t, 1.0, 0.0)
    imp = jnp.zeros((nb, tq), F32)
    for r in range(rep):
        s_r = s_t[:, r * tq:(r + 1) * tq] + bias_t
        e_r = jnp.exp2(s_r - jnp.max(s_r, axis=0, keepdims=True))
        imp = imp + e_r / jnp.sum(e_r, axis=0, keepdims=True) * okf_t
    cur = t_row // CMP_BLOCK
    forced = (blk_col == 0) | (blk_col == cur) | (blk_col == cur - 1)
    imp = jnp.where(blk_col <= cur, jnp.where(forced, SEL_FORCE, imp), -1.0)

    blk8 = lax.broadcasted_iota(I32, (SUBLANES, 1), 0)
    cnt = jnp.zeros((nb, tq), F32)
    for j in range(nb):
        rj = imp[j:j + 1, :]
        parts = []
        for v0 in range(0, nb, SUBLANES):
            sub = imp[v0:v0 + SUBLANES, :]
            if v0 > j:
                beats = rj >= sub
            elif v0 + SUBLANES - 1 <= j:
                beats = rj > sub
            else:
                beats = (rj > sub) | ((rj == sub) & (blk8 > j - v0))
            parts.append(jnp.where(beats, 1.0, 0.0))
        cnt = cnt + jnp.concatenate(parts, axis=0)
    keep = (cnt < min(N_SEL, nb)) & (imp >= 0.0) & (blk_col < i * (tq // CMP_BLOCK))
    drop_t = jnp.where(keep, 0.0, 1.0)
    if LANES > nb:
        drop_t = jnp.concatenate([drop_t, jnp.zeros((LANES - nb, tq), F32)], axis=0)
    drop = drop_t.T.astype(BF16)
    q_drop = jnp.concatenate([q, jnp.concatenate([drop] * rep, axis=0)], axis=1)

    q_eye = jnp.concatenate([q, eye_ref[...]], axis=1)

    w0 = pl.multiple_of(jnp.maximum(i * tq + tq - lw, 0), tq)
    band = band_ref[jnp.minimum(i, band_ref.shape[0] - 1)]
    k_w = jnp.concatenate([kw_ref[:, pl.ds(w0, lw)], band], axis=0)
    v_w = jnp.concatenate([vw_ref[0, pl.ds(w0, lw), :], jnp.ones((lw, HEAD_DIM), BF16)], axis=1)
    s_w = _dot(q_eye, k_w)
    p_w = jnp.exp2(s_w - jnp.max(s_w, axis=-1, keepdims=True))
    acc_w = _dot(p_w.astype(BF16), v_w)
    o_w = acc_w[:, :HEAD_DIM] / acc_w[:, HEAD_DIM:]

    d0 = pl.multiple_of(i * tq, tq)
    k_d = jnp.concatenate([ks_ref[:, pl.ds(d0, tq)], tri_ref[...]], axis=0)
    v_d = jnp.concatenate([vs_ref[0, pl.ds(d0, tq), :], jnp.ones((tq, HEAD_DIM), BF16)], axis=1)
    s_d = _dot(q_eye, k_d)
    m_d = jnp.max(s_d, axis=-1, keepdims=True)
    acc_d = _dot(jnp.exp2(s_d - m_d).astype(BF16), v_d)
    return q_drop, o_c, o_w, m_d, acc_d


def _nsa_prompt_kernel(pt_ref, q_ref, ks_ref, vs_ref, kw_ref, vw_ref, kc_ref, vc_ref, g_ref, z_ref,
                       eb_ref, band_ref, tri_ref, eye_ref, *rest, tq, nh, tk, seq, nb, rep, lw, pps):
    if pps:
        o_ref, pm_ref, s0_ref, s1_ref = rest[pps:]
        _store_page_means(rest[:pps], pm_ref)
    else:
        o_ref, s0_ref, s1_ref = rest
    step = pl.program_id(2)
    kc = kc_ref[0].astype(BF16)
    vc = vc_ref[0].astype(BF16)
    subs = [_nsa_sub_tile(step * nh + h, h, q_ref, ks_ref, vs_ref, kw_ref, vw_ref, kc, vc, band_ref,
                          tri_ref, eye_ref, tq=tq, nb=nb, rep=rep, lw=lw) for h in range(nh)]

    q_drop = jnp.concatenate([s[0] for s in subs], axis=0)
    m_0 = jnp.concatenate([s[3] for s in subs], axis=0)
    acc_0 = jnp.concatenate([s[4] for s in subs], axis=0)

    partial, gate_s, z_act = [], [], []
    for h in range(nh):
        ts = slice(h * tq, (h + 1) * tq)
        gs = _sigmoid(g_ref[0, ts, :])
        o_c, o_w = subs[h][1], subs[h][2]
        for r in range(rep):
            rs = slice(r * tq, (r + 1) * tq)
            partial.append(gs[:, 3 * r:3 * r + 1] * o_c[rs] + gs[:, 3 * r + 2:3 * r + 3] * o_w[rs])
            gate_s.append(gs[:, 3 * r + 1:3 * r + 2])
            z_act.append(_silu(z_ref[0, ts, r * HEAD_DIM:(r + 1) * HEAD_DIM]))

    ones_k = jnp.ones((tk, HEAD_DIM), BF16)
    n_tiles = seq // tk

    def tile_scores(t):
        k0 = pl.multiple_of(t * tk, tk)
        k = jnp.concatenate([ks_ref[:, pl.ds(k0, tk)], eb_ref[:, pl.ds(k0, tk)]], axis=0)
        return _dot(q_drop, k)

    def accumulate(carry, s, t):
        m_p, acc = carry
        k0 = pl.multiple_of(t * tk, tk)
        v = jnp.concatenate([vs_ref[0, pl.ds(k0, tk), :], ones_k], axis=1)
        m_n = jnp.maximum(m_p, jnp.max(s, axis=-1, keepdims=True))
        pv = _dot(jnp.exp2(s - m_n).astype(BF16), v)
        return m_n, jnp.exp2(m_p - m_n) * acc + pv

    def sel_body(j, carry):
        t1 = 2 * j + 1
        s1_ref[...] = tile_scores(t1)
        carry = accumulate(carry, s0_ref[...], 2 * j)
        s0_ref[...] = tile_scores(jnp.minimum(t1 + 1, n_tiles - 1))
        return accumulate(carry, s1_ref[...], t1)

    n_past = ((step * nh + nh - 1) * tq + tk - 1) // tk
    s0_ref[...] = tile_scores(0)
    _, acc_s = lax.fori_loop(0, (n_past + 1) // 2, sel_body, (m_0, acc_0))
    o_s = acc_s[:, :HEAD_DIM] / acc_s[:, HEAD_DIM:]

    for h in range(nh):
        for r in range(rep):
            n = h * rep + r
            o = partial[n] + gate_s[n] * o_s[n * tq:(n + 1) * tq]
            o_ref[0, h * tq:(h + 1) * tq, r * HEAD_DIM:(r + 1) * HEAD_DIM] = (o * z_act[n]).astype(BF16)


def _mask_tables(seq, tq, lw, rep):
    neg = lambda ok: jnp.where(ok, 0.0, NEG_INF).astype(BF16)
    row = jnp.arange(LANES, dtype=I32)[:, None]
    key = jnp.arange(seq, dtype=I32)[None, :]
    block_rows = neg(key // CMP_BLOCK != row)
    t = jnp.arange(tq, dtype=I32)[None, :, None]
    c = jnp.arange(lw, dtype=I32)[None, None, :]
    off = jnp.arange(WINDOW // tq + 1, dtype=I32)[:, None, None] * tq
    rel = off + t - c
    band = neg((rel >= 0) & (rel < WINDOW))
    tri = neg(jnp.arange(tq, dtype=I32)[None, :] <= jnp.arange(tq, dtype=I32)[:, None])
    eye = jnp.tile(jnp.eye(tq, dtype=BF16), (rep, 1))
    return block_rows, band, tri, eye


def _page_job_split(n_steps, page_table):
    db, n_pages = page_table.shape
    total = db * n_pages
    if total % n_steps:
        return 0
    pps = total // n_steps
    return pps if n_pages % pps == 0 else 0


def _nsa_prompt(q3, kt_sel, v_sel, kt_win, v_win, means3, proj3, tables, cache5, page_table, layer, *,
                cw, z_off, g_off, tq=LANES, nh=2, tk=512):
    b, s, _ = q3.shape
    kvw = v_sel.shape[2]
    groups = kvw // HEAD_DIM
    rep = cw // kvw
    nb = means3.shape[1]
    gw = rep * HEAD_DIM
    block_rows, band, tri, eye = tables
    tk = min(tk, s)
    lw = band.shape[2]
    ts = nh * tq
    assert tq == LANES and nb <= LANES and nb % SUBLANES == 0 and lw == WINDOW + tq and s >= lw
    assert s % ts == 0 and s % (2 * tk) == 0 and tk % tq == 0 and tq % CMP_BLOCK == 0
    assert nb * CMP_BLOCK == s
    assert z_off % gw == 0 and g_off % LANES == 0
    n_i = s // ts
    pps = _page_job_split(b * groups * n_i, page_table)
    kern = functools.partial(_nsa_prompt_kernel, tq=tq, nh=nh, tk=tk, seq=s, nb=nb, rep=rep, lw=lw,
                             pps=pps)
    db, n_pages = page_table.shape
    n_rows, bpp = cache5.shape[3], PAGE_SIZE // CMP_BLOCK
    spp = n_pages // pps if pps else 0
    assert b * groups * n_i * pps in (0, db * n_pages)
    flat = lambda bi, g, i: (bi * groups + g) * n_i + i

    def page_spec(r):
        def index_map(bi, g, i, pt):
            f = flat(bi, g, i)
            return (layer, pt[f // spp, (f % spp) * pps + r], 0, 0, 0)
        return pl.BlockSpec((None, None, PAGE_SIZE, n_rows, HEAD_DIM), index_map)

    kt_spec = pl.BlockSpec((HEAD_DIM, s), lambda bi, g, i, pt: (g, bi))
    v_spec = pl.BlockSpec((1, s, HEAD_DIM), lambda bi, g, i, pt: (bi, 0, g))
    mean_spec = lambda off: pl.BlockSpec((1, nb, HEAD_DIM), lambda bi, g, i, pt: (bi, 0, off + g))
    grid_spec = pltpu.PrefetchScalarGridSpec(
        num_scalar_prefetch=1,
        grid=(b, groups, n_i),
        in_specs=[
            pl.BlockSpec((1, ts, gw), lambda bi, g, i, pt: (bi, i, g)),
            kt_spec, v_spec, kt_spec, v_spec,
            mean_spec(0), mean_spec(groups),
            pl.BlockSpec((1, ts, LANES), lambda bi, g, i, pt: (bi, i, g_off // LANES + g)),
            pl.BlockSpec((1, ts, gw), lambda bi, g, i, pt: (bi, i, z_off // gw + g)),
            pl.BlockSpec(block_rows.shape, lambda bi, g, i, pt: (0, 0)),
            pl.BlockSpec(band.shape, lambda bi, g, i, pt: (0, 0, 0)),
            pl.BlockSpec(tri.shape, lambda bi, g, i, pt: (0, 0)),
            pl.BlockSpec(eye.shape, lambda bi, g, i, pt: (0, 0)),
        ] + [page_spec(r) for r in range(pps)],
        out_specs=[pl.BlockSpec((1, ts, gw), lambda bi, g, i, pt: (bi, i, g))] + ([
            pl.BlockSpec((1, n_rows, pps * bpp, HEAD_DIM),
                         lambda bi, g, i, pt: (flat(bi, g, i) // spp, 0, flat(bi, g, i) % spp, 0)),
        ] if pps else []),
        scratch_shapes=[pltpu.VMEM((nh * rep * tq, tk), F32), pltpu.VMEM((nh * rep * tq, tk), F32)],
    )
    outs = pl.pallas_call(
        kern,
        grid_spec=grid_spec,
        out_shape=[jax.ShapeDtypeStruct((b, s, cw), BF16)] + ([
            jax.ShapeDtypeStruct((db, n_rows, n_pages * bpp, HEAD_DIM), F32)] if pps else []),
        compiler_params=_cparams(56),
        name="nsa_prompt",
    )(page_table, q3, kt_sel, v_sel, kt_win, v_win, means3, means3, proj3, proj3, block_rows, band, tri,
      eye, *([cache5] * pps))
    return outs if pps else (outs[0], _page_means(cache5, page_table, layer))


def _store_page_means(page_refs, out_ref):
    bpp = PAGE_SIZE // CMP_BLOCK
    n_rows = out_ref.shape[1]
    for r, page_ref in enumerate(page_refs):
        for h in range(bpp):
            mean = jnp.mean(page_ref[h * CMP_BLOCK:(h + 1) * CMP_BLOCK], axis=0)
            n = r * bpp + h
            for j in range(n_rows):
                out_ref[0, j, n:n + 1, :] = mean[j:j + 1, :]


def _page_means_kernel(pt_ref, *refs, pps):
    _store_page_means(refs[:pps], refs[pps])


def _page_means(cache5, page_table, layer, *, pps=8):
    db, n_pages = page_table.shape
    n_rows, d = cache5.shape[3], cache5.shape[4]
    pps = pps if n_pages % pps == 0 else 1
    bpp = PAGE_SIZE // CMP_BLOCK

    def page_spec(r):
        return pl.BlockSpec((None, None, PAGE_SIZE, n_rows, d),
                            lambda bi, j, pt: (layer, pt[bi, j * pps + r], 0, 0, 0))

    grid_spec = pltpu.PrefetchScalarGridSpec(
        num_scalar_prefetch=1,
        grid=(db, n_pages // pps),
        in_specs=[page_spec(r) for r in range(pps)],
        out_specs=pl.BlockSpec((1, n_rows, pps * bpp, d), lambda bi, j, pt: (bi, 0, j, 0)),
    )
    return pl.pallas_call(
        functools.partial(_page_means_kernel, pps=pps),
        grid_spec=grid_spec,
        out_shape=jax.ShapeDtypeStruct((db, n_rows, n_pages * bpp, d), F32),
        compiler_params=_cparams(24),
        name="page_means",
    )(page_table, *([cache5] * pps))


def _nsa_sample_select_kernel(q_ref, mean_ref, oc_ref, idx_ref, *, nbp, groups, rep, q_pos, n_past_sel):
    q = q_ref[0]
    heads = groups * rep
    lane = lax.broadcasted_iota(I32, (1, nbp), 1)
    ri = lax.broadcasted_iota(I32, (nbp, nbp), 0)
    ci = lax.broadcasted_iota(I32, (nbp, nbp), 1)
    diag = ri == ci
    ok = ((lane + 1) * CMP_BLOCK - 1) <= q_pos
    cur = q_pos // CMP_BLOCK
    forced = (lane == 0) | (lane == cur) | (lane == cur - 1)
    hrow = lax.broadcasted_iota(I32, (heads, 1), 0) // rep
    out_lane = lax.broadcasted_iota(I32, (1, LANES), 1)
    o_c = jnp.zeros((heads, HEAD_DIM), F32)
    for g in range(groups):
        kc = mean_ref[0, g].astype(BF16)
        vc = mean_ref[0, groups + g].astype(BF16)
        s = _dot_nt(q, kc) + jnp.where(ok, 0.0, NEG_INF)
        e = jnp.exp2(s - jnp.max(s, axis=-1, keepdims=True))
        p = e / jnp.sum(e, axis=-1, keepdims=True) * jnp.where(ok, 1.0, 0.0)
        in_g = hrow == g
        o_c = o_c + jnp.where(in_g, jnp.dot(p.astype(BF16), vc, preferred_element_type=F32), 0.0)
        imp = jnp.sum(jnp.where(in_g, p, 0.0), axis=0, keepdims=True)
        imp = jnp.where(lane <= cur, jnp.where(forced, SEL_FORCE, imp), -1.0)
        imp_b = jnp.broadcast_to(imp, (nbp, nbp))
        imp_col = jnp.sum(jnp.where(diag, imp_b, 0.0), axis=1, keepdims=True)
        beats = (imp_col > imp_b) | ((imp_col == imp_b) & (ri < ci))
        cnt = jnp.sum(jnp.where(beats, 1.0, 0.0), axis=0, keepdims=True)
        sel = jnp.where((cnt < n_past_sel) & (imp >= 0.0), 1.0, 0.0)
        sel_col = jnp.sum(jnp.where(diag, jnp.broadcast_to(sel, (nbp, nbp)), 0.0), axis=1, keepdims=True)
        before = jnp.sum(jnp.where(ri < ci, sel_col, 0.0), axis=0, keepdims=True)
        row = jnp.zeros((1, LANES), F32)
        for k in range(n_past_sel):
            hit = (sel > 0.5) & (before == float(k))
            idx_k = jnp.sum(jnp.where(hit, lane.astype(F32), 0.0), axis=1, keepdims=True)
            row = jnp.where(out_lane == k, idx_k, row)
        idx_ref[0, g:g + 1, :] = row.astype(I32)
    oc_ref[0] = o_c


def _nsa_sample_select(q3, means, *, groups, rep, q_pos, n_past_sel):
    db, heads, _ = q3.shape
    nbp = means.shape[2]
    kern = functools.partial(_nsa_sample_select_kernel, nbp=nbp, groups=groups, rep=rep, q_pos=q_pos,
                             n_past_sel=n_past_sel)
    return pl.pallas_call(
        kern,
        grid=(db,),
        in_specs=[
            pl.BlockSpec((1, heads, HEAD_DIM), lambda bi: (bi, 0, 0)),
            pl.BlockSpec((1,) + means.shape[1:], lambda bi: (bi, 0, 0, 0)),
        ],
        out_specs=[
            pl.BlockSpec((1, heads, HEAD_DIM), lambda bi: (bi, 0, 0)),
            pl.BlockSpec((1, groups, LANES), lambda bi: (bi, 0, 0)),
        ],
        out_shape=[
            jax.ShapeDtypeStruct((db, heads, HEAD_DIM), F32),
            jax.ShapeDtypeStruct((db, groups, LANES), I32),
        ],
        compiler_params=_cparams(24),
        name="nsa_sample_select",
    )(q3, means)


def _nsa_sample_attend_kernel(pt_ref, si_ref, q_ref, *refs, groups, rep, n_steps, per_step, n_buf):
    kv_refs = refs[:groups * per_step]
    (newsel_ref, newwin_ref, oc_ref, win_ref, g_ref, z_ref,
     act_ref, winout_ref, m_ref, l_ref, acc_ref) = refs[groups * per_step:]
    kvw = groups * HEAD_DIM
    heads = groups * rep
    n_rows = 2 * groups
    k_id = pl.program_id(1)
    q = q_ref[0]
    qf = q.astype(F32)
    hgrp = lax.broadcasted_iota(I32, (heads, 1), 0) // rep

    def own_key_rows(n_tok):
        lane = lax.broadcasted_iota(I32, (1, n_tok * n_rows), 1)
        return lane % n_rows == hgrp, lane // n_rows

    def by_group(fn):
        out = None
        for g in range(groups):
            val = jnp.where(hgrp == g, fn(g), 0.0)
            out = val if out is None else out + val
        return out

    def new_token_scores(row_ref):
        return by_group(lambda g: jnp.sum(
            qf * row_ref[0, :, g * HEAD_DIM:(g + 1) * HEAD_DIM], axis=-1, keepdims=True))

    def new_token_values(row_ref):
        return by_group(lambda g: jnp.broadcast_to(
            row_ref[0, :, kvw + g * HEAD_DIM:kvw + (g + 1) * HEAD_DIM], (heads, HEAD_DIM)))

    @pl.when(k_id == 0)
    def _():
        m_ref[...] = jnp.full((heads, 1), NEG_INF, F32)
        l_ref[...] = jnp.zeros((heads, 1), F32)
        acc_ref[...] = jnp.zeros((heads, HEAD_DIM), F32)

    xs = [jnp.concatenate([kv_refs[g * per_step + u][...].reshape(CMP_BLOCK * n_rows, HEAD_DIM)
                           for u in range(per_step)], axis=0).astype(BF16) for g in range(groups)]
    own, _ = own_key_rows(CMP_BLOCK * per_step)
    s = by_group(lambda g: _dot_nt(q, xs[g])) + jnp.where(own, 0.0, NEG_INF)
    m_p = m_ref[...]
    m_n = jnp.maximum(m_p, jnp.max(s, axis=-1, keepdims=True))
    alpha = jnp.exp2(m_p - m_n)
    p = jnp.exp2(s - m_n)
    pb = pltpu.roll(p, groups, 1).astype(BF16)
    pv = by_group(lambda g: jnp.dot(pb, xs[g], preferred_element_type=F32))
    m_ref[...] = m_n
    l_ref[...] = alpha * l_ref[...] + jnp.sum(p, axis=-1, keepdims=True)
    acc_ref[...] = alpha * acc_ref[...] + pv

    @pl.when(k_id == n_steps - 1)
    def _():
        s_n = new_token_scores(newsel_ref)
        m_p = m_ref[...]
        m_n = jnp.maximum(m_p, s_n)
        alpha = jnp.exp2(m_p - m_n)
        p_n = jnp.exp2(s_n - m_n)
        l_s = alpha * l_ref[...] + p_n
        o_s = (alpha * acc_ref[...] + p_n * new_token_values(newsel_ref)) / l_s

        xw = win_ref[0].reshape(n_buf * n_rows, HEAD_DIM).astype(BF16)
        own_w, tok_w = own_key_rows(n_buf)
        ok_w = own_w & ((n_buf - tok_w) < WINDOW)
        s_w = _dot_nt(q, xw) + jnp.where(ok_w, 0.0, NEG_INF)
        s_wn = new_token_scores(newwin_ref)
        m_w = jnp.maximum(jnp.max(s_w, axis=-1, keepdims=True), s_wn)
        p_w = jnp.exp2(s_w - m_w)
        p_wn = jnp.exp2(s_wn - m_w)
        l_w = jnp.sum(p_w, axis=-1, keepdims=True) + p_wn
        o_w = jnp.dot(pltpu.roll(p_w, groups, 1).astype(BF16), xw, preferred_element_type=F32)
        o_w = (o_w + p_wn * new_token_values(newwin_ref)) / l_w

        winout_ref[0, 0:n_buf - 1] = win_ref[0, 1:n_buf]
        for j in range(n_rows):
            winout_ref[0, n_buf - 1, j:j + 1, :] = newwin_ref[0, :, j * HEAD_DIM:(j + 1) * HEAD_DIM]

        graw = jnp.broadcast_to(g_ref[0], (heads, groups * LANES))
        glane = lax.broadcasted_iota(I32, (heads, groups * LANES), 1)
        hidx = lax.broadcasted_iota(I32, (heads, 1), 0)
        gbase = (hidx // rep) * LANES + 3 * (hidx % rep)
        gate = lambda c: _sigmoid(jnp.sum(jnp.where(glane == gbase + c, graw, 0.0), axis=-1, keepdims=True))
        o = gate(0) * oc_ref[0] + gate(1) * o_s + gate(2) * o_w
        for h in range(heads):
            z = z_ref[0, :, h * HEAD_DIM:(h + 1) * HEAD_DIM]
            act_ref[0, :, h * HEAD_DIM:(h + 1) * HEAD_DIM] = (o[h:h + 1, :] * _silu(z)).astype(BF16)


def _nsa_sample_attend(page_table, sel_idx, q3, cache5, layer, new_sel, new_win, o_c, win_state, gates3,
                       z3, *, groups, rep, n_gather):
    db, heads, _ = q3.shape
    kvw = groups * HEAD_DIM
    n_buf = win_state.shape[2]
    n_rows = 2 * groups
    bpp = PAGE_SIZE // CMP_BLOCK
    assert sel_idx.shape[2] == n_gather
    per_step = max(u for u in range(1, 6) if n_gather % u == 0)
    n_steps = n_gather // per_step

    def gather_spec(g, u):
        def index_map(bi, k, pt, si):
            blk = si[bi, g, k * per_step + u]
            return (layer, pt[bi, blk // bpp], blk % bpp, 0, 0)
        return pl.BlockSpec((None, None, CMP_BLOCK, n_rows, HEAD_DIM), index_map)

    in_specs = [pl.BlockSpec((1, heads, HEAD_DIM), lambda bi, k, pt, si: (bi, 0, 0))]
    in_specs += [gather_spec(g, u) for g in range(groups) for u in range(per_step)]
    row3 = lambda w: pl.BlockSpec((1, 1, w), lambda bi, k, pt, si: (bi, 0, 0))
    in_specs += [
        row3(2 * kvw), row3(2 * kvw),
        pl.BlockSpec((1, heads, HEAD_DIM), lambda bi, k, pt, si: (bi, 0, 0)),
        pl.BlockSpec((None, 1, n_buf, n_rows, HEAD_DIM), lambda bi, k, pt, si: (layer, bi, 0, 0, 0)),
        row3(groups * LANES), row3(heads * HEAD_DIM),
    ]
    grid_spec = pltpu.PrefetchScalarGridSpec(
        num_scalar_prefetch=2,
        grid=(db, n_steps),
        in_specs=in_specs,
        out_specs=[
            row3(heads * HEAD_DIM),
            pl.BlockSpec((1, n_buf, n_rows, HEAD_DIM), lambda bi, k, pt, si: (bi, 0, 0, 0)),
        ],
        scratch_shapes=[pltpu.VMEM((heads, 1), F32), pltpu.VMEM((heads, 1), F32),
                        pltpu.VMEM((heads, HEAD_DIM), F32)],
    )
    kern = functools.partial(_nsa_sample_attend_kernel, groups=groups, rep=rep, n_steps=n_steps,
                             per_step=per_step, n_buf=n_buf)
    return pl.pallas_call(
        kern,
        grid_spec=grid_spec,
        out_shape=[
            jax.ShapeDtypeStruct((db, 1, heads * HEAD_DIM), BF16),
            jax.ShapeDtypeStruct((db, n_buf, n_rows, HEAD_DIM), F32),
        ],
        compiler_params=_cparams(32, ("arbitrary", "arbitrary")),
        name="nsa_sample_attend",
    )(page_table, sel_idx, q3, *([cache5] * (groups * per_step)), new_sel, new_win, o_c, win_state,
      gates3, z3)


def _rope_tables(pos):
    half = ROT_DIM // 2
    inv_freq = ROPE_THETA ** (-jnp.arange(half, dtype=F32) * (2.0 / ROT_DIM))
    ang = pos.astype(F32)[:, None] * inv_freq[None, :]
    cos, sin = jnp.cos(ang), jnp.sin(ang)
    n = pos.shape[0]
    cos_t = jnp.concatenate([cos, cos, jnp.ones((n, HEAD_DIM - ROT_DIM), F32)], axis=1)
    sin_t = jnp.concatenate([-sin, sin, jnp.zeros((n, HEAD_DIM - ROT_DIM), F32)], axis=1)
    return cos_t, sin_t


def _odd_weight_layout(w, cw, kvw, rep):
    n_layers, k, _ = w.shape
    groups = kvw // HEAD_DIM
    n_gate = 3 * groups * rep
    gates = w[:, :, :n_gate].reshape(n_layers, k, groups, 3 * rep)
    gates = jnp.pad(gates, ((0, 0), (0, 0), (0, 0), (0, LANES - 3 * rep))).reshape(n_layers, k, groups * LANES)
    return jnp.concatenate([w[:, :, n_gate:], gates], axis=2)


def kernel(x_prompt, x_sample, cache_cmp_kv, cache_sel_kv, state_win_kv, state_pool, page_table,
           norm_even, w_in_even, v_norm, w_spatial, b_spatial, w_pool, pool_scale, w_out_even,
           norm_odd, w_in_odd, q_norm, k_norm, w_out_odd):
    bsz, seq, d_model = x_prompt.shape
    db, dec_t, _ = x_sample.shape
    assert dec_t == 1
    n_even, n_odd = norm_even.shape[0], norm_odd.shape[0]
    depth = n_even + n_odd
    n_pages = page_table.shape[1]
    past_len = n_pages * PAGE_SIZE
    aw = v_norm.shape[1]
    bw = pool_scale.shape[1]
    groups = C_KV_HEADS
    kvw = groups * HEAD_DIM
    cw = w_out_odd.shape[1]
    rep = cw // kvw
    heads = cw // HEAD_DIM
    n_phys = cache_cmp_kv.shape[1]
    n_buf = state_win_kv.shape[2]
    assert past_len % CMP_BLOCK == 0 and seq % CMP_BLOCK == 0
    nb_p = seq // CMP_BLOCK
    nb_past = past_len // CMP_BLOCK
    n_past_sel = min(N_SEL, nb_past + 1) - 1
    assert nb_past >= 2 and n_past_sel >= 2

    xp = x_prompt.reshape(bsz * seq, d_model)
    xs = x_sample.reshape(db, d_model)

    cos_p, sin_p = _rope_tables(jnp.arange(seq, dtype=I32))
    cos_s, sin_s = _rope_tables(jnp.full((db,), past_len, dtype=I32))
    tables = _mask_tables(seq, LANES, WINDOW + LANES, rep)

    cache_cmp5 = cache_cmp_kv.reshape(n_odd, n_phys, PAGE_SIZE, 2 * groups, HEAD_DIM)
    cache_sel5 = cache_sel_kv.reshape(n_odd, n_phys, PAGE_SIZE, 2 * groups, HEAD_DIM)
    win_state5 = state_win_kv.reshape(n_odd, db, n_buf, 2 * groups, HEAD_DIM)

    tn_in = 2560
    n_even_in = w_in_even.shape[2]
    n_qkv = cw + 6 * kvw
    assert n_even_in % tn_in == 0 and n_qkv % tn_in == 0
    w_in_even_b = w_in_even.astype(BF16)
    w_out_even_b = w_out_even.astype(BF16)
    w_in_odd_b = w_in_odd[:, :, :n_qkv].astype(BF16)
    w_out_odd_b = w_out_odd.astype(BF16)
    w_zg_b = _odd_weight_layout(w_in_odd[:, :, n_qkv:].astype(BF16), cw, kvw, rep)
    n_zg = w_zg_b.shape[2]

    cmp_p, cmp_s, sel_p, sel_s, win_p, win_s = [], [], [], [], [], []
    pool_p, pool_s, gv_s = [], [], []
    for layer in range(depth):
        li = layer // 2
        if layer % 2 == 0:
            g_in = norm_even[li][None, :]
            vg = v_norm[li][None, :]
            ps = pool_scale[li][None, :]
            wp = w_pool[li].astype(BF16)
            proj = _norm_matmul(xp, g_in, w_in_even_b, li, n_even_in, tn=tn_in)
            y3, pool16 = _even_mix(proj.reshape(bsz, seq, -1), xp.reshape(bsz, seq, d_model), vg,
                                   w_spatial[li], b_spatial[li].T, wp, ps, w_out_even_b, li)
            xp = y3.reshape(bsz * seq, d_model)
            pool_p.append(pool16[:, 16 - POOL_PAD:])
            proj_s = _norm_matmul(xs, g_in, w_in_even_b, li, n_even_in, tn=tn_in)
            hd = aw // A_HEADS
            w00 = jnp.repeat(w_spatial[li][:, 0, 0], hd)[None, :]
            b0 = jnp.repeat(b_spatial[li][:, 0], hd)[None, :]
            act_s, vn_s, new_state = _even_mix_sample(
                proj_s, vg, w00, b0, wp, ps, jnp.swapaxes(state_pool[li], 0, 1), past_len)
            xs = _matmul_residual(act_s, w_out_even_b, li, xs)
            pool_s.append(jnp.swapaxes(new_state, 0, 1))
            gv_s.append(vn_s.reshape(db, 1, aw))
        else:
            g_in = norm_odd[li][None, :]
            qg = q_norm[li][None, :]
            kg = k_norm[li]
            proj = _norm_matmul(xp, g_in, w_in_odd_b, li, n_qkv, tn=tn_in)
            proj_zg = _norm_matmul(xp, g_in, w_zg_b, li, n_zg, tn=n_zg)
            q_b, kv_cmp, kv_sel, kv_win, kt_sel, v_sel, kt_win, v_win, means = _odd_post(
                proj, cos_p, sin_p, qg, kg, cw=cw, kvw=kvw, for_prompt=True, win_keep=min(WINDOW, seq))
            r3 = lambda a: a.reshape(bsz, seq, a.shape[-1])
            act, means_s = _nsa_prompt(r3(q_b), kt_sel, r3(v_sel), kt_win, r3(v_win),
                                       means.reshape(bsz, nb_p, 2 * kvw), r3(proj_zg), tables,
                                       cache_cmp5, page_table, li, cw=cw, z_off=0, g_off=cw)
            xp = _matmul_residual(act.reshape(bsz * seq, cw), w_out_odd_b, li, xp)
            kv6 = lambda a, n: a.reshape(n, -1, 2, groups, HEAD_DIM)
            cmp_p.append(kv6(kv_cmp, bsz))
            sel_p.append(kv6(kv_sel, bsz))
            win_p.append(kv6(kv_win, bsz))
            proj_s = _norm_matmul(xs, g_in, w_in_odd_b, li, n_qkv, tn=tn_in)
            proj_zg_s = _norm_matmul(xs, g_in, w_zg_b, li, n_zg, tn=n_zg)
            q_s, kvc_s, kvs_s, kvw_s = _odd_post(
                proj_s, cos_s, sin_s, qg, kg, cw=cw, kvw=kvw, for_prompt=False)
            q3s = q_s.reshape(db, heads, HEAD_DIM)
            o_c, idx = _nsa_sample_select(q3s, means_s, groups=groups, rep=rep, q_pos=past_len,
                                          n_past_sel=n_past_sel)
            act_s, win_new = _nsa_sample_attend(
                page_table, idx[:, :, :n_past_sel], q3s, cache_sel5, li,
                kvs_s.reshape(db, 1, 2 * kvw), kvw_s.reshape(db, 1, 2 * kvw), o_c, win_state5,
                proj_zg_s[:, cw:].reshape(db, 1, groups * LANES),
                proj_zg_s[:, :cw].reshape(db, 1, cw),
                groups=groups, rep=rep, n_gather=n_past_sel)
            xs = _matmul_residual(act_s.reshape(db, cw), w_out_odd_b, li, xs)
            cmp_s.append(kv6(kvc_s, db))
            sel_s.append(kv6(kvs_s, db))
            win_s.append(win_new.reshape(db, n_buf, 2, groups, HEAD_DIM))
    return (xp.reshape(bsz, seq, d_model), xs.reshape(db, 1, d_model),
            jnp.stack(cmp_p), jnp.stack(cmp_s), jnp.stack(sel_p), jnp.stack(sel_s),
            jnp.stack(win_p), jnp.stack(win_s), jnp.stack(pool_p), jnp.stack(pool_s), jnp.stack(gv_s))
```

```python
import functools

import jax
import jax.numpy as jnp
import numpy as np
from jax import lax
from jax.experimental import pallas as pl
from jax.experimental.pallas import tpu as pltpu

F32 = jnp.float32
BF16 = jnp.bfloat16
I32 = jnp.int32

EPS = 1e-6
PAGE_SIZE = 128
A_HEADS = 8
GMLP_CHUNK = 128
POOL_WINDOWS = (2, 4, 8, 16)
POOL_PAD = max(POOL_WINDOWS) - 1
HEAD_DIM = 128
C_KV_HEADS = 4
CMP_BLOCK = 64
N_SEL = 16
WINDOW = 512
SEL_FORCE = 1e4
NEG_INF = -1e30
ROPE_THETA = 500000.0
ROT_DIM = HEAD_DIM // 4
LOG2E = 1.4426950408889634
Q_SCALE = HEAD_DIM ** -0.5 * LOG2E

LANES = 128
SUBLANES = 8
MIB = 1024 * 1024


def _cparams(vmem_mib, semantics=None):
    return pltpu.CompilerParams(vmem_limit_bytes=int(vmem_mib * MIB), dimension_semantics=semantics)


def _silu(x):
    return x * (1.0 / (1.0 + jnp.exp(-x)))


def _sigmoid(x):
    return 1.0 / (1.0 + jnp.exp(-x))


def _dot_nt(a, b):
    return lax.dot_general(a, b, (((1,), (1,)), ((), ())), preferred_element_type=F32)


def _dot(a, b):
    return jnp.dot(a, b, preferred_element_type=F32)


def _norm_mm_kernel(x_ref, g_ref, w_ref, o_ref):
    x = x_ref[...]
    ms = jnp.mean(x * x, axis=-1, keepdims=True)
    h = (x * lax.rsqrt(ms + EPS) * g_ref[...]).astype(BF16)
    o_ref[...] = jnp.dot(h, w_ref[...], preferred_element_type=F32)


def _mm_res_kernel(a_ref, w_ref, r_ref, o_ref):
    o_ref[...] = r_ref[...] + jnp.dot(a_ref[...], w_ref[...], preferred_element_type=F32)


def _row_tile(m, want):
    return want if m % want == 0 else m


def _norm_matmul(x, g, w, layer, n, *, tn, tm=512):
    m, k = x.shape
    tm = _row_tile(m, tm)
    assert n % tn == 0 and n <= w.shape[2]
    vmem = 2 * (tm * k * 4 + k * tn * 2 + tm * tn * 4) / MIB + 8
    return pl.pallas_call(
        _norm_mm_kernel,
        grid=(n // tn, m // tm),
        in_specs=[
            pl.BlockSpec((tm, k), lambda j, i: (i, 0)),
            pl.BlockSpec((1, k), lambda j, i: (0, 0)),
            pl.BlockSpec((None, k, tn), lambda j, i: (layer, 0, j)),
        ],
        out_specs=pl.BlockSpec((tm, tn), lambda j, i: (i, j)),
        out_shape=jax.ShapeDtypeStruct((m, n), F32),
        compiler_params=_cparams(vmem),
        name="norm_matmul",
    )(x, g, w)


def _matmul_residual(a, w, layer, res, *, tm=512):
    m, k = a.shape
    n = w.shape[2]
    tm = _row_tile(m, tm)
    vmem = 2 * (tm * k * 2 + k * n * 2 + 2 * tm * n * 4) / MIB + 8
    return pl.pallas_call(
        _mm_res_kernel,
        grid=(m // tm,),
        in_specs=[
            pl.BlockSpec((tm, k), lambda i: (i, 0)),
            pl.BlockSpec((None, k, n), lambda i: (layer, 0, 0)),
            pl.BlockSpec((tm, n), lambda i: (i, 0)),
        ],
        out_specs=pl.BlockSpec((tm, n), lambda i: (i, 0)),
        out_shape=jax.ShapeDtypeStruct((m, n), F32),
        compiler_params=_cparams(vmem),
        name="matmul_residual",
    )(a, w, res)


def _even_mix_kernel(proj_ref, x_ref, vg_ref, ws_ref, bst_ref, wp_ref, ps_ref, wout_ref,
                     y_ref, pool_ref, ext_ref, act_ref, *, tm, aw, bw):
    c = pl.program_id(1)
    n_c = pl.num_programs(1)
    hd = aw // A_HEADS
    pg = bw // len(POOL_WINDOWS)

    cl = GMLP_CHUNK
    row = lax.broadcasted_iota(I32, (cl, cl), 0)
    col = lax.broadcasted_iota(I32, (cl, cl), 1)
    causal = row >= col

    for h in range(A_HEADS):
        sl = slice(h * hd, (h + 1) * hd)
        w = jnp.where(causal, ws_ref[h], 0.0).astype(BF16)
        for c0 in range(0, tm, cl):
            rs = slice(c0, c0 + cl)
            u = proj_ref[0, rs, sl]
            v = proj_ref[0, rs, aw + h * hd:aw + (h + 1) * hd]
            za = proj_ref[0, rs, 2 * aw + h * hd:2 * aw + (h + 1) * hd]
            ms = jnp.mean(v * v, axis=-1, keepdims=True)
            vn = v * lax.rsqrt(ms + EPS) * vg_ref[:, sl]
            s = jnp.dot(w, vn.astype(BF16), preferred_element_type=F32) + bst_ref[:, h:h + 1]
            act_ref[0, rs, sl] = ((u * s) * _silu(za)).astype(BF16)

    p_off = 3 * aw
    zb_off = 3 * aw + bw

    @pl.when(c == 0)
    def _():
        ext_ref[0:16, :] = jnp.zeros((16, bw), F32)

    ext_ref[16:16 + tm, :] = proj_ref[0, :, p_off:p_off + bw]
    pos = c * tm + lax.broadcasted_iota(I32, (tm, 1), 0)
    for g, wnd in enumerate(POOL_WINDOWS):
        sl = slice(g * pg, (g + 1) * pg)
        cur = ext_ref[16:16 + tm, sl]
        acc = cur
        for k in range(1, wnd):
            acc = acc + ext_ref[16 - k:16 - k + tm, sl]
        cnt = jnp.minimum(pos + 1, wnd).astype(F32)
        d = acc / cnt - cur
        y = jnp.dot(d.astype(BF16), wp_ref[g], preferred_element_type=F32)
        zb = proj_ref[0, :, zb_off + g * pg:zb_off + (g + 1) * pg]
        act_ref[0, :, aw + g * pg:aw + (g + 1) * pg] = ((y * ps_ref[:, sl]) * _silu(zb)).astype(BF16)

    tail = ext_ref[tm:tm + 16, :]
    ext_ref[0:16, :] = tail

    @pl.when(c == n_c - 1)
    def _():
        pool_ref[0] = tail

    y_ref[0] = x_ref[0] + jnp.dot(act_ref[0], wout_ref[...], preferred_element_type=F32)


def _even_mix(proj3, x3, v_gain, w_s, b_st, w_pool, pool_scale, w_out, layer):
    b, s, _ = proj3.shape
    d = x3.shape[2]
    aw = v_gain.shape[1]
    bw = pool_scale.shape[1]
    tm = 2 * GMLP_CHUNK if s % (2 * GMLP_CHUNK) == 0 else GMLP_CHUNK
    assert s % tm == 0 and s >= 16
    n_in = 3 * aw + 2 * bw
    kern = functools.partial(_even_mix_kernel, tm=tm, aw=aw, bw=bw)
    return pl.pallas_call(
        kern,
        grid=(b, s // tm),
        in_specs=[
            pl.BlockSpec((1, tm, n_in), lambda i, c: (i, c, 0)),
            pl.BlockSpec((1, tm, d), lambda i, c: (i, c, 0)),
            pl.BlockSpec((1, aw), lambda i, c: (0, 0)),
            pl.BlockSpec(w_s.shape, lambda i, c: (0, 0, 0)),
            pl.BlockSpec(b_st.shape, lambda i, c: (0, 0)),
            pl.BlockSpec(w_pool.shape, lambda i, c: (0, 0, 0)),
            pl.BlockSpec((1, bw), lambda i, c: (0, 0)),
            pl.BlockSpec((None, aw + bw, d), lambda i, c: (layer, 0, 0)),
        ],
        out_specs=[
            pl.BlockSpec((1, tm, d), lambda i, c: (i, c, 0)),
            pl.BlockSpec((1, 16, bw), lambda i, c: (i, 0, 0)),
        ],
        out_shape=[
            jax.ShapeDtypeStruct((b, s, d), F32),
            jax.ShapeDtypeStruct((b, 16, bw), F32),
        ],
        scratch_shapes=[pltpu.VMEM((16 + tm, bw), F32), pltpu.VMEM((1, tm, aw + bw), BF16)],
        compiler_params=_cparams(48, ("arbitrary", "arbitrary")),
        name="even_mix",
    )(proj3, x3, v_gain, w_s, b_st, w_pool, pool_scale, w_out)


def _even_mix_sample_kernel(proj_ref, vg_ref, w00_ref, b0_ref, wp_ref, ps_ref, st_ref,
                            act_ref, vn_ref, newst_ref, *, aw, bw, pos):
    hd = aw // A_HEADS
    pg = bw // len(POOL_WINDOWS)
    for h in range(A_HEADS):
        sl = slice(h * hd, (h + 1) * hd)
        u = proj_ref[:, sl]
        v = proj_ref[:, aw + h * hd:aw + (h + 1) * hd]
        za = proj_ref[:, 2 * aw + h * hd:2 * aw + (h + 1) * hd]
        ms = jnp.mean(v * v, axis=-1, keepdims=True)
        vn = v * lax.rsqrt(ms + EPS) * vg_ref[:, sl]
        vn_ref[:, sl] = vn
        s = w00_ref[:, sl] * vn + b0_ref[:, sl]
        act_ref[:, sl] = ((u * s) * _silu(za)).astype(BF16)

    p = proj_ref[:, 3 * aw:3 * aw + bw]
    for g, wnd in enumerate(POOL_WINDOWS):
        sl = slice(g * pg, (g + 1) * pg)
        cur = p[:, sl]
        acc = cur
        for k in range(1, wnd):
            acc = acc + st_ref[POOL_PAD - k, :, sl]
        cnt = float(min(pos + 1, wnd))
        d = acc / cnt - cur
        y = jnp.dot(d.astype(BF16), wp_ref[g], preferred_element_type=F32)
        zb = proj_ref[:, 3 * aw + bw + g * pg:3 * aw + bw + (g + 1) * pg]
        act_ref[:, aw + g * pg:aw + (g + 1) * pg] = ((y * ps_ref[:, sl]) * _silu(zb)).astype(BF16)

    for k in range(POOL_PAD - 1):
        newst_ref[k] = st_ref[k + 1]
    newst_ref[POOL_PAD - 1] = p


def _even_mix_sample(proj, v_gain, w00, b0, w_pool, pool_scale, state_t, pos):
    db = proj.shape[0]
    aw = v_gain.shape[1]
    bw = pool_scale.shape[1]
    kern = functools.partial(_even_mix_sample_kernel, aw=aw, bw=bw, pos=pos)
    return pl.pallas_call(
        kern,
        out_shape=[
            jax.ShapeDtypeStruct((db, aw + bw), BF16),
            jax.ShapeDtypeStruct((db, aw), F32),
            jax.ShapeDtypeStruct((POOL_PAD, db, bw), F32),
        ],
        name="even_mix_sample",
    )(proj, v_gain, w00, b0, w_pool, pool_scale, state_t)


def _odd_post_kernel(proj_ref, cos_ref, sin_ref, qg_ref, kg_ref, *out_refs, tm, cw, kvw, for_prompt):
    q_out, cmp_out, sel_out, win_out = out_refs[:4]
    if for_prompt:
        selkt_out, selv_out, winkt_out, winv_out, means_out = out_refs[4:]
        attn_outs = (None, (selkt_out, selv_out), (winkt_out, winv_out))
    else:
        attn_outs = (None, None, None)
    cosf = cos_ref[...]
    sinf = sin_ref[...]
    half = ROT_DIM // 2
    first = lax.broadcasted_iota(I32, (tm, HEAD_DIM), 1) < half

    def norm_rope(x, gain):
        ms = jnp.mean(x * x, axis=-1, keepdims=True)
        y = x * lax.rsqrt(ms + EPS) * gain
        rot = jnp.where(first, pltpu.roll(y, HEAD_DIM - half, 1), pltpu.roll(y, half, 1))
        return y * cosf + rot * sinf

    qg = qg_ref[...]
    for h in range(cw // HEAD_DIM):
        sl = slice(h * HEAD_DIM, (h + 1) * HEAD_DIM)
        q_out[:, sl] = (norm_rope(proj_ref[:, sl], qg) * Q_SCALE).astype(BF16)

    for br, (o32, attn) in enumerate(zip((cmp_out, sel_out, win_out), attn_outs)):
        k_off = cw + br * 2 * kvw
        v_off = k_off + kvw
        kg = kg_ref[br:br + 1, :]
        pieces = []
        for g in range(kvw // HEAD_DIM):
            sl = slice(g * HEAD_DIM, (g + 1) * HEAD_DIM)
            kk = norm_rope(proj_ref[:, k_off + g * HEAD_DIM:k_off + (g + 1) * HEAD_DIM], kg)
            pieces.append(kk)
            if attn is not None:
                attn[0][sl, :] = kk.T.astype(BF16)
        vv = proj_ref[:, v_off:v_off + kvw]
        if attn is not None:
            attn[1][...] = vv.astype(BF16)
        row = jnp.concatenate(pieces + [vv], axis=1)
        o32[...] = row.reshape(o32.shape)
        if br == 0 and for_prompt:
            for j in range(tm // CMP_BLOCK):
                blk = row[j * CMP_BLOCK:(j + 1) * CMP_BLOCK, :]
                means_out[0, j:j + 1, :] = jnp.mean(blk, axis=0, keepdims=True)


def _odd_post(proj, cos_t, sin_t, q_gain, k_gain, *, cw, kvw, for_prompt, win_keep=0, tm=512):
    m = proj.shape[0]
    tm = _row_tile(m, tm)
    n_pos_tiles = cos_t.shape[0] // tm
    n_used = cw + 6 * kvw
    kern = functools.partial(_odd_post_kernel, tm=tm, cw=cw, kvw=kvw, for_prompt=for_prompt)
    row_spec = lambda w: pl.BlockSpec((tm, w), lambda i: (i, 0))
    n_rows = 2 * kvw // HEAD_DIM
    if for_prompt:
        assert win_keep % tm == 0 and cos_t.shape[0] >= win_keep
        keep_tiles = win_keep // tm
        skip_tiles = n_pos_tiles - keep_tiles
        kv_shape = jax.ShapeDtypeStruct((m, n_rows, HEAD_DIM), F32)
        kv_spec = pl.BlockSpec((tm, n_rows, HEAD_DIM), lambda i: (i, 0, 0))
        win_shape = jax.ShapeDtypeStruct((m // n_pos_tiles * keep_tiles, n_rows, HEAD_DIM), F32)
        win_spec = pl.BlockSpec(
            (tm, n_rows, HEAD_DIM),
            lambda i: (i // n_pos_tiles * keep_tiles + jnp.maximum(i % n_pos_tiles - skip_tiles, 0), 0, 0))
    else:
        kv_shape = win_shape = jax.ShapeDtypeStruct((m, 2 * kvw), F32)
        kv_spec = win_spec = row_spec(2 * kvw)
    out_shape = [jax.ShapeDtypeStruct((m, cw), BF16), kv_shape, kv_shape, win_shape]
    out_specs = [row_spec(cw), kv_spec, kv_spec, win_spec]
    if for_prompt:
        assert tm % CMP_BLOCK == 0 and tm % LANES == 0
        for _ in range(2):
            out_shape += [jax.ShapeDtypeStruct((kvw, m), BF16), jax.ShapeDtypeStruct((m, kvw), BF16)]
            out_specs += [pl.BlockSpec((kvw, tm), lambda i: (0, i)), row_spec(kvw)]
        out_shape.append(jax.ShapeDtypeStruct((m // tm, tm // CMP_BLOCK, 2 * kvw), F32))
        out_specs.append(pl.BlockSpec((1, tm // CMP_BLOCK, 2 * kvw), lambda i: (i, 0, 0)))
    return pl.pallas_call(
        kern,
        grid=(m // tm,),
        in_specs=[
            pl.BlockSpec((tm, n_used), lambda i: (i, 0)),
            pl.BlockSpec((tm, HEAD_DIM), lambda i: (i % n_pos_tiles, 0)),
            pl.BlockSpec((tm, HEAD_DIM), lambda i: (i % n_pos_tiles, 0)),
            pl.BlockSpec((1, HEAD_DIM), lambda i: (0, 0)),
            pl.BlockSpec((3, HEAD_DIM), lambda i: (0, 0)),
        ],
        out_specs=out_specs,
        out_shape=out_shape,
        compiler_params=_cparams(56),
        name="odd_post",
    )(proj, cos_t, sin_t, q_gain, k_gain)


def _nsa_sub_tile(i, h, q_ref, ks_ref, vs_ref, kw_ref, vw_ref, kc, vc, band_ref, tri_ref, eye_ref,
                  *, tq, nb, rep, lw):
    rows = rep * tq
    r0 = h * tq
    q = jnp.concatenate([q_ref[0, r0:r0 + tq, r * HEAD_DIM:(r + 1) * HEAD_DIM] for r in range(rep)],
                        axis=0)
    t_col = i * tq + lax.broadcasted_iota(I32, (tq, 1), 0)
    t_row = i * tq + lax.broadcasted_iota(I32, (1, tq), 1)

    blk_row = lax.broadcasted_iota(I32, (1, nb), 1)
    s_c = _dot_nt(q, kc).reshape(rep, tq, nb)
    ok_c = ((blk_row + 1) * CMP_BLOCK - 1) <= t_col
    s_c = s_c + jnp.where(ok_c, 0.0, NEG_INF)[None]
    m_c = jnp.max(s_c, axis=-1, keepdims=True)
    e_c = jnp.exp2(s_c - m_c)
    p_c = e_c / jnp.sum(e_c, axis=-1, keepdims=True) * jnp.where(ok_c, 1.0, 0.0)[None]
    o_c = jnp.dot(p_c.reshape(rows, nb).astype(BF16), vc, preferred_element_type=F32)

    blk_col = lax.broadcasted_iota(I32, (nb, 1), 0)
    s_t = _dot_nt(kc, q)
    ok_t = ((blk_col + 1) * CMP_BLOCK - 1) <= t_row
    bias_t = jnp.where(ok_t, 0.0, NEG_INF)
    okf_t = jnp.where(ok_t, 1.0, 0.0)
    imp = jnp.zeros((nb, tq), F32)
    for r in range(rep):
        s_r = s_t[:, r * tq:(r + 1) * tq] + bias_t
        e_r = jnp.exp2(s_r - jnp.max(s_r, axis=0, keepdims=True))
        imp = imp + e_r / jnp.sum(e_r, axis=0, keepdims=True) * okf_t
    cur = t_row // CMP_BLOCK
    forced = (blk_col == 0) | (blk_col == cur) | (blk_col == cur - 1)
    imp = jnp.where(blk_col <= cur, jnp.where(forced, SEL_FORCE, imp), -1.0)

    blk8 = lax.broadcasted_iota(I32, (SUBLANES, 1), 0)
    cnt = jnp.zeros((nb, tq), F32)
    for j in range(nb):
        rj = imp[j:j + 1, :]
        parts = []
        for v0 in range(0, nb, SUBLANES):
            sub = imp[v0:v0 + SUBLANES, :]
            if v0 > j:
                beats = rj >= sub
            elif v0 + SUBLANES - 1 <= j:
                beats = rj > sub
            else:
                beats = (rj > sub) | ((rj == sub) & (blk8 > j - v0))
            parts.append(jnp.where(beats, 1.0, 0.0))
        cnt = cnt + jnp.concatenate(parts, axis=0)
    keep = (cnt < min(N_SEL, nb)) & (imp >= 0.0) & (blk_col < i * (tq // CMP_BLOCK))
    drop_t = jnp.where(keep, 0.0, 1.0)
    if LANES > nb:
        drop_t = jnp.concatenate([drop_t, jnp.zeros((LANES - nb, tq), F32)], axis=0)
    drop = drop_t.T.astype(BF16)
    q_drop = jnp.concatenate([q, jnp.concatenate([drop] * rep, axis=0)], axis=1)

    q_eye = jnp.concatenate([q, eye_ref[...]], axis=1)

    w0 = pl.multiple_of(jnp.maximum(i * tq + tq - lw, 0), tq)
    band = band_ref[jnp.minimum(i, band_ref.shape[0] - 1)]
    k_w = jnp.concatenate([kw_ref[:, pl.ds(w0, lw)], band], axis=0)
    v_w = jnp.concatenate([vw_ref[0, pl.ds(w0, lw), :], jnp.ones((lw, HEAD_DIM), BF16)], axis=1)
    s_w = _dot(q_eye, k_w)
    p_w = jnp.exp2(s_w - jnp.max(s_w, axis=-1, keepdims=True))
    acc_w = _dot(p_w.astype(BF16), v_w)
    o_w = acc_w[:, :HEAD_DIM] / acc_w[:, HEAD_DIM:]

    d0 = pl.multiple_of(i * tq, tq)
    k_d = jnp.concatenate([ks_ref[:, pl.ds(d0, tq)], tri_ref[...]], axis=0)
    v_d = jnp.concatenate([vs_ref[0, pl.ds(d0, tq), :], jnp.ones((tq, HEAD_DIM), BF16)], axis=1)
    s_d = _dot(q_eye, k_d)
    m_d = jnp.max(s_d, axis=-1, keepdims=True)
    acc_d = _dot(jnp.exp2(s_d - m_d).astype(BF16), v_d)
    return q_drop, o_c, o_w, m_d, acc_d


def _nsa_prompt_kernel(pt_ref, q_ref, ks_ref, vs_ref, kw_ref, vw_ref, kc_ref, vc_ref, g_ref, z_ref,
                       eb_ref, band_ref, tri_ref, eye_ref, *rest, tq, nh, tk, seq, nb, rep, lw, pps):
    if pps:
        o_ref, pm_ref, s0_ref, s1_ref = rest[pps:]
        _store_page_means(rest[:pps], pm_ref)
    else:
        o_ref, s0_ref, s1_ref = rest
    step = pl.program_id(2)
    kc = kc_ref[0].astype(BF16)
    vc = vc_ref[0].astype(BF16)
    subs = [_nsa_sub_tile(step * nh + h, h, q_ref, ks_ref, vs_ref, kw_ref, vw_ref, kc, vc, band_ref,
                          tri_ref, eye_ref, tq=tq, nb=nb, rep=rep, lw=lw) for h in range(nh)]

    q_drop = jnp.concatenate([s[0] for s in subs], axis=0)
    m_0 = jnp.concatenate([s[3] for s in subs], axis=0)
    acc_0 = jnp.concatenate([s[4] for s in subs], axis=0)

    partial, gate_s, z_act = [], [], []
    for h in range(nh):
        ts = slice(h * tq, (h + 1) * tq)
        gs = _sigmoid(g_ref[0, ts, :])
        o_c, o_w = subs[h][1], subs[h][2]
        for r in range(rep):
            rs = slice(r * tq, (r + 1) * tq)
            partial.append(gs[:, 3 * r:3 * r + 1] * o_c[rs] + gs[:, 3 * r + 2:3 * r + 3] * o_w[rs])
            gate_s.append(gs[:, 3 * r + 1:3 * r + 2])
            z_act.append(_silu(z_ref[0, ts, r * HEAD_DIM:(r + 1) * HEAD_DIM]))

    ones_k = jnp.ones((tk, HEAD_DIM), BF16)
    n_tiles = seq // tk

    def tile_scores(t):
        k0 = pl.multiple_of(t * tk, tk)
        k = jnp.concatenate([ks_ref[:, pl.ds(k0, tk)], eb_ref[:, pl.ds(k0, tk)]], axis=0)
        return _dot(q_drop, k)

    def accumulate(carry, s, t):
        m_p, acc = carry
        k0 = pl.multiple_of(t * tk, tk)
        v = jnp.concatenate([vs_ref[0, pl.ds(k0, tk), :], ones_k], axis=1)
        m_n = jnp.maximum(m_p, jnp.max(s, axis=-1, keepdims=True))
        pv = _dot(jnp.exp2(s - m_n).astype(BF16), v)
        return m_n, jnp.exp2(m_p - m_n) * acc + pv

    def sel_body(j, carry):
        t1 = 2 * j + 1
        s1_ref[...] = tile_scores(t1)
        carry = accumulate(carry, s0_ref[...], 2 * j)
        s0_ref[...] = tile_scores(jnp.minimum(t1 + 1, n_tiles - 1))
        return accumulate(carry, s1_ref[...], t1)

    n_past = ((step * nh + nh - 1) * tq + tk - 1) // tk
    s0_ref[...] = tile_scores(0)
    _, acc_s = lax.fori_loop(0, (n_past + 1) // 2, sel_body, (m_0, acc_0))
    o_s = acc_s[:, :HEAD_DIM] / acc_s[:, HEAD_DIM:]

    for h in range(nh):
        for r in range(rep):
            n = h * rep + r
            o = partial[n] + gate_s[n] * o_s[n * tq:(n + 1) * tq]
            o_ref[0, h * tq:(h + 1) * tq, r * HEAD_DIM:(r + 1) * HEAD_DIM] = (o * z_act[n]).astype(BF16)


def _mask_tables(seq, tq, lw, rep):
    neg = lambda ok: jnp.where(ok, 0.0, NEG_INF).astype(BF16)
    row = jnp.arange(LANES, dtype=I32)[:, None]
    key = jnp.arange(seq, dtype=I32)[None, :]
    block_rows = neg(key // CMP_BLOCK != row)
    t = jnp.arange(tq, dtype=I32)[None, :, None]
    c = jnp.arange(lw, dtype=I32)[None, None, :]
    off = jnp.arange(WINDOW // tq + 1, dtype=I32)[:, None, None] * tq
    rel = off + t - c
    band = neg((rel >= 0) & (rel < WINDOW))
    tri = neg(jnp.arange(tq, dtype=I32)[None, :] <= jnp.arange(tq, dtype=I32)[:, None])
    eye = jnp.tile(jnp.eye(tq, dtype=BF16), (rep, 1))
    return block_rows, band, tri, eye


def _page_job_split(n_steps, page_table):
    db, n_pages = page_table.shape
    total = db * n_pages
    if total % n_steps:
        return 0
    pps = total // n_steps
    return pps if n_pages % pps == 0 else 0


def _nsa_prompt(q3, kt_sel, v_sel, kt_win, v_win, means3, proj3, tables, cache5, page_table, layer, *,
                cw, z_off, g_off, tq=LANES, nh=2, tk=512):
    b, s, _ = q3.shape
    kvw = v_sel.shape[2]
    groups = kvw // HEAD_DIM
    rep = cw // kvw
    nb = means3.shape[1]
    gw = rep * HEAD_DIM
    block_rows, band, tri, eye = tables
    tk = min(tk, s)
    lw = band.shape[2]
    ts = nh * tq
    assert tq == LANES and nb <= LANES and nb % SUBLANES == 0 and lw == WINDOW + tq and s >= lw
    assert s % ts == 0 and s % (2 * tk) == 0 and tk % tq == 0 and tq % CMP_BLOCK == 0
    assert nb * CMP_BLOCK == s
    assert z_off % gw == 0 and g_off % LANES == 0
    n_i = s // ts
    pps = _page_job_split(b * groups * n_i, page_table)
    kern = functools.partial(_nsa_prompt_kernel, tq=tq, nh=nh, tk=tk, seq=s, nb=nb, rep=rep, lw=lw,
                             pps=pps)
    db, n_pages = page_table.shape
    n_rows, bpp = cache5.shape[3], PAGE_SIZE // CMP_BLOCK
    spp = n_pages // pps if pps else 0
    assert b * groups * n_i * pps in (0, db * n_pages)
    flat = lambda bi, g, i: (bi * groups + g) * n_i + i

    def page_spec(r):
        def index_map(bi, g, i, pt):
            f = flat(bi, g, i)
            return (layer, pt[f // spp, (f % spp) * pps + r], 0, 0, 0)
        return pl.BlockSpec((None, None, PAGE_SIZE, n_rows, HEAD_DIM), index_map)

    kt_spec = pl.BlockSpec((HEAD_DIM, s), lambda bi, g, i, pt: (g, bi))
    v_spec = pl.BlockSpec((1, s, HEAD_DIM), lambda bi, g, i, pt: (bi, 0, g))
    mean_spec = lambda off: pl.BlockSpec((1, nb, HEAD_DIM), lambda bi, g, i, pt: (bi, 0, off + g))
    grid_spec = pltpu.PrefetchScalarGridSpec(
        num_scalar_prefetch=1,
        grid=(b, groups, n_i),
        in_specs=[
            pl.BlockSpec((1, ts, gw), lambda bi, g, i, pt: (bi, i, g)),
            kt_spec, v_spec, kt_spec, v_spec,
            mean_spec(0), mean_spec(groups),
            pl.BlockSpec((1, ts, LANES), lambda bi, g, i, pt: (bi, i, g_off // LANES + g)),
            pl.BlockSpec((1, ts, gw), lambda bi, g, i, pt: (bi, i, z_off // gw + g)),
            pl.BlockSpec(block_rows.shape, lambda bi, g, i, pt: (0, 0)),
            pl.BlockSpec(band.shape, lambda bi, g, i, pt: (0, 0, 0)),
            pl.BlockSpec(tri.shape, lambda bi, g, i, pt: (0, 0)),
            pl.BlockSpec(eye.shape, lambda bi, g, i, pt: (0, 0)),
        ] + [page_spec(r) for r in range(pps)],
        out_specs=[pl.BlockSpec((1, ts, gw), lambda bi, g, i, pt: (bi, i, g))] + ([
            pl.BlockSpec((1, n_rows, pps * bpp, HEAD_DIM),
                         lambda bi, g, i, pt: (flat(bi, g, i) // spp, 0, flat(bi, g, i) % spp, 0)),
        ] if pps else []),
        scratch_shapes=[pltpu.VMEM((nh * rep * tq, tk), F32), pltpu.VMEM((nh * rep * tq, tk), F32)],
    )
    outs = pl.pallas_call(
        kern,
        grid_spec=grid_spec,
        out_shape=[jax.ShapeDtypeStruct((b, s, cw), BF16)] + ([
            jax.ShapeDtypeStruct((db, n_rows, n_pages * bpp, HEAD_DIM), F32)] if pps else []),
        compiler_params=_cparams(56),
        name="nsa_prompt",
    )(page_table, q3, kt_sel, v_sel, kt_win, v_win, means3, means3, proj3, proj3, block_rows, band, tri,
      eye, *([cache5] * pps))
    return outs if pps else (outs[0], _page_means(cache5, page_table, layer))


def _store_page_means(page_refs, out_ref):
    bpp = PAGE_SIZE // CMP_BLOCK
    n_rows = out_ref.shape[1]
    for r, page_ref in enumerate(page_refs):
        for h in range(bpp):
            mean = jnp.mean(page_ref[h * CMP_BLOCK:(h + 1) * CMP_BLOCK], axis=0)
            n = r * bpp + h
            for j in range(n_rows):
                out_ref[0, j, n:n + 1, :] = mean[j:j + 1, :]


def _page_means_kernel(pt_ref, *refs, pps):
    _store_page_means(refs[:pps], refs[pps])


def _page_means(cache5, page_table, layer, *, pps=8):
    db, n_pages = page_table.shape
    n_rows, d = cache5.shape[3], cache5.shape[4]
    pps = pps if n_pages % pps == 0 else 1
    bpp = PAGE_SIZE // CMP_BLOCK

    def page_spec(r):
        return pl.BlockSpec((None, None, PAGE_SIZE, n_rows, d),
                            lambda bi, j, pt: (layer, pt[bi, j * pps + r], 0, 0, 0))

    grid_spec = pltpu.PrefetchScalarGridSpec(
        num_scalar_prefetch=1,
        grid=(db, n_pages // pps),
        in_specs=[page_spec(r) for r in range(pps)],
        out_specs=pl.BlockSpec((1, n_rows, pps * bpp, d), lambda bi, j, pt: (bi, 0, j, 0)),
    )
    return pl.pallas_call(
        functools.partial(_page_means_kernel, pps=pps),
        grid_spec=grid_spec,
        out_shape=jax.ShapeDtypeStruct((db, n_rows, n_pages * bpp, d), F32),
        compiler_params=_cparams(24),
        name="page_means",
    )(page_table, *([cache5] * pps))


def _nsa_sample_select_kernel(q_ref, mean_ref, oc_ref, idx_ref, *, nbp, groups, rep, q_pos, n_past_sel):
    q = q_ref[0]
    heads = groups * rep
    lane = lax.broadcasted_iota(I32, (1, nbp), 1)
    ri = lax.broadcasted_iota(I32, (nbp, nbp), 0)
    ci = lax.broadcasted_iota(I32, (nbp, nbp), 1)
    diag = ri == ci
    ok = ((lane + 1) * CMP_BLOCK - 1) <= q_pos
    cur = q_pos // CMP_BLOCK
    forced = (lane == 0) | (lane == cur) | (lane == cur - 1)
    hrow = lax.broadcasted_iota(I32, (heads, 1), 0) // rep
    out_lane = lax.broadcasted_iota(I32, (1, LANES), 1)
    o_c = jnp.zeros((heads, HEAD_DIM), F32)
    for g in range(groups):
        kc = mean_ref[0, g].astype(BF16)
        vc = mean_ref[0, groups + g].astype(BF16)
        s = _dot_nt(q, kc) + jnp.where(ok, 0.0, NEG_INF)
        e = jnp.exp2(s - jnp.max(s, axis=-1, keepdims=True))
        p = e / jnp.sum(e, axis=-1, keepdims=True) * jnp.where(ok, 1.0, 0.0)
        in_g = hrow == g
        o_c = o_c + jnp.where(in_g, jnp.dot(p.astype(BF16), vc, preferred_element_type=F32), 0.0)
        imp = jnp.sum(jnp.where(in_g, p, 0.0), axis=0, keepdims=True)
        imp = jnp.where(lane <= cur, jnp.where(forced, SEL_FORCE, imp), -1.0)
        imp_b = jnp.broadcast_to(imp, (nbp, nbp))
        imp_col = jnp.sum(jnp.where(diag, imp_b, 0.0), axis=1, keepdims=True)
        beats = (imp_col > imp_b) | ((imp_col == imp_b) & (ri < ci))
        cnt = jnp.sum(jnp.where(beats, 1.0, 0.0), axis=0, keepdims=True)
        sel = jnp.where((cnt < n_past_sel) & (imp >= 0.0), 1.0, 0.0)
        sel_col = jnp.sum(jnp.where(diag, jnp.broadcast_to(sel, (nbp, nbp)), 0.0), axis=1, keepdims=True)
        before = jnp.sum(jnp.where(ri < ci, sel_col, 0.0), axis=0, keepdims=True)
        row = jnp.zeros((1, LANES), F32)
        for k in range(n_past_sel):
            hit = (sel > 0.5) & (before == float(k))
            idx_k = jnp.sum(jnp.where(hit, lane.astype(F32), 0.0), axis=1, keepdims=True)
            row = jnp.where(out_lane == k, idx_k, row)
        idx_ref[0, g:g + 1, :] = row.astype(I32)
    oc_ref[0] = o_c


def _nsa_sample_select(q3, means, *, groups, rep, q_pos, n_past_sel):
    db, heads, _ = q3.shape
    nbp = means.shape[2]
    kern = functools.partial(_nsa_sample_select_kernel, nbp=nbp, groups=groups, rep=rep, q_pos=q_pos,
                             n_past_sel=n_past_sel)
    return pl.pallas_call(
        kern,
        grid=(db,),
        in_specs=[
            pl.BlockSpec((1, heads, HEAD_DIM), lambda bi: (bi, 0, 0)),
            pl.BlockSpec((1,) + means.shape[1:], lambda bi: (bi, 0, 0, 0)),
        ],
        out_specs=[
            pl.BlockSpec((1, heads, HEAD_DIM), lambda bi: (bi, 0, 0)),
            pl.BlockSpec((1, groups, LANES), lambda bi: (bi, 0, 0)),
        ],
        out_shape=[
            jax.ShapeDtypeStruct((db, heads, HEAD_DIM), F32),
            jax.ShapeDtypeStruct((db, groups, LANES), I32),
        ],
        compiler_params=_cparams(24),
        name="nsa_sample_select",
    )(q3, means)


def _nsa_sample_attend_kernel(pt_ref, si_ref, q_ref, *refs, groups, rep, n_steps, per_step, n_buf):
    kv_refs = refs[:groups * per_step]
    (newsel_ref, newwin_ref, oc_ref, win_ref, g_ref, z_ref,
     act_ref, winout_ref, m_ref, l_ref, acc_ref) = refs[groups * per_step:]
    kvw = groups * HEAD_DIM
    heads = groups * rep
    n_rows = 2 * groups
    k_id = pl.program_id(1)
    q = q_ref[0]
    qf = q.astype(F32)
    hgrp = lax.broadcasted_iota(I32, (heads, 1), 0) // rep

    def own_key_rows(n_tok):
        lane = lax.broadcasted_iota(I32, (1, n_tok * n_rows), 1)
        return lane % n_rows == hgrp, lane // n_rows

    def by_group(fn):
        out = None
        for g in range(groups):
            val = jnp.where(hgrp == g, fn(g), 0.0)
            out = val if out is None else out + val
        return out

    def new_token_scores(row_ref):
        return by_group(lambda g: jnp.sum(
            qf * row_ref[0, :, g * HEAD_DIM:(g + 1) * HEAD_DIM], axis=-1, keepdims=True))

    def new_token_values(row_ref):
        return by_group(lambda g: jnp.broadcast_to(
            row_ref[0, :, kvw + g * HEAD_DIM:kvw + (g + 1) * HEAD_DIM], (heads, HEAD_DIM)))

    @pl.when(k_id == 0)
    def _():
        m_ref[...] = jnp.full((heads, 1), NEG_INF, F32)
        l_ref[...] = jnp.zeros((heads, 1), F32)
        acc_ref[...] = jnp.zeros((heads, HEAD_DIM), F32)

    xs = [jnp.concatenate([kv_refs[g * per_step + u][...].reshape(CMP_BLOCK * n_rows, HEAD_DIM)
                           for u in range(per_step)], axis=0).astype(BF16) for g in range(groups)]
    own, _ = own_key_rows(CMP_BLOCK * per_step)
    s = by_group(lambda g: _dot_nt(q, xs[g])) + jnp.where(own, 0.0, NEG_INF)
    m_p = m_ref[...]
    m_n = jnp.maximum(m_p, jnp.max(s, axis=-1, keepdims=True))
    alpha = jnp.exp2(m_p - m_n)
    p = jnp.exp2(s - m_n)
    pb = pltpu.roll(p, groups, 1).astype(BF16)
    pv = by_group(lambda g: jnp.dot(pb, xs[g], preferred_element_type=F32))
    m_ref[...] = m_n
    l_ref[...] = alpha * l_ref[...] + jnp.sum(p, axis=-1, keepdims=True)
    acc_ref[...] = alpha * acc_ref[...] + pv

    @pl.when(k_id == n_steps - 1)
    def _():
        s_n = new_token_scores(newsel_ref)
        m_p = m_ref[...]
        m_n = jnp.maximum(m_p, s_n)
        alpha = jnp.exp2(m_p - m_n)
        p_n = jnp.exp2(s_n - m_n)
        l_s = alpha * l_ref[...] + p_n
        o_s = (alpha * acc_ref[...] + p_n * new_token_values(newsel_ref)) / l_s

        xw = win_ref[0].reshape(n_buf * n_rows, HEAD_DIM).astype(BF16)
        own_w, tok_w = own_key_rows(n_buf)
        ok_w = own_w & ((n_buf - tok_w) < WINDOW)
        s_w = _dot_nt(q, xw) + jnp.where(ok_w, 0.0, NEG_INF)
        s_wn = new_token_scores(newwin_ref)
        m_w = jnp.maximum(jnp.max(s_w, axis=-1, keepdims=True), s_wn)
        p_w = jnp.exp2(s_w - m_w)
        p_wn = jnp.exp2(s_wn - m_w)
        l_w = jnp.sum(p_w, axis=-1, keepdims=True) + p_wn
        o_w = jnp.dot(pltpu.roll(p_w, groups, 1).astype(BF16), xw, preferred_element_type=F32)
        o_w = (o_w + p_wn * new_token_values(newwin_ref)) / l_w

        winout_ref[0, 0:n_buf - 1] = win_ref[0, 1:n_buf]
        for j in range(n_rows):
            winout_ref[0, n_buf - 1, j:j + 1, :] = newwin_ref[0, :, j * HEAD_DIM:(j + 1) * HEAD_DIM]

        graw = jnp.broadcast_to(g_ref[0], (heads, groups * LANES))
        glane = lax.broadcasted_iota(I32, (heads, groups * LANES), 1)
        hidx = lax.broadcasted_iota(I32, (heads, 1), 0)
        gbase = (hidx // rep) * LANES + 3 * (hidx % rep)
        gate = lambda c: _sigmoid(jnp.sum(jnp.where(glane == gbase + c, graw, 0.0), axis=-1, keepdims=True))
        o = gate(0) * oc_ref[0] + gate(1) * o_s + gate(2) * o_w
        for h in range(heads):
            z = z_ref[0, :, h * HEAD_DIM:(h + 1) * HEAD_DIM]
            act_ref[0, :, h * HEAD_DIM:(h + 1) * HEAD_DIM] = (o[h:h + 1, :] * _silu(z)).astype(BF16)


def _nsa_sample_attend(page_table, sel_idx, q3, cache5, layer, new_sel, new_win, o_c, win_state, gates3,
                       z3, *, groups, rep, n_gather):
    db, heads, _ = q3.shape
    kvw = groups * HEAD_DIM
    n_buf = win_state.shape[2]
    n_rows = 2 * groups
    bpp = PAGE_SIZE // CMP_BLOCK
    assert sel_idx.shape[2] == n_gather
    per_step = max(u for u in range(1, 6) if n_gather % u == 0)
    n_steps = n_gather // per_step

    def gather_spec(g, u):
        def index_map(bi, k, pt, si):
            blk = si[bi, g, k * per_step + u]
            return (layer, pt[bi, blk // bpp], blk % bpp, 0, 0)
        return pl.BlockSpec((None, None, CMP_BLOCK, n_rows, HEAD_DIM), index_map)

    in_specs = [pl.BlockSpec((1, heads, HEAD_DIM), lambda bi, k, pt, si: (bi, 0, 0))]
    in_specs += [gather_spec(g, u) for g in range(groups) for u in range(per_step)]
    row3 = lambda w: pl.BlockSpec((1, 1, w), lambda bi, k, pt, si: (bi, 0, 0))
    in_specs += [
        row3(2 * kvw), row3(2 * kvw),
        pl.BlockSpec((1, heads, HEAD_DIM), lambda bi, k, pt, si: (bi, 0, 0)),
        pl.BlockSpec((None, 1, n_buf, n_rows, HEAD_DIM), lambda bi, k, pt, si: (layer, bi, 0, 0, 0)),
        row3(groups * LANES), row3(heads * HEAD_DIM),
    ]
    grid_spec = pltpu.PrefetchScalarGridSpec(
        num_scalar_prefetch=2,
        grid=(db, n_steps),
        in_specs=in_specs,
        out_specs=[
            row3(heads * HEAD_DIM),
            pl.BlockSpec((1, n_buf, n_rows, HEAD_DIM), lambda bi, k, pt, si: (bi, 0, 0, 0)),
        ],
        scratch_shapes=[pltpu.VMEM((heads, 1), F32), pltpu.VMEM((heads, 1), F32),
                        pltpu.VMEM((heads, HEAD_DIM), F32)],
    )
    kern = functools.partial(_nsa_sample_attend_kernel, groups=groups, rep=rep, n_steps=n_steps,
                             per_step=per_step, n_buf=n_buf)
    return pl.pallas_call(
        kern,
        grid_spec=grid_spec,
        out_shape=[
            jax.ShapeDtypeStruct((db, 1, heads * HEAD_DIM), BF16),
            jax.ShapeDtypeStruct((db, n_buf, n_rows, HEAD_DIM), F32),
        ],
        compiler_params=_cparams(32, ("arbitrary", "arbitrary")),
        name="nsa_sample_attend",
    )(page_table, sel_idx, q3, *([cache5] * (groups * per_step)), new_sel, new_win, o_c, win_state,
      gates3, z3)


def _rope_tables(pos):
    half = ROT_DIM // 2
    inv_freq = ROPE_THETA ** (-jnp.arange(half, dtype=F32) * (2.0 / ROT_DIM))
    ang = pos.astype(F32)[:, None] * inv_freq[None, :]
    cos, sin = jnp.cos(ang), jnp.sin(ang)
    n = pos.shape[0]
    cos_t = jnp.concatenate([cos, cos, jnp.ones((n, HEAD_DIM - ROT_DIM), F32)], axis=1)
    sin_t = jnp.concatenate([-sin, sin, jnp.zeros((n, HEAD_DIM - ROT_DIM), F32)], axis=1)
    return cos_t, sin_t


def _odd_weight_layout(w, cw, kvw, rep):
    n_layers, k, _ = w.shape
    groups = kvw // HEAD_DIM
    n_gate = 3 * groups * rep
    gates = w[:, :, :n_gate].reshape(n_layers, k, groups, 3 * rep)
    gates = jnp.pad(gates, ((0, 0), (0, 0), (0, 0), (0, LANES - 3 * rep))).reshape(n_layers, k, groups * LANES)
    return jnp.concatenate([w[:, :, n_gate:], gates], axis=2)


def kernel(x_prompt, x_sample, cache_cmp_kv, cache_sel_kv, state_win_kv, state_pool, page_table,
           norm_even, w_in_even, v_norm, w_spatial, b_spatial, w_pool, pool_scale, w_out_even,
           norm_odd, w_in_odd, q_norm, k_norm, w_out_odd):
    bsz, seq, d_model = x_prompt.shape
    db, dec_t, _ = x_sample.shape
    assert dec_t == 1
    n_even, n_odd = norm_even.shape[0], norm_odd.shape[0]
    depth = n_even + n_odd
    n_pages = page_table.shape[1]
    past_len = n_pages * PAGE_SIZE
    aw = v_norm.shape[1]
    bw = pool_scale.shape[1]
    groups = C_KV_HEADS
    kvw = groups * HEAD_DIM
    cw = w_out_odd.shape[1]
    rep = cw // kvw
    heads = cw // HEAD_DIM
    n_phys = cache_cmp_kv.shape[1]
    n_buf = state_win_kv.shape[2]
    assert past_len % CMP_BLOCK == 0 and seq % CMP_BLOCK == 0
    nb_p = seq // CMP_BLOCK
    nb_past = past_len // CMP_BLOCK
    n_past_sel = min(N_SEL, nb_past + 1) - 1
    assert nb_past >= 2 and n_past_sel >= 2

    xp = x_prompt.reshape(bsz * seq, d_model)
    xs = x_sample.reshape(db, d_model)

    cos_p, sin_p = _rope_tables(jnp.arange(seq, dtype=I32))
    cos_s, sin_s = _rope_tables(jnp.full((db,), past_len, dtype=I32))
    tables = _mask_tables(seq, LANES, WINDOW + LANES, rep)

    cache_cmp5 = cache_cmp_kv.reshape(n_odd, n_phys, PAGE_SIZE, 2 * groups, HEAD_DIM)
    cache_sel5 = cache_sel_kv.reshape(n_odd, n_phys, PAGE_SIZE, 2 * groups, HEAD_DIM)
    win_state5 = state_win_kv.reshape(n_odd, db, n_buf, 2 * groups, HEAD_DIM)

    tn_in = 2560
    n_even_in = w_in_even.shape[2]
    n_qkv = cw + 6 * kvw
    assert n_even_in % tn_in == 0 and n_qkv % tn_in == 0
    w_in_even_b = w_in_even.astype(BF16)
    w_out_even_b = w_out_even.astype(BF16)
    w_in_odd_b = w_in_odd.astype(BF16)
    w_out_odd_b = w_out_odd.astype(BF16)
    w_zg_b = _odd_weight_layout(w_in_odd_b[:, :, n_qkv:], cw, kvw, rep)
    n_zg = w_zg_b.shape[2]

    cmp_p, cmp_s, sel_p, sel_s, win_p, win_s = [], [], [], [], [], []
    pool_p, pool_s, gv_s = [], [], []
    for layer in range(depth):
        li = layer // 2
        if layer % 2 == 0:
            g_in = norm_even[li][None, :]
            vg = v_norm[li][None, :]
            ps = pool_scale[li][None, :]
            wp = w_pool[li].astype(BF16)
            proj = _norm_matmul(xp, g_in, w_in_even_b, li, n_even_in, tn=tn_in)
            y3, pool16 = _even_mix(proj.reshape(bsz, seq, -1), xp.reshape(bsz, seq, d_model), vg,
                                   w_spatial[li], b_spatial[li].T, wp, ps, w_out_even_b, li)
            xp = y3.reshape(bsz * seq, d_model)
            pool_p.append(pool16[:, 16 - POOL_PAD:])
            proj_s = _norm_matmul(xs, g_in, w_in_even_b, li, n_even_in, tn=tn_in)
            hd = aw // A_HEADS
            w00 = jnp.repeat(w_spatial[li][:, 0, 0], hd)[None, :]
            b0 = jnp.repeat(b_spatial[li][:, 0], hd)[None, :]
            act_s, vn_s, new_state = _even_mix_sample(
                proj_s, vg, w00, b0, wp, ps, jnp.swapaxes(state_pool[li], 0, 1), past_len)
            xs = _matmul_residual(act_s, w_out_even_b, li, xs)
            pool_s.append(jnp.swapaxes(new_state, 0, 1))
            gv_s.append(vn_s.reshape(db, 1, aw))
        else:
            g_in = norm_odd[li][None, :]
            qg = q_norm[li][None, :]
            kg = k_norm[li]
            proj = _norm_matmul(xp, g_in, w_in_odd_b, li, n_qkv, tn=tn_in)
            proj_zg = _norm_matmul(xp, g_in, w_zg_b, li, n_zg, tn=n_zg)
            q_b, kv_cmp, kv_sel, kv_win, kt_sel, v_sel, kt_win, v_win, means = _odd_post(
                proj, cos_p, sin_p, qg, kg, cw=cw, kvw=kvw, for_prompt=True, win_keep=min(WINDOW, seq))
            r3 = lambda a: a.reshape(bsz, seq, a.shape[-1])
            act, means_s = _nsa_prompt(r3(q_b), kt_sel, r3(v_sel), kt_win, r3(v_win),
                                       means.reshape(bsz, nb_p, 2 * kvw), r3(proj_zg), tables,
                                       cache_cmp5, page_table, li, cw=cw, z_off=0, g_off=cw)
            xp = _matmul_residual(act.reshape(bsz * seq, cw), w_out_odd_b, li, xp)
            kv6 = lambda a, n: a.reshape(n, -1, 2, groups, HEAD_DIM)
            cmp_p.append(kv6(kv_cmp, bsz))
            sel_p.append(kv6(kv_sel, bsz))
            win_p.append(kv6(kv_win, bsz))
            proj_s = _norm_matmul(xs, g_in, w_in_odd_b, li, n_qkv, tn=tn_in)
            proj_zg_s = _norm_matmul(xs, g_in, w_zg_b, li, n_zg, tn=n_zg)
            q_s, kvc_s, kvs_s, kvw_s = _odd_post(
                proj_s, cos_s, sin_s, qg, kg, cw=cw, kvw=kvw, for_prompt=False)
            q3s = q_s.reshape(db, heads, HEAD_DIM)
            o_c, idx = _nsa_sample_select(q3s, means_s, groups=groups, rep=rep, q_pos=past_len,
                                          n_past_sel=n_past_sel)
            act_s, win_new = _nsa_sample_attend(
                page_table, idx[:, :, :n_past_sel], q3s, cache_sel5, li,
                kvs_s.reshape(db, 1, 2 * kvw), kvw_s.reshape(db, 1, 2 * kvw), o_c, win_state5,
                proj_zg_s[:, cw:].reshape(db, 1, groups * LANES),
                proj_zg_s[:, :cw].reshape(db, 1, cw),
                groups=groups, rep=rep, n_gather=n_past_sel)
            xs = _matmul_residual(act_s.reshape(db, cw), w_out_odd_b, li, xs)
            cmp_s.append(kv6(kvc_s, db))
            sel_s.append(kv6(kvs_s, db))
            win_s.append(win_new.reshape(db, n_buf, 2, groups, HEAD_DIM))
    return (xp.reshape(bsz, seq, d_model), xs.reshape(db, 1, d_model),
            jnp.stack(cmp_p), jnp.stack(cmp_s), jnp.stack(sel_p), jnp.stack(sel_s),
            jnp.stack(win_p), jnp.stack(win_s), jnp.stack(pool_p), jnp.stack(pool_s), jnp.stack(gv_s))
```

```python
import functools

import jax
import jax.numpy as jnp
import numpy as np
from jax import lax
from jax.experimental import pallas as pl
from jax.experimental.pallas import tpu as pltpu

F32 = jnp.float32
BF16 = jnp.bfloat16
I32 = jnp.int32

EPS = 1e-6
PAGE_SIZE = 128
A_HEADS = 8
GMLP_CHUNK = 128
POOL_WINDOWS = (2, 4, 8, 16)
POOL_PAD = max(POOL_WINDOWS) - 1
HEAD_DIM = 128
C_KV_HEADS = 4
CMP_BLOCK = 64
N_SEL = 16
WINDOW = 512
SEL_FORCE = 1e4
NEG_INF = -1e30
ROPE_THETA = 500000.0
ROT_DIM = HEAD_DIM // 4
LOG2E = 1.4426950408889634
Q_SCALE = HEAD_DIM ** -0.5 * LOG2E

LANES = 128
SUBLANES = 8
MIB = 1024 * 1024


def _cparams(vmem_mib, semantics=None):
    return pltpu.CompilerParams(vmem_limit_bytes=int(vmem_mib * MIB), dimension_semantics=semantics)


def _silu(x):
    return x * (1.0 / (1.0 + jnp.exp(-x)))


def _sigmoid(x):
    return 1.0 / (1.0 + jnp.exp(-x))


def _dot_nt(a, b):
    return lax.dot_general(a, b, (((1,), (1,)), ((), ())), preferred_element_type=F32)


def _dot(a, b):
    return jnp.dot(a, b, preferred_element_type=F32)


def _norm_mm_kernel(x_ref, g_ref, w_ref, o_ref):
    x = x_ref[...]
    ms = jnp.mean(x * x, axis=-1, keepdims=True)
    h = (x * lax.rsqrt(ms + EPS) * g_ref[...]).astype(BF16)
    o_ref[...] = jnp.dot(h, w_ref[...], preferred_element_type=F32)


def _mm_res_kernel(a_ref, w_ref, r_ref, o_ref):
    o_ref[...] = r_ref[...] + jnp.dot(a_ref[...], w_ref[...], preferred_element_type=F32)


def _row_tile(m, want):
    return want if m % want == 0 else m


def _norm_matmul(x, g, w, layer, n, *, tn, tm=512):
    m, k = x.shape
    tm = _row_tile(m, tm)
    assert n % tn == 0 and n <= w.shape[2]
    vmem = 2 * (tm * k * 4 + k * tn * 2 + tm * tn * 4) / MIB + 8
    return pl.pallas_call(
        _norm_mm_kernel,
        grid=(n // tn, m // tm),
        in_specs=[
            pl.BlockSpec((tm, k), lambda j, i: (i, 0)),
            pl.BlockSpec((1, k), lambda j, i: (0, 0)),
            pl.BlockSpec((None, k, tn), lambda j, i: (layer, 0, j)),
        ],
        out_specs=pl.BlockSpec((tm, tn), lambda j, i: (i, j)),
        out_shape=jax.ShapeDtypeStruct((m, n), F32),
        compiler_params=_cparams(vmem),
        name="norm_matmul",
    )(x, g, w)


def _matmul_residual(a, w, layer, res, *, tm=512):
    m, k = a.shape
    n = w.shape[2]
    tm = _row_tile(m, tm)
    vmem = 2 * (tm * k * 2 + k * n * 2 + 2 * tm * n * 4) / MIB + 8
    return pl.pallas_call(
        _mm_res_kernel,
        grid=(m // tm,),
        in_specs=[
            pl.BlockSpec((tm, k), lambda i: (i, 0)),
            pl.BlockSpec((None, k, n), lambda i: (layer, 0, 0)),
            pl.BlockSpec((tm, n), lambda i: (i, 0)),
        ],
        out_specs=pl.BlockSpec((tm, n), lambda i: (i, 0)),
        out_shape=jax.ShapeDtypeStruct((m, n), F32),
        compiler_params=_cparams(vmem),
        name="matmul_residual",
    )(a, w, res)


def _even_mix_kernel(proj_ref, x_ref, vg_ref, ws_ref, bst_ref, wp_ref, ps_ref, wout_ref,
                     y_ref, pool_ref, ext_ref, act_ref, *, tm, aw, bw):
    c = pl.program_id(1)
    n_c = pl.num_programs(1)
    hd = aw // A_HEADS
    pg = bw // len(POOL_WINDOWS)

    cl = GMLP_CHUNK
    row = lax.broadcasted_iota(I32, (cl, cl), 0)
    col = lax.broadcasted_iota(I32, (cl, cl), 1)
    causal = row >= col

    for h in range(A_HEADS):
        sl = slice(h * hd, (h + 1) * hd)
        w = jnp.where(causal, ws_ref[h], 0.0).astype(BF16)
        for c0 in range(0, tm, cl):
            rs = slice(c0, c0 + cl)
            u = proj_ref[0, rs, sl]
            v = proj_ref[0, rs, aw + h * hd:aw + (h + 1) * hd]
            za = proj_ref[0, rs, 2 * aw + h * hd:2 * aw + (h + 1) * hd]
            ms = jnp.mean(v * v, axis=-1, keepdims=True)
            vn = v * lax.rsqrt(ms + EPS) * vg_ref[:, sl]
            s = jnp.dot(w, vn.astype(BF16), preferred_element_type=F32) + bst_ref[:, h:h + 1]
            act_ref[0, rs, sl] = ((u * s) * _silu(za)).astype(BF16)

    p_off = 3 * aw
    zb_off = 3 * aw + bw

    @pl.when(c == 0)
    def _():
        ext_ref[0:16, :] = jnp.zeros((16, bw), F32)

    ext_ref[16:16 + tm, :] = proj_ref[0, :, p_off:p_off + bw]
    pos = c * tm + lax.broadcasted_iota(I32, (tm, 1), 0)
    for g, wnd in enumerate(POOL_WINDOWS):
        sl = slice(g * pg, (g + 1) * pg)
        cur = ext_ref[16:16 + tm, sl]
        acc = cur
        for k in range(1, wnd):
            acc = acc + ext_ref[16 - k:16 - k + tm, sl]
        cnt = jnp.minimum(pos + 1, wnd).astype(F32)
        d = acc / cnt - cur
        y = jnp.dot(d.astype(BF16), wp_ref[g], preferred_element_type=F32)
        zb = proj_ref[0, :, zb_off + g * pg:zb_off + (g + 1) * pg]
        act_ref[0, :, aw + g * pg:aw + (g + 1) * pg] = ((y * ps_ref[:, sl]) * _silu(zb)).astype(BF16)

    tail = ext_ref[tm:tm + 16, :]
    ext_ref[0:16, :] = tail

    @pl.when(c == n_c - 1)
    def _():
        pool_ref[0] = tail

    y_ref[0] = x_ref[0] + jnp.dot(act_ref[0], wout_ref[...], preferred_element_type=F32)


def _even_mix(proj3, x3, v_gain, w_s, b_st, w_pool, pool_scale, w_out, layer):
    b, s, _ = proj3.shape
    d = x3.shape[2]
    aw = v_gain.shape[1]
    bw = pool_scale.shape[1]
    tm = 2 * GMLP_CHUNK if s % (2 * GMLP_CHUNK) == 0 else GMLP_CHUNK
    assert s % tm == 0 and s >= 16
    n_in = 3 * aw + 2 * bw
    kern = functools.partial(_even_mix_kernel, tm=tm, aw=aw, bw=bw)
    return pl.pallas_call(
        kern,
        grid=(b, s // tm),
        in_specs=[
            pl.BlockSpec((1, tm, n_in), lambda i, c: (i, c, 0)),
            pl.BlockSpec((1, tm, d), lambda i, c: (i, c, 0)),
            pl.BlockSpec((1, aw), lambda i, c: (0, 0)),
            pl.BlockSpec(w_s.shape, lambda i, c: (0, 0, 0)),
            pl.BlockSpec(b_st.shape, lambda i, c: (0, 0)),
            pl.BlockSpec(w_pool.shape, lambda i, c: (0, 0, 0)),
            pl.BlockSpec((1, bw), lambda i, c: (0, 0)),
            pl.BlockSpec((None, aw + bw, d), lambda i, c: (layer, 0, 0)),
        ],
        out_specs=[
            pl.BlockSpec((1, tm, d), lambda i, c: (i, c, 0)),
            pl.BlockSpec((1, 16, bw), lambda i, c: (i, 0, 0)),
        ],
        out_shape=[
            jax.ShapeDtypeStruct((b, s, d), F32),
            jax.ShapeDtypeStruct((b, 16, bw), F32),
        ],
        scratch_shapes=[pltpu.VMEM((16 + tm, bw), F32), pltpu.VMEM((1, tm, aw + bw), BF16)],
        compiler_params=_cparams(48, ("arbitrary", "arbitrary")),
        name="even_mix",
    )(proj3, x3, v_gain, w_s, b_st, w_pool, pool_scale, w_out)


def _even_mix_sample_kernel(proj_ref, vg_ref, w00_ref, b0_ref, wp_ref, ps_ref, st_ref,
                            act_ref, vn_ref, newst_ref, *, aw, bw, pos):
    hd = aw // A_HEADS
    pg = bw // len(POOL_WINDOWS)
    for h in range(A_HEADS):
        sl = slice(h * hd, (h + 1) * hd)
        u = proj_ref[:, sl]
        v = proj_ref[:, aw + h * hd:aw + (h + 1) * hd]
        za = proj_ref[:, 2 * aw + h * hd:2 * aw + (h + 1) * hd]
        ms = jnp.mean(v * v, axis=-1, keepdims=True)
        vn = v * lax.rsqrt(ms + EPS) * vg_ref[:, sl]
        vn_ref[:, sl] = vn
        s = w00_ref[:, sl] * vn + b0_ref[:, sl]
        act_ref[:, sl] = ((u * s) * _silu(za)).astype(BF16)

    p = proj_ref[:, 3 * aw:3 * aw + bw]
    for g, wnd in enumerate(POOL_WINDOWS):
        sl = slice(g * pg, (g + 1) * pg)
        cur = p[:, sl]
        acc = cur
        for k in range(1, wnd):
            acc = acc + st_ref[POOL_PAD - k, :, sl]
        cnt = float(min(pos + 1, wnd))
        d = acc / cnt - cur
        y = jnp.dot(d.astype(BF16), wp_ref[g], preferred_element_type=F32)
        zb = proj_ref[:, 3 * aw + bw + g * pg:3 * aw + bw + (g + 1) * pg]
        act_ref[:, aw + g * pg:aw + (g + 1) * pg] = ((y * ps_ref[:, sl]) * _silu(zb)).astype(BF16)

    for k in range(POOL_PAD - 1):
        newst_ref[k] = st_ref[k + 1]
    newst_ref[POOL_PAD - 1] = p


def _even_mix_sample(proj, v_gain, w00, b0, w_pool, pool_scale, state_t, pos):
    db = proj.shape[0]
    aw = v_gain.shape[1]
    bw = pool_scale.shape[1]
    kern = functools.partial(_even_mix_sample_kernel, aw=aw, bw=bw, pos=pos)
    return pl.pallas_call(
        kern,
        out_shape=[
            jax.ShapeDtypeStruct((db, aw + bw), BF16),
            jax.ShapeDtypeStruct((db, aw), F32),
            jax.ShapeDtypeStruct((POOL_PAD, db, bw), F32),
        ],
        name="even_mix_sample",
    )(proj, v_gain, w00, b0, w_pool, pool_scale, state_t)


def _odd_post_kernel(proj_ref, cos_ref, sin_ref, qg_ref, kg_ref, *out_refs, tm, cw, kvw, for_prompt):
    q_out, cmp_out, sel_out, win_out = out_refs[:4]
    if for_prompt:
        selkt_out, selv_out, winkt_out, winv_out, means_out = out_refs[4:]
        attn_outs = (None, (selkt_out, selv_out), (winkt_out, winv_out))
    else:
        attn_outs = (None, None, None)
    cosf = cos_ref[...]
    sinf = sin_ref[...]
    half = ROT_DIM // 2
    first = lax.broadcasted_iota(I32, (tm, HEAD_DIM), 1) < half

    def norm_rope(x, gain):
        ms = jnp.mean(x * x, axis=-1, keepdims=True)
        y = x * lax.rsqrt(ms + EPS) * gain
        rot = jnp.where(first, pltpu.roll(y, HEAD_DIM - half, 1), pltpu.roll(y, half, 1))
        return y * cosf + rot * sinf

    qg = qg_ref[...]
    for h in range(cw // HEAD_DIM):
        sl = slice(h * HEAD_DIM, (h + 1) * HEAD_DIM)
        q_out[:, sl] = (norm_rope(proj_ref[:, sl], qg) * Q_SCALE).astype(BF16)

    for br, (o32, attn) in enumerate(zip((cmp_out, sel_out, win_out), attn_outs)):
        k_off = cw + br * 2 * kvw
        v_off = k_off + kvw
        kg = kg_ref[br:br + 1, :]
        pieces = []
        for g in range(kvw // HEAD_DIM):
            sl = slice(g * HEAD_DIM, (g + 1) * HEAD_DIM)
            kk = norm_rope(proj_ref[:, k_off + g * HEAD_DIM:k_off + (g + 1) * HEAD_DIM], kg)
            pieces.append(kk)
            if attn is not None:
                attn[0][sl, :] = kk.T.astype(BF16)
        vv = proj_ref[:, v_off:v_off + kvw]
        if attn is not None:
            attn[1][...] = vv.astype(BF16)
        row = jnp.concatenate(pieces + [vv], axis=1)
        o32[...] = row.reshape(o32.shape)
        if br == 0 and for_prompt:
            for j in range(tm // CMP_BLOCK):
                blk = row[j * CMP_BLOCK:(j + 1) * CMP_BLOCK, :]
                means_out[0, j:j + 1, :] = jnp.mean(blk, axis=0, keepdims=True)


def _odd_post(proj, cos_t, sin_t, q_gain, k_gain, *, cw, kvw, for_prompt, win_keep=0, tm=512):
    m = proj.shape[0]
    tm = _row_tile(m, tm)
    n_pos_tiles = cos_t.shape[0] // tm
    n_used = cw + 6 * kvw
    kern = functools.partial(_odd_post_kernel, tm=tm, cw=cw, kvw=kvw, for_prompt=for_prompt)
    row_spec = lambda w: pl.BlockSpec((tm, w), lambda i: (i, 0))
    n_rows = 2 * kvw // HEAD_DIM
    if for_prompt:
        assert win_keep % tm == 0 and cos_t.shape[0] >= win_keep
        keep_tiles = win_keep // tm
        skip_tiles = n_pos_tiles - keep_tiles
        kv_shape = jax.ShapeDtypeStruct((m, n_rows, HEAD_DIM), F32)
        kv_spec = pl.BlockSpec((tm, n_rows, HEAD_DIM), lambda i: (i, 0, 0))
        win_shape = jax.ShapeDtypeStruct((m // n_pos_tiles * keep_tiles, n_rows, HEAD_DIM), F32)
        win_spec = pl.BlockSpec(
            (tm, n_rows, HEAD_DIM),
            lambda i: (i // n_pos_tiles * keep_tiles + jnp.maximum(i % n_pos_tiles - skip_tiles, 0), 0, 0))
    else:
        kv_shape = win_shape = jax.ShapeDtypeStruct((m, 2 * kvw), F32)
        kv_spec = win_spec = row_spec(2 * kvw)
    out_shape = [jax.ShapeDtypeStruct((m, cw), BF16), kv_shape, kv_shape, win_shape]
    out_specs = [row_spec(cw), kv_spec, kv_spec, win_spec]
    if for_prompt:
        assert tm % CMP_BLOCK == 0 and tm % LANES == 0
        for _ in range(2):
            out_shape += [jax.ShapeDtypeStruct((kvw, m), BF16), jax.ShapeDtypeStruct((m, kvw), BF16)]
            out_specs += [pl.BlockSpec((kvw, tm), lambda i: (0, i)), row_spec(kvw)]
        out_shape.append(jax.ShapeDtypeStruct((m // tm, tm // CMP_BLOCK, 2 * kvw), F32))
        out_specs.append(pl.BlockSpec((1, tm // CMP_BLOCK, 2 * kvw), lambda i: (i, 0, 0)))
    return pl.pallas_call(
        kern,
        grid=(m // tm,),
        in_specs=[
            pl.BlockSpec((tm, n_used), lambda i: (i, 0)),
            pl.BlockSpec((tm, HEAD_DIM), lambda i: (i % n_pos_tiles, 0)),
            pl.BlockSpec((tm, HEAD_DIM), lambda i: (i % n_pos_tiles, 0)),
            pl.BlockSpec((1, HEAD_DIM), lambda i: (0, 0)),
            pl.BlockSpec((3, HEAD_DIM), lambda i: (0, 0)),
        ],
        out_specs=out_specs,
        out_shape=out_shape,
        compiler_params=_cparams(56),
        name="odd_post",
    )(proj, cos_t, sin_t, q_gain, k_gain)


def _nsa_sub_tile(i, h, q_ref, ks_ref, vs_ref, kw_ref, vw_ref, kc, vc_t, band_ref, tri_ref, eye_ref,
                  *, tq, nb, rep, lw):
    r0 = h * tq
    q = jnp.concatenate([q_ref[0, r0:r0 + tq, r * HEAD_DIM:(r + 1) * HEAD_DIM] for r in range(rep)],
                        axis=0)
    t_row = i * tq + lax.broadcasted_iota(I32, (1, tq), 1)

    blk_col = lax.broadcasted_iota(I32, (nb, 1), 0)
    s_t = _dot_nt(kc, q)
    ok_t = ((blk_col + 1) * CMP_BLOCK - 1) <= t_row
    bias_t = jnp.where(ok_t, 0.0, NEG_INF)
    okf_t = jnp.where(ok_t, 1.0, 0.0)
    imp = jnp.zeros((nb, tq), F32)
    o_c = []
    for r in range(rep):
        s_r = s_t[:, r * tq:(r + 1) * tq] + bias_t
        e_r = jnp.exp2(s_r - jnp.max(s_r, axis=0, keepdims=True))
        p_r = e_r / jnp.sum(e_r, axis=0, keepdims=True) * okf_t
        imp = imp + p_r
        if LANES > nb:
            p_r = jnp.concatenate([p_r, jnp.zeros((LANES - nb, tq), F32)], axis=0)
        o_c.append(_dot(vc_t, p_r.astype(BF16)).T)
    o_c = jnp.concatenate(o_c, axis=0)
    cur = t_row // CMP_BLOCK
    forced = (blk_col == 0) | (blk_col == cur) | (blk_col == cur - 1)
    imp = jnp.where(blk_col <= cur, jnp.where(forced, SEL_FORCE, imp), -1.0)

    blk8 = lax.broadcasted_iota(I32, (SUBLANES, 1), 0)
    cnt = jnp.zeros((nb, tq), F32)
    for j in range(nb):
        rj = imp[j:j + 1, :]
        parts = []
        for v0 in range(0, nb, SUBLANES):
            sub = imp[v0:v0 + SUBLANES, :]
            if v0 > j:
                beats = rj >= sub
            elif v0 + SUBLANES - 1 <= j:
                beats = rj > sub
            else:
                beats = (rj > sub) | ((rj == sub) & (blk8 > j - v0))
            parts.append(jnp.where(beats, 1.0, 0.0))
        cnt = cnt + jnp.concatenate(parts, axis=0)
    keep = (cnt < min(N_SEL, nb)) & (imp >= 0.0) & (blk_col < i * (tq // CMP_BLOCK))
    drop_t = jnp.where(keep, 0.0, 1.0)
    if LANES > nb:
        drop_t = jnp.concatenate([drop_t, jnp.zeros((LANES - nb, tq), F32)], axis=0)
    drop = drop_t.T.astype(BF16)
    q_drop = jnp.concatenate([q, jnp.concatenate([drop] * rep, axis=0)], axis=1)

    q_eye = jnp.concatenate([q, eye_ref[...]], axis=1)

    w0 = pl.multiple_of(jnp.maximum(i * tq + tq - lw, 0), tq)
    band = band_ref[jnp.minimum(i, band_ref.shape[0] - 1)]
    k_w = jnp.concatenate([kw_ref[:, pl.ds(w0, lw)], band], axis=0)
    v_w = jnp.concatenate([vw_ref[0, pl.ds(w0, lw), :], jnp.ones((lw, HEAD_DIM), BF16)], axis=1)
    s_w = _dot(q_eye, k_w)
    p_w = jnp.exp2(s_w - jnp.max(s_w, axis=-1, keepdims=True))
    acc_w = _dot(p_w.astype(BF16), v_w)
    o_w = acc_w[:, :HEAD_DIM] / acc_w[:, HEAD_DIM:]

    d0 = pl.multiple_of(i * tq, tq)
    k_d = jnp.concatenate([ks_ref[:, pl.ds(d0, tq)], tri_ref[...]], axis=0)
    v_d = jnp.concatenate([vs_ref[0, pl.ds(d0, tq), :], jnp.ones((tq, HEAD_DIM), BF16)], axis=1)
    s_d = _dot(q_eye, k_d)
    m_d = jnp.max(s_d, axis=-1, keepdims=True)
    acc_d = _dot(jnp.exp2(s_d - m_d).astype(BF16), v_d)
    return q_drop, o_c, o_w, m_d, acc_d


def _nsa_prompt_kernel(pt_ref, q_ref, ks_ref, vs_ref, kw_ref, vw_ref, kc_ref, vc_ref, g_ref, z_ref,
                       eb_ref, band_ref, tri_ref, eye_ref, *rest, tq, nh, tk, seq, nb, rep, lw, pps):
    if pps:
        o_ref, pm_ref, s0_ref, s1_ref = rest[pps:]
        _store_page_means(rest[:pps], pm_ref)
    else:
        o_ref, s0_ref, s1_ref = rest
    step = pl.program_id(2)
    kc = kc_ref[0].astype(BF16)
    vc = vc_ref[0]
    if LANES > nb:
        vc = jnp.concatenate([vc, jnp.zeros((LANES - nb, HEAD_DIM), F32)], axis=0)
    vc_t = vc.T.astype(BF16)
    subs = [_nsa_sub_tile(step * nh + h, h, q_ref, ks_ref, vs_ref, kw_ref, vw_ref, kc, vc_t, band_ref,
                          tri_ref, eye_ref, tq=tq, nb=nb, rep=rep, lw=lw) for h in range(nh)]

    q_drop = jnp.concatenate([s[0] for s in subs], axis=0)
    m_0 = jnp.concatenate([s[3] for s in subs], axis=0)
    acc_0 = jnp.concatenate([s[4] for s in subs], axis=0)

    partial, gate_s, z_act = [], [], []
    for h in range(nh):
        ts = slice(h * tq, (h + 1) * tq)
        gs = _sigmoid(g_ref[0, ts, :])
        o_c, o_w = subs[h][1], subs[h][2]
        for r in range(rep):
            rs = slice(r * tq, (r + 1) * tq)
            partial.append(gs[:, 3 * r:3 * r + 1] * o_c[rs] + gs[:, 3 * r + 2:3 * r + 3] * o_w[rs])
            gate_s.append(gs[:, 3 * r + 1:3 * r + 2])
            z_act.append(_silu(z_ref[0, ts, r * HEAD_DIM:(r + 1) * HEAD_DIM]))

    ones_k = jnp.ones((tk, HEAD_DIM), BF16)
    n_tiles = seq // tk

    def tile_scores(t):
        k0 = pl.multiple_of(t * tk, tk)
        k = jnp.concatenate([ks_ref[:, pl.ds(k0, tk)], eb_ref[:, pl.ds(k0, tk)]], axis=0)
        return _dot(q_drop, k)

    def accumulate(carry, s, t):
        m_p, acc = carry
        k0 = pl.multiple_of(t * tk, tk)
        v = jnp.concatenate([vs_ref[0, pl.ds(k0, tk), :], ones_k], axis=1)
        m_n = jnp.maximum(m_p, jnp.max(s, axis=-1, keepdims=True))
        pv = _dot(jnp.exp2(s - m_n).astype(BF16), v)
        return m_n, jnp.exp2(m_p - m_n) * acc + pv

    def sel_body(j, carry):
        t1 = 2 * j + 1
        s1_ref[...] = tile_scores(t1)
        carry = accumulate(carry, s0_ref[...], 2 * j)
        s0_ref[...] = tile_scores(jnp.minimum(t1 + 1, n_tiles - 1))
        return accumulate(carry, s1_ref[...], t1)

    n_past = ((step * nh + nh - 1) * tq + tk - 1) // tk
    s0_ref[...] = tile_scores(0)
    _, acc_s = lax.fori_loop(0, (n_past + 1) // 2, sel_body, (m_0, acc_0))
    o_s = acc_s[:, :HEAD_DIM] / acc_s[:, HEAD_DIM:]

    for h in range(nh):
        for r in range(rep):
            n = h * rep + r
            o = partial[n] + gate_s[n] * o_s[n * tq:(n + 1) * tq]
            o_ref[0, h * tq:(h + 1) * tq, r * HEAD_DIM:(r + 1) * HEAD_DIM] = (o * z_act[n]).astype(BF16)


def _mask_tables(seq, tq, lw, rep):
    neg = lambda ok: jnp.where(ok, 0.0, NEG_INF).astype(BF16)
    row = jnp.arange(LANES, dtype=I32)[:, None]
    key = jnp.arange(seq, dtype=I32)[None, :]
    block_rows = neg(key // CMP_BLOCK != row)
    t = jnp.arange(tq, dtype=I32)[None, :, None]
    c = jnp.arange(lw, dtype=I32)[None, None, :]
    off = jnp.arange(WINDOW // tq + 1, dtype=I32)[:, None, None] * tq
    rel = off + t - c
    band = neg((rel >= 0) & (rel < WINDOW))
    tri = neg(jnp.arange(tq, dtype=I32)[None, :] <= jnp.arange(tq, dtype=I32)[:, None])
    eye = jnp.tile(jnp.eye(tq, dtype=BF16), (rep, 1))
    return block_rows, band, tri, eye


def _page_job_split(n_steps, page_table):
    db, n_pages = page_table.shape
    total = db * n_pages
    if total % n_steps:
        return 0
    pps = total // n_steps
    return pps if n_pages % pps == 0 else 0


def _nsa_prompt(q3, kt_sel, v_sel, kt_win, v_win, means3, proj3, tables, cache5, page_table, layer, *,
                cw, z_off, g_off, tq=LANES, nh=2, tk=512):
    b, s, _ = q3.shape
    kvw = v_sel.shape[2]
    groups = kvw // HEAD_DIM
    rep = cw // kvw
    nb = means3.shape[1]
    gw = rep * HEAD_DIM
    block_rows, band, tri, eye = tables
    tk = min(tk, s)
    lw = band.shape[2]
    ts = nh * tq
    assert tq == LANES and nb <= LANES and nb % SUBLANES == 0 and lw == WINDOW + tq and s >= lw
    assert s % ts == 0 and s % (2 * tk) == 0 and tk % tq == 0 and tq % CMP_BLOCK == 0
    assert nb * CMP_BLOCK == s
    assert z_off % gw == 0 and g_off % LANES == 0
    n_i = s // ts
    pps = _page_job_split(b * groups * n_i, page_table)
    kern = functools.partial(_nsa_prompt_kernel, tq=tq, nh=nh, tk=tk, seq=s, nb=nb, rep=rep, lw=lw,
                             pps=pps)
    db, n_pages = page_table.shape
    n_rows, bpp = cache5.shape[3], PAGE_SIZE // CMP_BLOCK
    spp = n_pages // pps if pps else 0
    assert b * groups * n_i * pps in (0, db * n_pages)
    flat = lambda bi, g, i: (bi * groups + g) * n_i + i

    def page_spec(r):
        def index_map(bi, g, i, pt):
            f = flat(bi, g, i)
            return (layer, pt[f // spp, (f % spp) * pps + r], 0, 0, 0)
        return pl.BlockSpec((None, None, PAGE_SIZE, n_rows, HEAD_DIM), index_map)

    kt_spec = pl.BlockSpec((HEAD_DIM, s), lambda bi, g, i, pt: (g, bi))
    v_spec = pl.BlockSpec((1, s, HEAD_DIM), lambda bi, g, i, pt: (bi, 0, g))
    mean_spec = lambda off: pl.BlockSpec((1, nb, HEAD_DIM), lambda bi, g, i, pt: (bi, 0, off + g))
    grid_spec = pltpu.PrefetchScalarGridSpec(
        num_scalar_prefetch=1,
        grid=(b, groups, n_i),
        in_specs=[
            pl.BlockSpec((1, ts, gw), lambda bi, g, i, pt: (bi, i, g)),
            kt_spec, v_spec, kt_spec, v_spec,
            mean_spec(0), mean_spec(groups),
            pl.BlockSpec((1, ts, LANES), lambda bi, g, i, pt: (bi, i, g_off // LANES + g)),
            pl.BlockSpec((1, ts, gw), lambda bi, g, i, pt: (bi, i, z_off // gw + g)),
            pl.BlockSpec(block_rows.shape, lambda bi, g, i, pt: (0, 0)),
            pl.BlockSpec(band.shape, lambda bi, g, i, pt: (0, 0, 0)),
            pl.BlockSpec(tri.shape, lambda bi, g, i, pt: (0, 0)),
            pl.BlockSpec(eye.shape, lambda bi, g, i, pt: (0, 0)),
        ] + [page_spec(r) for r in range(pps)],
        out_specs=[pl.BlockSpec((1, ts, gw), lambda bi, g, i, pt: (bi, i, g))] + ([
            pl.BlockSpec((1, n_rows, pps * bpp, HEAD_DIM),
                         lambda bi, g, i, pt: (flat(bi, g, i) // spp, 0, flat(bi, g, i) % spp, 0)),
        ] if pps else []),
        scratch_shapes=[pltpu.VMEM((nh * rep * tq, tk), F32), pltpu.VMEM((nh * rep * tq, tk), F32)],
    )
    outs = pl.pallas_call(
        kern,
        grid_spec=grid_spec,
        out_shape=[jax.ShapeDtypeStruct((b, s, cw), BF16)] + ([
            jax.ShapeDtypeStruct((db, n_rows, n_pages * bpp, HEAD_DIM), F32)] if pps else []),
        compiler_params=_cparams(56),
        name="nsa_prompt",
    )(page_table, q3, kt_sel, v_sel, kt_win, v_win, means3, means3, proj3, proj3, block_rows, band, tri,
      eye, *([cache5] * pps))
    return outs if pps else (outs[0], _page_means(cache5, page_table, layer))


def _store_page_means(page_refs, out_ref):
    bpp = PAGE_SIZE // CMP_BLOCK
    n_rows = out_ref.shape[1]
    for r, page_ref in enumerate(page_refs):
        for h in range(bpp):
            mean = jnp.mean(page_ref[h * CMP_BLOCK:(h + 1) * CMP_BLOCK], axis=0)
            n = r * bpp + h
            for j in range(n_rows):
                out_ref[0, j, n:n + 1, :] = mean[j:j + 1, :]


def _page_means_kernel(pt_ref, *refs, pps):
    _store_page_means(refs[:pps], refs[pps])


def _page_means(cache5, page_table, layer, *, pps=8):
    db, n_pages = page_table.shape
    n_rows, d = cache5.shape[3], cache5.shape[4]
    pps = pps if n_pages % pps == 0 else 1
    bpp = PAGE_SIZE // CMP_BLOCK

    def page_spec(r):
        return pl.BlockSpec((None, None, PAGE_SIZE, n_rows, d),
                            lambda bi, j, pt: (layer, pt[bi, j * pps + r], 0, 0, 0))

    grid_spec = pltpu.PrefetchScalarGridSpec(
        num_scalar_prefetch=1,
        grid=(db, n_pages // pps),
        in_specs=[page_spec(r) for r in range(pps)],
        out_specs=pl.BlockSpec((1, n_rows, pps * bpp, d), lambda bi, j, pt: (bi, 0, j, 0)),
    )
    return pl.pallas_call(
        functools.partial(_page_means_kernel, pps=pps),
        grid_spec=grid_spec,
        out_shape=jax.ShapeDtypeStruct((db, n_rows, n_pages * bpp, d), F32),
        compiler_params=_cparams(24),
        name="page_means",
    )(page_table, *([cache5] * pps))


def _nsa_sample_select_kernel(q_ref, mean_ref, oc_ref, idx_ref, *, nbp, groups, rep, q_pos, n_past_sel):
    q = q_ref[0]
    heads = groups * rep
    lane = lax.broadcasted_iota(I32, (1, nbp), 1)
    ri = lax.broadcasted_iota(I32, (nbp, nbp), 0)
    ci = lax.broadcasted_iota(I32, (nbp, nbp), 1)
    diag = ri == ci
    ok = ((lane + 1) * CMP_BLOCK - 1) <= q_pos
    cur = q_pos // CMP_BLOCK
    forced = (lane == 0) | (lane == cur) | (lane == cur - 1)
    hrow = lax.broadcasted_iota(I32, (heads, 1), 0) // rep
    out_lane = lax.broadcasted_iota(I32, (1, LANES), 1)
    o_c = jnp.zeros((heads, HEAD_DIM), F32)
    for g in range(groups):
        kc = mean_ref[0, g].astype(BF16)
        vc = mean_ref[0, groups + g].astype(BF16)
        s = _dot_nt(q, kc) + jnp.where(ok, 0.0, NEG_INF)
        e = jnp.exp2(s - jnp.max(s, axis=-1, keepdims=True))
        p = e / jnp.sum(e, axis=-1, keepdims=True) * jnp.where(ok, 1.0, 0.0)
        in_g = hrow == g
        o_c = o_c + jnp.where(in_g, jnp.dot(p.astype(BF16), vc, preferred_element_type=F32), 0.0)
        imp = jnp.sum(jnp.where(in_g, p, 0.0), axis=0, keepdims=True)
        imp = jnp.where(lane <= cur, jnp.where(forced, SEL_FORCE, imp), -1.0)
        imp_b = jnp.broadcast_to(imp, (nbp, nbp))
        imp_col = jnp.sum(jnp.where(diag, imp_b, 0.0), axis=1, keepdims=True)
        beats = (imp_col > imp_b) | ((imp_col == imp_b) & (ri < ci))
        cnt = jnp.sum(jnp.where(beats, 1.0, 0.0), axis=0, keepdims=True)
        sel = jnp.where((cnt < n_past_sel) & (imp >= 0.0), 1.0, 0.0)
        sel_col = jnp.sum(jnp.where(diag, jnp.broadcast_to(sel, (nbp, nbp)), 0.0), axis=1, keepdims=True)
        before = jnp.sum(jnp.where(ri < ci, sel_col, 0.0), axis=0, keepdims=True)
        row = jnp.zeros((1, LANES), F32)
        for k in range(n_past_sel):
            hit = (sel > 0.5) & (before == float(k))
            idx_k = jnp.sum(jnp.where(hit, lane.astype(F32), 0.0), axis=1, keepdims=True)
            row = jnp.where(out_lane == k, idx_k, row)
        idx_ref[0, g:g + 1, :] = row.astype(I32)
    oc_ref[0] = o_c


def _nsa_sample_select(q3, means, *, groups, rep, q_pos, n_past_sel):
    db, heads, _ = q3.shape
    nbp = means.shape[2]
    kern = functools.partial(_nsa_sample_select_kernel, nbp=nbp, groups=groups, rep=rep, q_pos=q_pos,
                             n_past_sel=n_past_sel)
    return pl.pallas_call(
        kern,
        grid=(db,),
        in_specs=[
            pl.BlockSpec((1, heads, HEAD_DIM), lambda bi: (bi, 0, 0)),
            pl.BlockSpec((1,) + means.shape[1:], lambda bi: (bi, 0, 0, 0)),
        ],
        out_specs=[
            pl.BlockSpec((1, heads, HEAD_DIM), lambda bi: (bi, 0, 0)),
            pl.BlockSpec((1, groups, LANES), lambda bi: (bi, 0, 0)),
        ],
        out_shape=[
            jax.ShapeDtypeStruct((db, heads, HEAD_DIM), F32),
            jax.ShapeDtypeStruct((db, groups, LANES), I32),
        ],
        compiler_params=_cparams(24),
        name="nsa_sample_select",
    )(q3, means)


def _nsa_sample_attend_kernel(pt_ref, si_ref, q_ref, *refs, groups, rep, n_steps, per_step, n_buf):
    kv_refs = refs[:groups * per_step]
    (newsel_ref, newwin_ref, oc_ref, win_ref, g_ref, z_ref,
     act_ref, winout_ref, m_ref, l_ref, acc_ref) = refs[groups * per_step:]
    kvw = groups * HEAD_DIM
    heads = groups * rep
    n_rows = 2 * groups
    k_id = pl.program_id(1)
    q = q_ref[0]
    qf = q.astype(F32)
    hgrp = lax.broadcasted_iota(I32, (heads, 1), 0) // rep

    def own_key_rows(n_tok):
        lane = lax.broadcasted_iota(I32, (1, n_tok * n_rows), 1)
        return lane % n_rows == hgrp, lane // n_rows

    def by_group(fn):
        out = None
        for g in range(groups):
            val = jnp.where(hgrp == g, fn(g), 0.0)
            out = val if out is None else out + val
        return out

    def new_token_scores(row_ref):
        return by_group(lambda g: jnp.sum(
            qf * row_ref[0, :, g * HEAD_DIM:(g + 1) * HEAD_DIM], axis=-1, keepdims=True))

    def new_token_values(row_ref):
        return by_group(lambda g: jnp.broadcast_to(
            row_ref[0, :, kvw + g * HEAD_DIM:kvw + (g + 1) * HEAD_DIM], (heads, HEAD_DIM)))

    @pl.when(k_id == 0)
    def _():
        m_ref[...] = jnp.full((heads, 1), NEG_INF, F32)
        l_ref[...] = jnp.zeros((heads, 1), F32)
        acc_ref[...] = jnp.zeros((heads, HEAD_DIM), F32)

    xs = [jnp.concatenate([kv_refs[g * per_step + u][...].reshape(CMP_BLOCK * n_rows, HEAD_DIM)
                           for u in range(per_step)], axis=0).astype(BF16) for g in range(groups)]
    own, _ = own_key_rows(CMP_BLOCK * per_step)
    s = by_group(lambda g: _dot_nt(q, xs[g])) + jnp.where(own, 0.0, NEG_INF)
    m_p = m_ref[...]
    m_n = jnp.maximum(m_p, jnp.max(s, axis=-1, keepdims=True))
    alpha = jnp.exp2(m_p - m_n)
    p = jnp.exp2(s - m_n)
    pb = pltpu.roll(p, groups, 1).astype(BF16)
    pv = by_group(lambda g: jnp.dot(pb, xs[g], preferred_element_type=F32))
    m_ref[...] = m_n
    l_ref[...] = alpha * l_ref[...] + jnp.sum(p, axis=-1, keepdims=True)
    acc_ref[...] = alpha * acc_ref[...] + pv

    @pl.when(k_id == n_steps - 1)
    def _():
        s_n = new_token_scores(newsel_ref)
        m_p = m_ref[...]
        m_n = jnp.maximum(m_p, s_n)
        alpha = jnp.exp2(m_p - m_n)
        p_n = jnp.exp2(s_n - m_n)
        l_s = alpha * l_ref[...] + p_n
        o_s = (alpha * acc_ref[...] + p_n * new_token_values(newsel_ref)) / l_s

        xw = win_ref[0].reshape(n_buf * n_rows, HEAD_DIM).astype(BF16)
        own_w, tok_w = own_key_rows(n_buf)
        ok_w = own_w & ((n_buf - tok_w) < WINDOW)
        s_w = _dot_nt(q, xw) + jnp.where(ok_w, 0.0, NEG_INF)
        s_wn = new_token_scores(newwin_ref)
        m_w = jnp.maximum(jnp.max(s_w, axis=-1, keepdims=True), s_wn)
        p_w = jnp.exp2(s_w - m_w)
        p_wn = jnp.exp2(s_wn - m_w)
        l_w = jnp.sum(p_w, axis=-1, keepdims=True) + p_wn
        o_w = jnp.dot(pltpu.roll(p_w, groups, 1).astype(BF16), xw, preferred_element_type=F32)
        o_w = (o_w + p_wn * new_token_values(newwin_ref)) / l_w

        winout_ref[0, 0:n_buf - 1] = win_ref[0, 1:n_buf]
        for j in range(n_rows):
            winout_ref[0, n_buf - 1, j:j + 1, :] = newwin_ref[0, :, j * HEAD_DIM:(j + 1) * HEAD_DIM]

        graw = jnp.broadcast_to(g_ref[0], (heads, groups * LANES))
        glane = lax.broadcasted_iota(I32, (heads, groups * LANES), 1)
        hidx = lax.broadcasted_iota(I32, (heads, 1), 0)
        gbase = (hidx // rep) * LANES + 3 * (hidx % rep)
        gate = lambda c: _sigmoid(jnp.sum(jnp.where(glane == gbase + c, graw, 0.0), axis=-1, keepdims=True))
        o = gate(0) * oc_ref[0] + gate(1) * o_s + gate(2) * o_w
        for h in range(heads):
            z = z_ref[0, :, h * HEAD_DIM:(h + 1) * HEAD_DIM]
            act_ref[0, :, h * HEAD_DIM:(h + 1) * HEAD_DIM] = (o[h:h + 1, :] * _silu(z)).astype(BF16)


def _nsa_sample_attend(page_table, sel_idx, q3, cache5, layer, new_sel, new_win, o_c, win_state, gates3,
                       z3, *, groups, rep, n_gather):
    db, heads, _ = q3.shape
    kvw = groups * HEAD_DIM
    n_buf = win_state.shape[2]
    n_rows = 2 * groups
    bpp = PAGE_SIZE // CMP_BLOCK
    assert sel_idx.shape[2] == n_gather
    per_step = max(u for u in range(1, 6) if n_gather % u == 0)
    n_steps = n_gather // per_step

    def gather_spec(g, u):
        def index_map(bi, k, pt, si):
            blk = si[bi, g, k * per_step + u]
            return (layer, pt[bi, blk // bpp], blk % bpp, 0, 0)
        return pl.BlockSpec((None, None, CMP_BLOCK, n_rows, HEAD_DIM), index_map)

    in_specs = [pl.BlockSpec((1, heads, HEAD_DIM), lambda bi, k, pt, si: (bi, 0, 0))]
    in_specs += [gather_spec(g, u) for g in range(groups) for u in range(per_step)]
    row3 = lambda w: pl.BlockSpec((1, 1, w), lambda bi, k, pt, si: (bi, 0, 0))
    in_specs += [
        row3(2 * kvw), row3(2 * kvw),
        pl.BlockSpec((1, heads, HEAD_DIM), lambda bi, k, pt, si: (bi, 0, 0)),
        pl.BlockSpec((None, 1, n_buf, n_rows, HEAD_DIM), lambda bi, k, pt, si: (layer, bi, 0, 0, 0)),
        row3(groups * LANES), row3(heads * HEAD_DIM),
    ]
    grid_spec = pltpu.PrefetchScalarGridSpec(
        num_scalar_prefetch=2,
        grid=(db, n_steps),
        in_specs=in_specs,
        out_specs=[
            row3(heads * HEAD_DIM),
            pl.BlockSpec((1, n_buf, n_rows, HEAD_DIM), lambda bi, k, pt, si: (bi, 0, 0, 0)),
        ],
        scratch_shapes=[pltpu.VMEM((heads, 1), F32), pltpu.VMEM((heads, 1), F32),
                        pltpu.VMEM((heads, HEAD_DIM), F32)],
    )
    kern = functools.partial(_nsa_sample_attend_kernel, groups=groups, rep=rep, n_steps=n_steps,
                             per_step=per_step, n_buf=n_buf)
    return pl.pallas_call(
        kern,
        grid_spec=grid_spec,
        out_shape=[
            jax.ShapeDtypeStruct((db, 1, heads * HEAD_DIM), BF16),
            jax.ShapeDtypeStruct((db, n_buf, n_rows, HEAD_DIM), F32),
        ],
        compiler_params=_cparams(32, ("arbitrary", "arbitrary")),
        name="nsa_sample_attend",
    )(page_table, sel_idx, q3, *([cache5] * (groups * per_step)), new_sel, new_win, o_c, win_state,
      gates3, z3)


def _rope_tables(pos):
    half = ROT_DIM // 2
    inv_freq = ROPE_THETA ** (-jnp.arange(half, dtype=F32) * (2.0 / ROT_DIM))
    ang = pos.astype(F32)[:, None] * inv_freq[None, :]
    cos, sin = jnp.cos(ang), jnp.sin(ang)
    n = pos.shape[0]
    cos_t = jnp.concatenate([cos, cos, jnp.ones((n, HEAD_DIM - ROT_DIM), F32)], axis=1)
    sin_t = jnp.concatenate([-sin, sin, jnp.zeros((n, HEAD_DIM - ROT_DIM), F32)], axis=1)
    return cos_t, sin_t


def _odd_weight_layout(w, cw, kvw, rep):
    n_layers, k, _ = w.shape
    groups = kvw // HEAD_DIM
    n_gate = 3 * groups * rep
    gates = w[:, :, :n_gate].reshape(n_layers, k, groups, 3 * rep)
    gates = jnp.pad(gates, ((0, 0), (0, 0), (0, 0), (0, LANES - 3 * rep))).reshape(n_layers, k, groups * LANES)
    return jnp.concatenate([w[:, :, n_gate:], gates], axis=2)


def kernel(x_prompt, x_sample, cache_cmp_kv, cache_sel_kv, state_win_kv, state_pool, page_table,
           norm_even, w_in_even, v_norm, w_spatial, b_spatial, w_pool, pool_scale, w_out_even,
           norm_odd, w_in_odd, q_norm, k_norm, w_out_odd):
    bsz, seq, d_model = x_prompt.shape
    db, dec_t, _ = x_sample.shape
    assert dec_t == 1
    n_even, n_odd = norm_even.shape[0], norm_odd.shape[0]
    depth = n_even + n_odd
    n_pages = page_table.shape[1]
    past_len = n_pages * PAGE_SIZE
    aw = v_norm.shape[1]
    bw = pool_scale.shape[1]
    groups = C_KV_HEADS
    kvw = groups * HEAD_DIM
    cw = w_out_odd.shape[1]
    rep = cw // kvw
    heads = cw // HEAD_DIM
    n_phys = cache_cmp_kv.shape[1]
    n_buf = state_win_kv.shape[2]
    assert past_len % CMP_BLOCK == 0 and seq % CMP_BLOCK == 0
    nb_p = seq // CMP_BLOCK
    nb_past = past_len // CMP_BLOCK
    n_past_sel = min(N_SEL, nb_past + 1) - 1
    assert nb_past >= 2 and n_past_sel >= 2

    xp = x_prompt.reshape(bsz * seq, d_model)
    xs = x_sample.reshape(db, d_model)

    cos_p, sin_p = _rope_tables(jnp.arange(seq, dtype=I32))
    cos_s, sin_s = _rope_tables(jnp.full((db,), past_len, dtype=I32))
    tables = _mask_tables(seq, LANES, WINDOW + LANES, rep)

    cache_cmp5 = cache_cmp_kv.reshape(n_odd, n_phys, PAGE_SIZE, 2 * groups, HEAD_DIM)
    cache_sel5 = cache_sel_kv.reshape(n_odd, n_phys, PAGE_SIZE, 2 * groups, HEAD_DIM)
    win_state5 = state_win_kv.reshape(n_odd, db, n_buf, 2 * groups, HEAD_DIM)

    tn_in = 2560
    n_even_in = w_in_even.shape[2]
    n_qkv = cw + 6 * kvw
    assert n_even_in % tn_in == 0 and n_qkv % tn_in == 0
    w_in_even_b = w_in_even.astype(BF16)
    w_out_even_b = w_out_even.astype(BF16)
    w_in_odd_b = w_in_odd.astype(BF16)
    w_out_odd_b = w_out_odd.astype(BF16)
    w_zg_b = _odd_weight_layout(w_in_odd_b[:, :, n_qkv:], cw, kvw, rep)
    n_zg = w_zg_b.shape[2]

    cmp_p, cmp_s, sel_p, sel_s, win_p, win_s = [], [], [], [], [], []
    pool_p, pool_s, gv_s = [], [], []
    for layer in range(depth):
        li = layer // 2
        if layer % 2 == 0:
            g_in = norm_even[li][None, :]
            vg = v_norm[li][None, :]
            ps = pool_scale[li][None, :]
            wp = w_pool[li].astype(BF16)
            proj = _norm_matmul(xp, g_in, w_in_even_b, li, n_even_in, tn=tn_in)
            y3, pool16 = _even_mix(proj.reshape(bsz, seq, -1), xp.reshape(bsz, seq, d_model), vg,
                                   w_spatial[li], b_spatial[li].T, wp, ps, w_out_even_b, li)
            xp = y3.reshape(bsz * seq, d_model)
            pool_p.append(pool16[:, 16 - POOL_PAD:])
            proj_s = _norm_matmul(xs, g_in, w_in_even_b, li, n_even_in, tn=tn_in)
            hd = aw // A_HEADS
            w00 = jnp.repeat(w_spatial[li][:, 0, 0], hd)[None, :]
            b0 = jnp.repeat(b_spatial[li][:, 0], hd)[None, :]
            act_s, vn_s, new_state = _even_mix_sample(
                proj_s, vg, w00, b0, wp, ps, jnp.swapaxes(state_pool[li], 0, 1), past_len)
            xs = _matmul_residual(act_s, w_out_even_b, li, xs)
            pool_s.append(jnp.swapaxes(new_state, 0, 1))
            gv_s.append(vn_s.reshape(db, 1, aw))
        else:
            g_in = norm_odd[li][None, :]
            qg = q_norm[li][None, :]
            kg = k_norm[li]
            proj = _norm_matmul(xp, g_in, w_in_odd_b, li, n_qkv, tn=tn_in)
            proj_zg = _norm_matmul(xp, g_in, w_zg_b, li, n_zg, tn=n_zg)
            q_b, kv_cmp, kv_sel, kv_win, kt_sel, v_sel, kt_win, v_win, means = _odd_post(
                proj, cos_p, sin_p, qg, kg, cw=cw, kvw=kvw, for_prompt=True, win_keep=min(WINDOW, seq))
            r3 = lambda a: a.reshape(bsz, seq, a.shape[-1])
            act, means_s = _nsa_prompt(r3(q_b), kt_sel, r3(v_sel), kt_win, r3(v_win),
                                       means.reshape(bsz, nb_p, 2 * kvw), r3(proj_zg), tables,
                                       cache_cmp5, page_table, li, cw=cw, z_off=0, g_off=cw)
            xp = _matmul_residual(act.reshape(bsz * seq, cw), w_out_odd_b, li, xp)
            kv6 = lambda a, n: a.reshape(n, -1, 2, groups, HEAD_DIM)
            cmp_p.append(kv6(kv_cmp, bsz))
            sel_p.append(kv6(kv_sel, bsz))
            win_p.append(kv6(kv_win, bsz))
            proj_s = _norm_matmul(xs, g_in, w_in_odd_b, li, n_qkv, tn=tn_in)
            proj_zg_s = _norm_matmul(xs, g_in, w_zg_b, li, n_zg, tn=n_zg)
            q_s, kvc_s, kvs_s, kvw_s = _odd_post(
                proj_s, cos_s, sin_s, qg, kg, cw=cw, kvw=kvw, for_prompt=False)
            q3s = q_s.reshape(db, heads, HEAD_DIM)
            o_c, idx = _nsa_sample_select(q3s, means_s, groups=groups, rep=rep, q_pos=past_len,
                                          n_past_sel=n_past_sel)
            act_s, win_new = _nsa_sample_attend(
                page_table, idx[:, :, :n_past_sel], q3s, cache_sel5, li,
                kvs_s.reshape(db, 1, 2 * kvw), kvw_s.reshape(db, 1, 2 * kvw), o_c, win_state5,
                proj_zg_s[:, cw:].reshape(db, 1, groups * LANES),
                proj_zg_s[:, :cw].reshape(db, 1, cw),
                groups=groups, rep=rep, n_gather=n_past_sel)
            xs = _matmul_residual(act_s.reshape(db, cw), w_out_odd_b, li, xs)
            cmp_s.append(kv6(kvc_s, db))
            sel_s.append(kv6(kvs_s, db))
            win_s.append(win_new.reshape(db, n_buf, 2, groups, HEAD_DIM))
    return (xp.reshape(bsz, seq, d_model), xs.reshape(db, 1, d_model),
            jnp.stack(cmp_p), jnp.stack(cmp_s), jnp.stack(sel_p), jnp.stack(sel_s),
            jnp.stack(win_p), jnp.stack(win_s), jnp.stack(pool_p), jnp.stack(pool_s), jnp.stack(gv_s))
```

```python
import functools

import jax
import jax.numpy as jnp
import numpy as np
from jax import lax
from jax.experimental import pallas as pl
from jax.experimental.pallas import tpu as pltpu

F32 = jnp.float32
BF16 = jnp.bfloat16
I32 = jnp.int32

EPS = 1e-6
PAGE_SIZE = 128
A_HEADS = 8
GMLP_CHUNK = 128
POOL_WINDOWS = (2, 4, 8, 16)
POOL_PAD = max(POOL_WINDOWS) - 1
HEAD_DIM = 128
C_KV_HEADS = 4
CMP_BLOCK = 64
N_SEL = 16
WINDOW = 512
SEL_FORCE = 1e4
NEG_INF = -1e30
ROPE_THETA = 500000.0
ROT_DIM = HEAD_DIM // 4
LOG2E = 1.4426950408889634
Q_SCALE = HEAD_DIM ** -0.5 * LOG2E

LANES = 128
SUBLANES = 8
MIB = 1024 * 1024


def _cparams(vmem_mib, semantics=None):
    return pltpu.CompilerParams(vmem_limit_bytes=int(vmem_mib * MIB), dimension_semantics=semantics)


def _silu(x):
    return x * (1.0 / (1.0 + jnp.exp(-x)))


def _sigmoid(x):
    return 1.0 / (1.0 + jnp.exp(-x))


def _dot_nt(a, b):
    return lax.dot_general(a, b, (((1,), (1,)), ((), ())), preferred_element_type=F32)


def _dot(a, b):
    return jnp.dot(a, b, preferred_element_type=F32)


def _norm_mm_kernel(x_ref, g_ref, w_ref, o_ref):
    x = x_ref[...]
    ms = jnp.mean(x * x, axis=-1, keepdims=True)
    h = (x * lax.rsqrt(ms + EPS) * g_ref[...]).astype(BF16)
    o_ref[...] = jnp.dot(h, w_ref[...], preferred_element_type=F32)


def _mm_res_kernel(a_ref, w_ref, r_ref, o_ref):
    o_ref[...] = r_ref[...] + jnp.dot(a_ref[...], w_ref[...], preferred_element_type=F32)


def _row_tile(m, want):
    return want if m % want == 0 else m


def _norm_matmul(x, g, w, layer, n, *, tn, tm=512):
    m, k = x.shape
    tm = _row_tile(m, tm)
    assert n % tn == 0 and n <= w.shape[2]
    vmem = 2 * (tm * k * 4 + k * tn * 2 + tm * tn * 4) / MIB + 8
    return pl.pallas_call(
        _norm_mm_kernel,
        grid=(n // tn, m // tm),
        in_specs=[
            pl.BlockSpec((tm, k), lambda j, i: (i, 0)),
            pl.BlockSpec((1, k), lambda j, i: (0, 0)),
            pl.BlockSpec((None, k, tn), lambda j, i: (layer, 0, j)),
        ],
        out_specs=pl.BlockSpec((tm, tn), lambda j, i: (i, j)),
        out_shape=jax.ShapeDtypeStruct((m, n), F32),
        compiler_params=_cparams(vmem),
        name="norm_matmul",
    )(x, g, w)


def _matmul_residual(a, w, layer, res, *, tm=512):
    m, k = a.shape
    n = w.shape[2]
    tm = _row_tile(m, tm)
    vmem = 2 * (tm * k * 2 + k * n * 2 + 2 * tm * n * 4) / MIB + 8
    return pl.pallas_call(
        _mm_res_kernel,
        grid=(m // tm,),
        in_specs=[
            pl.BlockSpec((tm, k), lambda i: (i, 0)),
            pl.BlockSpec((None, k, n), lambda i: (layer, 0, 0)),
            pl.BlockSpec((tm, n), lambda i: (i, 0)),
        ],
        out_specs=pl.BlockSpec((tm, n), lambda i: (i, 0)),
        out_shape=jax.ShapeDtypeStruct((m, n), F32),
        compiler_params=_cparams(vmem),
        name="matmul_residual",
    )(a, w, res)


def _even_mix_kernel(proj_ref, x_ref, vg_ref, ws_ref, bst_ref, wp_ref, ps_ref, wout_ref,
                     y_ref, pool_ref, ext_ref, act_ref, *, tm, aw, bw):
    c = pl.program_id(1)
    n_c = pl.num_programs(1)
    hd = aw // A_HEADS
    pg = bw // len(POOL_WINDOWS)

    cl = GMLP_CHUNK
    row = lax.broadcasted_iota(I32, (cl, cl), 0)
    col = lax.broadcasted_iota(I32, (cl, cl), 1)
    causal = row >= col

    for h in range(A_HEADS):
        sl = slice(h * hd, (h + 1) * hd)
        w = jnp.where(causal, ws_ref[h], 0.0).astype(BF16)
        for c0 in range(0, tm, cl):
            rs = slice(c0, c0 + cl)
            u = proj_ref[0, rs, sl]
            v = proj_ref[0, rs, aw + h * hd:aw + (h + 1) * hd]
            za = proj_ref[0, rs, 2 * aw + h * hd:2 * aw + (h + 1) * hd]
            ms = jnp.mean(v * v, axis=-1, keepdims=True)
            vn = v * lax.rsqrt(ms + EPS) * vg_ref[:, sl]
            s = jnp.dot(w, vn.astype(BF16), preferred_element_type=F32) + bst_ref[:, h:h + 1]
            act_ref[0, rs, sl] = ((u * s) * _silu(za)).astype(BF16)

    p_off = 3 * aw
    zb_off = 3 * aw + bw

    @pl.when(c == 0)
    def _():
        ext_ref[0:16, :] = jnp.zeros((16, bw), F32)

    ext_ref[16:16 + tm, :] = proj_ref[0, :, p_off:p_off + bw]
    pos = c * tm + lax.broadcasted_iota(I32, (tm, 1), 0)
    for g, wnd in enumerate(POOL_WINDOWS):
        sl = slice(g * pg, (g + 1) * pg)
        cur = ext_ref[16:16 + tm, sl]
        acc = cur
        for k in range(1, wnd):
            acc = acc + ext_ref[16 - k:16 - k + tm, sl]
        cnt = jnp.minimum(pos + 1, wnd).astype(F32)
        d = acc / cnt - cur
        y = jnp.dot(d.astype(BF16), wp_ref[g], preferred_element_type=F32)
        zb = proj_ref[0, :, zb_off + g * pg:zb_off + (g + 1) * pg]
        act_ref[0, :, aw + g * pg:aw + (g + 1) * pg] = ((y * ps_ref[:, sl]) * _silu(zb)).astype(BF16)

    tail = ext_ref[tm:tm + 16, :]
    ext_ref[0:16, :] = tail

    @pl.when(c == n_c - 1)
    def _():
        pool_ref[0] = tail

    y_ref[0] = x_ref[0] + jnp.dot(act_ref[0], wout_ref[...], preferred_element_type=F32)


def _even_mix(proj3, x3, v_gain, w_s, b_st, w_pool, pool_scale, w_out, layer):
    b, s, _ = proj3.shape
    d = x3.shape[2]
    aw = v_gain.shape[1]
    bw = pool_scale.shape[1]
    tm = 2 * GMLP_CHUNK if s % (2 * GMLP_CHUNK) == 0 else GMLP_CHUNK
    assert s % tm == 0 and s >= 16
    n_in = 3 * aw + 2 * bw
    kern = functools.partial(_even_mix_kernel, tm=tm, aw=aw, bw=bw)
    return pl.pallas_call(
        kern,
        grid=(b, s // tm),
        in_specs=[
            pl.BlockSpec((1, tm, n_in), lambda i, c: (i, c, 0)),
            pl.BlockSpec((1, tm, d), lambda i, c: (i, c, 0)),
            pl.BlockSpec((1, aw), lambda i, c: (0, 0)),
            pl.BlockSpec(w_s.shape, lambda i, c: (0, 0, 0)),
            pl.BlockSpec(b_st.shape, lambda i, c: (0, 0)),
            pl.BlockSpec(w_pool.shape, lambda i, c: (0, 0, 0)),
            pl.BlockSpec((1, bw), lambda i, c: (0, 0)),
            pl.BlockSpec((None, aw + bw, d), lambda i, c: (layer, 0, 0)),
        ],
        out_specs=[
            pl.BlockSpec((1, tm, d), lambda i, c: (i, c, 0)),
            pl.BlockSpec((1, 16, bw), lambda i, c: (i, 0, 0)),
        ],
        out_shape=[
            jax.ShapeDtypeStruct((b, s, d), F32),
            jax.ShapeDtypeStruct((b, 16, bw), F32),
        ],
        scratch_shapes=[pltpu.VMEM((16 + tm, bw), F32), pltpu.VMEM((1, tm, aw + bw), BF16)],
        compiler_params=_cparams(48, ("arbitrary", "arbitrary")),
        name="even_mix",
    )(proj3, x3, v_gain, w_s, b_st, w_pool, pool_scale, w_out)


def _even_mix_sample_kernel(proj_ref, vg_ref, w00_ref, b0_ref, wp_ref, ps_ref, st_ref,
                            act_ref, vn_ref, newst_ref, *, aw, bw, pos):
    hd = aw // A_HEADS
    pg = bw // len(POOL_WINDOWS)
    for h in range(A_HEADS):
        sl = slice(h * hd, (h + 1) * hd)
        u = proj_ref[:, sl]
        v = proj_ref[:, aw + h * hd:aw + (h + 1) * hd]
        za = proj_ref[:, 2 * aw + h * hd:2 * aw + (h + 1) * hd]
        ms = jnp.mean(v * v, axis=-1, keepdims=True)
        vn = v * lax.rsqrt(ms + EPS) * vg_ref[:, sl]
        vn_ref[:, sl] = vn
        s = w00_ref[:, sl] * vn + b0_ref[:, sl]
        act_ref[:, sl] = ((u * s) * _silu(za)).astype(BF16)

    p = proj_ref[:, 3 * aw:3 * aw + bw]
    for g, wnd in enumerate(POOL_WINDOWS):
        sl = slice(g * pg, (g + 1) * pg)
        cur = p[:, sl]
        acc = cur
        for k in range(1, wnd):
            acc = acc + st_ref[POOL_PAD - k, :, sl]
        cnt = float(min(pos + 1, wnd))
        d = acc / cnt - cur
        y = jnp.dot(d.astype(BF16), wp_ref[g], preferred_element_type=F32)
        zb = proj_ref[:, 3 * aw + bw + g * pg:3 * aw + bw + (g + 1) * pg]
        act_ref[:, aw + g * pg:aw + (g + 1) * pg] = ((y * ps_ref[:, sl]) * _silu(zb)).astype(BF16)

    for k in range(POOL_PAD - 1):
        newst_ref[k] = st_ref[k + 1]
    newst_ref[POOL_PAD - 1] = p


def _even_mix_sample(proj, v_gain, w00, b0, w_pool, pool_scale, state_t, pos):
    db = proj.shape[0]
    aw = v_gain.shape[1]
    bw = pool_scale.shape[1]
    kern = functools.partial(_even_mix_sample_kernel, aw=aw, bw=bw, pos=pos)
    return pl.pallas_call(
        kern,
        out_shape=[
            jax.ShapeDtypeStruct((db, aw + bw), BF16),
            jax.ShapeDtypeStruct((db, aw), F32),
            jax.ShapeDtypeStruct((POOL_PAD, db, bw), F32),
        ],
        name="even_mix_sample",
    )(proj, v_gain, w00, b0, w_pool, pool_scale, state_t)


def _odd_post_kernel(proj_ref, cos_ref, sin_ref, qg_ref, kg_ref, *out_refs, tm, cw, kvw, for_prompt):
    q_out, cmp_out, sel_out, win_out = out_refs[:4]
    if for_prompt:
        selkt_out, selv_out, winkt_out, winv_out, means_out = out_refs[4:]
        attn_outs = (None, (selkt_out, selv_out), (winkt_out, winv_out))
    else:
        attn_outs = (None, None, None)
    cosf = cos_ref[...]
    sinf = sin_ref[...]
    half = ROT_DIM // 2
    first = lax.broadcasted_iota(I32, (tm, HEAD_DIM), 1) < half

    ones_d = jnp.full((HEAD_DIM, HEAD_DIM), 1.0 / HEAD_DIM, BF16)

    def norm_rope(x, gain):
        ms = _dot((x * x).astype(BF16), ones_d)
        y = x * lax.rsqrt(ms + EPS) * gain
        rot = jnp.where(first, pltpu.roll(y, HEAD_DIM - half, 1), pltpu.roll(y, half, 1))
        return y * cosf + rot * sinf

    qg = qg_ref[...]
    for h in range(cw // HEAD_DIM):
        sl = slice(h * HEAD_DIM, (h + 1) * HEAD_DIM)
        q_out[:, sl] = (norm_rope(proj_ref[:, sl], qg) * Q_SCALE).astype(BF16)

    for br, (o32, attn) in enumerate(zip((cmp_out, sel_out, win_out), attn_outs)):
        k_off = cw + br * 2 * kvw
        v_off = k_off + kvw
        kg = kg_ref[br:br + 1, :]
        pieces = []
        for g in range(kvw // HEAD_DIM):
            sl = slice(g * HEAD_DIM, (g + 1) * HEAD_DIM)
            kk = norm_rope(proj_ref[:, k_off + g * HEAD_DIM:k_off + (g + 1) * HEAD_DIM], kg)
            pieces.append(kk)
            if attn is not None:
                attn[0][sl, :] = kk.T.astype(BF16)
        vv = proj_ref[:, v_off:v_off + kvw]
        if attn is not None:
            attn[1][...] = vv.astype(BF16)
        row = jnp.concatenate(pieces + [vv], axis=1)
        o32[...] = row.reshape(o32.shape)
        if br == 0 and for_prompt:
            for j in range(tm // CMP_BLOCK):
                blk = row[j * CMP_BLOCK:(j + 1) * CMP_BLOCK, :]
                means_out[0, j:j + 1, :] = jnp.mean(blk, axis=0, keepdims=True)


def _odd_post(proj, cos_t, sin_t, q_gain, k_gain, *, cw, kvw, for_prompt, win_keep=0, tm=512):
    m = proj.shape[0]
    tm = _row_tile(m, tm)
    n_pos_tiles = cos_t.shape[0] // tm
    n_used = cw + 6 * kvw
    kern = functools.partial(_odd_post_kernel, tm=tm, cw=cw, kvw=kvw, for_prompt=for_prompt)
    row_spec = lambda w: pl.BlockSpec((tm, w), lambda i: (i, 0))
    n_rows = 2 * kvw // HEAD_DIM
    if for_prompt:
        assert win_keep % tm == 0 and cos_t.shape[0] >= win_keep
        keep_tiles = win_keep // tm
        skip_tiles = n_pos_tiles - keep_tiles
        kv_shape = jax.ShapeDtypeStruct((m, n_rows, HEAD_DIM), F32)
        kv_spec = pl.BlockSpec((tm, n_rows, HEAD_DIM), lambda i: (i, 0, 0))
        win_shape = jax.ShapeDtypeStruct((m // n_pos_tiles * keep_tiles, n_rows, HEAD_DIM), F32)
        win_spec = pl.BlockSpec(
            (tm, n_rows, HEAD_DIM),
            lambda i: (i // n_pos_tiles * keep_tiles + jnp.maximum(i % n_pos_tiles - skip_tiles, 0), 0, 0))
    else:
        kv_shape = win_shape = jax.ShapeDtypeStruct((m, 2 * kvw), F32)
        kv_spec = win_spec = row_spec(2 * kvw)
    out_shape = [jax.ShapeDtypeStruct((m, cw), BF16), kv_shape, kv_shape, win_shape]
    out_specs = [row_spec(cw), kv_spec, kv_spec, win_spec]
    if for_prompt:
        assert tm % CMP_BLOCK == 0 and tm % LANES == 0
        for _ in range(2):
            out_shape += [jax.ShapeDtypeStruct((kvw, m), BF16), jax.ShapeDtypeStruct((m, kvw), BF16)]
            out_specs += [pl.BlockSpec((kvw, tm), lambda i: (0, i)), row_spec(kvw)]
        out_shape.append(jax.ShapeDtypeStruct((m // tm, tm // CMP_BLOCK, 2 * kvw), F32))
        out_specs.append(pl.BlockSpec((1, tm // CMP_BLOCK, 2 * kvw), lambda i: (i, 0, 0)))
    return pl.pallas_call(
        kern,
        grid=(m // tm,),
        in_specs=[
            pl.BlockSpec((tm, n_used), lambda i: (i, 0)),
            pl.BlockSpec((tm, HEAD_DIM), lambda i: (i % n_pos_tiles, 0)),
            pl.BlockSpec((tm, HEAD_DIM), lambda i: (i % n_pos_tiles, 0)),
            pl.BlockSpec((1, HEAD_DIM), lambda i: (0, 0)),
            pl.BlockSpec((3, HEAD_DIM), lambda i: (0, 0)),
        ],
        out_specs=out_specs,
        out_shape=out_shape,
        compiler_params=_cparams(56),
        name="odd_post",
    )(proj, cos_t, sin_t, q_gain, k_gain)


def _nsa_sub_tile(i, h, q_ref, ks_ref, vs_ref, kw_ref, vw_ref, kc, vc_t, band_ref, tri_ref, eye_ref,
                  *, tq, nb, rep, lw):
    r0 = h * tq
    q = jnp.concatenate([q_ref[0, r0:r0 + tq, r * HEAD_DIM:(r + 1) * HEAD_DIM] for r in range(rep)],
                        axis=0)
    t_row = i * tq + lax.broadcasted_iota(I32, (1, tq), 1)

    blk_col = lax.broadcasted_iota(I32, (nb, 1), 0)
    s_t = _dot_nt(kc, q)
    ok_t = ((blk_col + 1) * CMP_BLOCK - 1) <= t_row
    bias_t = jnp.where(ok_t, 0.0, NEG_INF)
    okf_t = jnp.where(ok_t, 1.0, 0.0)
    imp = jnp.zeros((nb, tq), F32)
    o_c = []
    for r in range(rep):
        s_r = s_t[:, r * tq:(r + 1) * tq] + bias_t
        e_r = jnp.exp2(s_r - jnp.max(s_r, axis=0, keepdims=True))
        p_r = e_r / jnp.sum(e_r, axis=0, keepdims=True) * okf_t
        imp = imp + p_r
        if LANES > nb:
            p_r = jnp.concatenate([p_r, jnp.zeros((LANES - nb, tq), F32)], axis=0)
        o_c.append(_dot(vc_t, p_r.astype(BF16)).T)
    o_c = jnp.concatenate(o_c, axis=0)
    cur = t_row // CMP_BLOCK
    forced = (blk_col == 0) | (blk_col == cur) | (blk_col == cur - 1)
    imp = jnp.where(blk_col <= cur, jnp.where(forced, SEL_FORCE, imp), -1.0)

    blk8 = lax.broadcasted_iota(I32, (SUBLANES, 1), 0)
    cnt = jnp.zeros((nb, tq), F32)
    for j in range(nb):
        rj = imp[j:j + 1, :]
        parts = []
        for v0 in range(0, nb, SUBLANES):
            sub = imp[v0:v0 + SUBLANES, :]
            if v0 > j:
                beats = rj >= sub
            elif v0 + SUBLANES - 1 <= j:
                beats = rj > sub
            else:
                beats = (rj > sub) | ((rj == sub) & (blk8 > j - v0))
            parts.append(jnp.where(beats, 1.0, 0.0))
        cnt = cnt + jnp.concatenate(parts, axis=0)
    keep = (cnt < min(N_SEL, nb)) & (imp >= 0.0) & (blk_col < i * (tq // CMP_BLOCK))
    drop_t = jnp.where(keep, 0.0, 1.0)
    if LANES > nb:
        drop_t = jnp.concatenate([drop_t, jnp.zeros((LANES - nb, tq), F32)], axis=0)
    drop = drop_t.T.astype(BF16)
    q_drop = jnp.concatenate([q, jnp.concatenate([drop] * rep, axis=0)], axis=1)

    q_eye = jnp.concatenate([q, eye_ref[...]], axis=1)

    w0 = pl.multiple_of(jnp.maximum(i * tq + tq - lw, 0), tq)
    band = band_ref[jnp.minimum(i, band_ref.shape[0] - 1)]
    k_w = jnp.concatenate([kw_ref[:, pl.ds(w0, lw)], band], axis=0)
    v_w = jnp.concatenate([vw_ref[0, pl.ds(w0, lw), :], jnp.ones((lw, HEAD_DIM), BF16)], axis=1)
    s_w = _dot(q_eye, k_w)
    p_w = jnp.exp2(s_w - jnp.max(s_w, axis=-1, keepdims=True))
    acc_w = _dot(p_w.astype(BF16), v_w)
    o_w = acc_w[:, :HEAD_DIM] / acc_w[:, HEAD_DIM:]

    d0 = pl.multiple_of(i * tq, tq)
    k_d = jnp.concatenate([ks_ref[:, pl.ds(d0, tq)], tri_ref[...]], axis=0)
    v_d = jnp.concatenate([vs_ref[0, pl.ds(d0, tq), :], jnp.ones((tq, HEAD_DIM), BF16)], axis=1)
    s_d = _dot(q_eye, k_d)
    m_d = jnp.max(s_d, axis=-1, keepdims=True)
    acc_d = _dot(jnp.exp2(s_d - m_d).astype(BF16), v_d)
    return q_drop, o_c, o_w, m_d, acc_d


def _nsa_prompt_kernel(pt_ref, q_ref, ks_ref, vs_ref, kw_ref, vw_ref, kc_ref, vc_ref, g_ref, z_ref,
                       eb_ref, band_ref, tri_ref, eye_ref, *rest, tq, nh, tk, seq, nb, rep, lw, pps):
    if pps:
        o_ref, pm_ref, s0_ref, s1_ref = rest[pps:]
        _store_page_means(rest[:pps], pm_ref)
    else:
        o_ref, s0_ref, s1_ref = rest
    step = pl.program_id(2)
    kc = kc_ref[0].astype(BF16)
    vc = vc_ref[0]
    if LANES > nb:
        vc = jnp.concatenate([vc, jnp.zeros((LANES - nb, HEAD_DIM), F32)], axis=0)
    vc_t = vc.T.astype(BF16)
    subs = [_nsa_sub_tile(step * nh + h, h, q_ref, ks_ref, vs_ref, kw_ref, vw_ref, kc, vc_t, band_ref,
                          tri_ref, eye_ref, tq=tq, nb=nb, rep=rep, lw=lw) for h in range(nh)]

    q_drop = jnp.concatenate([s[0] for s in subs], axis=0)
    m_0 = jnp.concatenate([s[3] for s in subs], axis=0)
    acc_0 = jnp.concatenate([s[4] for s in subs], axis=0)

    partial, gate_s, z_act = [], [], []
    for h in range(nh):
        ts = slice(h * tq, (h + 1) * tq)
        gs = _sigmoid(g_ref[0, ts, :])
        o_c, o_w = subs[h][1], subs[h][2]
        for r in range(rep):
            rs = slice(r * tq, (r + 1) * tq)
            partial.append(gs[:, 3 * r:3 * r + 1] * o_c[rs] + gs[:, 3 * r + 2:3 * r + 3] * o_w[rs])
            gate_s.append(gs[:, 3 * r + 1:3 * r + 2])
            z_act.append(_silu(z_ref[0, ts, r * HEAD_DIM:(r + 1) * HEAD_DIM]))

    ones_k = jnp.ones((tk, HEAD_DIM), BF16)
    n_tiles = seq // tk

    def tile_scores(t):
        k0 = pl.multiple_of(t * tk, tk)
        k = jnp.concatenate([ks_ref[:, pl.ds(k0, tk)], eb_ref[:, pl.ds(k0, tk)]], axis=0)
        return _dot(q_drop, k)

    def accumulate(carry, s, t):
        m_p, acc = carry
        k0 = pl.multiple_of(t * tk, tk)
        v = jnp.concatenate([vs_ref[0, pl.ds(k0, tk), :], ones_k], axis=1)
        m_n = jnp.maximum(m_p, jnp.max(s, axis=-1, keepdims=True))
        pv = _dot(jnp.exp2(s - m_n).astype(BF16), v)
        return m_n, jnp.exp2(m_p - m_n) * acc + pv

    def sel_body(j, carry):
        t1 = 2 * j + 1
        s1_ref[...] = tile_scores(t1)
        carry = accumulate(carry, s0_ref[...], 2 * j)
        s0_ref[...] = tile_scores(jnp.minimum(t1 + 1, n_tiles - 1))
        return accumulate(carry, s1_ref[...], t1)

    n_past = ((step * nh + nh - 1) * tq + tk - 1) // tk
    s0_ref[...] = tile_scores(0)
    _, acc_s = lax.fori_loop(0, (n_past + 1) // 2, sel_body, (m_0, acc_0))
    o_s = acc_s[:, :HEAD_DIM] / acc_s[:, HEAD_DIM:]

    for h in range(nh):
        for r in range(rep):
            n = h * rep + r
            o = partial[n] + gate_s[n] * o_s[n * tq:(n + 1) * tq]
            o_ref[0, h * tq:(h + 1) * tq, r * HEAD_DIM:(r + 1) * HEAD_DIM] = (o * z_act[n]).astype(BF16)


def _mask_tables(seq, tq, lw, rep):
    neg = lambda ok: jnp.where(ok, 0.0, NEG_INF).astype(BF16)
    row = jnp.arange(LANES, dtype=I32)[:, None]
    key = jnp.arange(seq, dtype=I32)[None, :]
    block_rows = neg(key // CMP_BLOCK != row)
    t = jnp.arange(tq, dtype=I32)[None, :, None]
    c = jnp.arange(lw, dtype=I32)[None, None, :]
    off = jnp.arange(WINDOW // tq + 1, dtype=I32)[:, None, None] * tq
    rel = off + t - c
    band = neg((rel >= 0) & (rel < WINDOW))
    tri = neg(jnp.arange(tq, dtype=I32)[None, :] <= jnp.arange(tq, dtype=I32)[:, None])
    eye = jnp.tile(jnp.eye(tq, dtype=BF16), (rep, 1))
    return block_rows, band, tri, eye


def _page_job_split(n_steps, page_table):
    db, n_pages = page_table.shape
    total = db * n_pages
    if total % n_steps:
        return 0
    pps = total // n_steps
    return pps if n_pages % pps == 0 else 0


def _nsa_prompt(q3, kt_sel, v_sel, kt_win, v_win, means3, proj3, tables, cache5, page_table, layer, *,
                cw, z_off, g_off, tq=LANES, nh=2, tk=512):
    b, s, _ = q3.shape
    kvw = v_sel.shape[2]
    groups = kvw // HEAD_DIM
    rep = cw // kvw
    nb = means3.shape[1]
    gw = rep * HEAD_DIM
    block_rows, band, tri, eye = tables
    tk = min(tk, s)
    lw = band.shape[2]
    ts = nh * tq
    assert tq == LANES and nb <= LANES and nb % SUBLANES == 0 and lw == WINDOW + tq and s >= lw
    assert s % ts == 0 and s % (2 * tk) == 0 and tk % tq == 0 and tq % CMP_BLOCK == 0
    assert nb * CMP_BLOCK == s
    assert z_off % gw == 0 and g_off % LANES == 0
    n_i = s // ts
    pps = _page_job_split(b * groups * n_i, page_table)
    kern = functools.partial(_nsa_prompt_kernel, tq=tq, nh=nh, tk=tk, seq=s, nb=nb, rep=rep, lw=lw,
                             pps=pps)
    db, n_pages = page_table.shape
    n_rows, bpp = cache5.shape[3], PAGE_SIZE // CMP_BLOCK
    spp = n_pages // pps if pps else 0
    assert b * groups * n_i * pps in (0, db * n_pages)
    flat = lambda bi, g, i: (bi * groups + g) * n_i + i

    def page_spec(r):
        def index_map(bi, g, i, pt):
            f = flat(bi, g, i)
            return (layer, pt[f // spp, (f % spp) * pps + r], 0, 0, 0)
        return pl.BlockSpec((None, None, PAGE_SIZE, n_rows, HEAD_DIM), index_map)

    kt_spec = pl.BlockSpec((HEAD_DIM, s), lambda bi, g, i, pt: (g, bi))
    v_spec = pl.BlockSpec((1, s, HEAD_DIM), lambda bi, g, i, pt: (bi, 0, g))
    mean_spec = lambda off: pl.BlockSpec((1, nb, HEAD_DIM), lambda bi, g, i, pt: (bi, 0, off + g))
    grid_spec = pltpu.PrefetchScalarGridSpec(
        num_scalar_prefetch=1,
        grid=(b, groups, n_i),
        in_specs=[
            pl.BlockSpec((1, ts, gw), lambda bi, g, i, pt: (bi, i, g)),
            kt_spec, v_spec, kt_spec, v_spec,
            mean_spec(0), mean_spec(groups),
            pl.BlockSpec((1, ts, LANES), lambda bi, g, i, pt: (bi, i, g_off // LANES + g)),
            pl.BlockSpec((1, ts, gw), lambda bi, g, i, pt: (bi, i, z_off // gw + g)),
            pl.BlockSpec(block_rows.shape, lambda bi, g, i, pt: (0, 0)),
            pl.BlockSpec(band.shape, lambda bi, g, i, pt: (0, 0, 0)),
            pl.BlockSpec(tri.shape, lambda bi, g, i, pt: (0, 0)),
            pl.BlockSpec(eye.shape, lambda bi, g, i, pt: (0, 0)),
        ] + [page_spec(r) for r in range(pps)],
        out_specs=[pl.BlockSpec((1, ts, gw), lambda bi, g, i, pt: (bi, i, g))] + ([
            pl.BlockSpec((1, n_rows, pps * bpp, HEAD_DIM),
                         lambda bi, g, i, pt: (flat(bi, g, i) // spp, 0, flat(bi, g, i) % spp, 0)),
        ] if pps else []),
        scratch_shapes=[pltpu.VMEM((nh * rep * tq, tk), F32), pltpu.VMEM((nh * rep * tq, tk), F32)],
    )
    outs = pl.pallas_call(
        kern,
        grid_spec=grid_spec,
        out_shape=[jax.ShapeDtypeStruct((b, s, cw), BF16)] + ([
            jax.ShapeDtypeStruct((db, n_rows, n_pages * bpp, HEAD_DIM), F32)] if pps else []),
        compiler_params=_cparams(56),
        name="nsa_prompt",
    )(page_table, q3, kt_sel, v_sel, kt_win, v_win, means3, means3, proj3, proj3, block_rows, band, tri,
      eye, *([cache5] * pps))
    return outs if pps else (outs[0], _page_means(cache5, page_table, layer))


def _store_page_means(page_refs, out_ref):
    bpp = PAGE_SIZE // CMP_BLOCK
    n_rows = out_ref.shape[1]
    for r, page_ref in enumerate(page_refs):
        for h in range(bpp):
            mean = jnp.mean(page_ref[h * CMP_BLOCK:(h + 1) * CMP_BLOCK], axis=0)
            n = r * bpp + h
            for j in range(n_rows):
                out_ref[0, j, n:n + 1, :] = mean[j:j + 1, :]


def _page_means_kernel(pt_ref, *refs, pps):
    _store_page_means(refs[:pps], refs[pps])


def _page_means(cache5, page_table, layer, *, pps=8):
    db, n_pages = page_table.shape
    n_rows, d = cache5.shape[3], cache5.shape[4]
    pps = pps if n_pages % pps == 0 else 1
    bpp = PAGE_SIZE // CMP_BLOCK

    def page_spec(r):
        return pl.BlockSpec((None, None, PAGE_SIZE, n_rows, d),
                            lambda bi, j, pt: (layer, pt[bi, j * pps + r], 0, 0, 0))

    grid_spec = pltpu.PrefetchScalarGridSpec(
        num_scalar_prefetch=1,
        grid=(db, n_pages // pps),
        in_specs=[page_spec(r) for r in range(pps)],
        out_specs=pl.BlockSpec((1, n_rows, pps * bpp, d), lambda bi, j, pt: (bi, 0, j, 0)),
    )
    return pl.pallas_call(
        functools.partial(_page_means_kernel, pps=pps),
        grid_spec=grid_spec,
        out_shape=jax.ShapeDtypeStruct((db, n_rows, n_pages * bpp, d), F32),
        compiler_params=_cparams(24),
        name="page_means",
    )(page_table, *([cache5] * pps))


def _nsa_sample_select_kernel(q_ref, mean_ref, oc_ref, idx_ref, *, nbp, groups, rep, q_pos, n_past_sel):
    q = q_ref[0]
    heads = groups * rep
    lane = lax.broadcasted_iota(I32, (1, nbp), 1)
    ri = lax.broadcasted_iota(I32, (nbp, nbp), 0)
    ci = lax.broadcasted_iota(I32, (nbp, nbp), 1)
    diag = ri == ci
    ok = ((lane + 1) * CMP_BLOCK - 1) <= q_pos
    cur = q_pos // CMP_BLOCK
    forced = (lane == 0) | (lane == cur) | (lane == cur - 1)
    hrow = lax.broadcasted_iota(I32, (heads, 1), 0) // rep
    out_lane = lax.broadcasted_iota(I32, (1, LANES), 1)
    o_c = jnp.zeros((heads, HEAD_DIM), F32)
    for g in range(groups):
        kc = mean_ref[0, g].astype(BF16)
        vc = mean_ref[0, groups + g].astype(BF16)
        s = _dot_nt(q, kc) + jnp.where(ok, 0.0, NEG_INF)
        e = jnp.exp2(s - jnp.max(s, axis=-1, keepdims=True))
        p = e / jnp.sum(e, axis=-1, keepdims=True) * jnp.where(ok, 1.0, 0.0)
        in_g = hrow == g
        o_c = o_c + jnp.where(in_g, jnp.dot(p.astype(BF16), vc, preferred_element_type=F32), 0.0)
        imp = jnp.sum(jnp.where(in_g, p, 0.0), axis=0, keepdims=True)
        imp = jnp.where(lane <= cur, jnp.where(forced, SEL_FORCE, imp), -1.0)
        imp_b = jnp.broadcast_to(imp, (nbp, nbp))
        imp_col = jnp.sum(jnp.where(diag, imp_b, 0.0), axis=1, keepdims=True)
        beats = (imp_col > imp_b) | ((imp_col == imp_b) & (ri < ci))
        cnt = jnp.sum(jnp.where(beats, 1.0, 0.0), axis=0, keepdims=True)
        sel = jnp.where((cnt < n_past_sel) & (imp >= 0.0), 1.0, 0.0)
        sel_col = jnp.sum(jnp.where(diag, jnp.broadcast_to(sel, (nbp, nbp)), 0.0), axis=1, keepdims=True)
        before = jnp.sum(jnp.where(ri < ci, sel_col, 0.0), axis=0, keepdims=True)
        row = jnp.zeros((1, LANES), F32)
        for k in range(n_past_sel):
            hit = (sel > 0.5) & (before == float(k))
            idx_k = jnp.sum(jnp.where(hit, lane.astype(F32), 0.0), axis=1, keepdims=True)
            row = jnp.where(out_lane == k, idx_k, row)
        idx_ref[0, g:g + 1, :] = row.astype(I32)
    oc_ref[0] = o_c


def _nsa_sample_select(q3, means, *, groups, rep, q_pos, n_past_sel):
    db, heads, _ = q3.shape
    nbp = means.shape[2]
    kern = functools.partial(_nsa_sample_select_kernel, nbp=nbp, groups=groups, rep=rep, q_pos=q_pos,
                             n_past_sel=n_past_sel)
    return pl.pallas_call(
        kern,
        grid=(db,),
        in_specs=[
            pl.BlockSpec((1, heads, HEAD_DIM), lambda bi: (bi, 0, 0)),
            pl.BlockSpec((1,) + means.shape[1:], lambda bi: (bi, 0, 0, 0)),
        ],
        out_specs=[
            pl.BlockSpec((1, heads, HEAD_DIM), lambda bi: (bi, 0, 0)),
            pl.BlockSpec((1, groups, LANES), lambda bi: (bi, 0, 0)),
        ],
        out_shape=[
            jax.ShapeDtypeStruct((db, heads, HEAD_DIM), F32),
            jax.ShapeDtypeStruct((db, groups, LANES), I32),
        ],
        compiler_params=_cparams(24),
        name="nsa_sample_select",
    )(q3, means)


def _nsa_sample_attend_kernel(pt_ref, si_ref, q_ref, *refs, groups, rep, n_steps, per_step, n_buf):
    kv_refs = refs[:groups * per_step]
    (newsel_ref, newwin_ref, oc_ref, win_ref, g_ref, z_ref,
     act_ref, winout_ref, m_ref, l_ref, acc_ref) = refs[groups * per_step:]
    kvw = groups * HEAD_DIM
    heads = groups * rep
    n_rows = 2 * groups
    k_id = pl.program_id(1)
    q = q_ref[0]
    qf = q.astype(F32)
    hgrp = lax.broadcasted_iota(I32, (heads, 1), 0) // rep

    def own_key_rows(n_tok):
        lane = lax.broadcasted_iota(I32, (1, n_tok * n_rows), 1)
        return lane % n_rows == hgrp, lane // n_rows

    def by_group(fn):
        out = None
        for g in range(groups):
            val = jnp.where(hgrp == g, fn(g), 0.0)
            out = val if out is None else out + val
        return out

    def new_token_scores(row_ref):
        return by_group(lambda g: jnp.sum(
            qf * row_ref[0, :, g * HEAD_DIM:(g + 1) * HEAD_DIM], axis=-1, keepdims=True))

    def new_token_values(row_ref):
        return by_group(lambda g: jnp.broadcast_to(
            row_ref[0, :, kvw + g * HEAD_DIM:kvw + (g + 1) * HEAD_DIM], (heads, HEAD_DIM)))

    @pl.when(k_id == 0)
    def _():
        m_ref[...] = jnp.full((heads, 1), NEG_INF, F32)
        l_ref[...] = jnp.zeros((heads, 1), F32)
        acc_ref[...] = jnp.zeros((heads, HEAD_DIM), F32)

    xs = [jnp.concatenate([kv_refs[g * per_step + u][...].reshape(CMP_BLOCK * n_rows, HEAD_DIM)
                           for u in range(per_step)], axis=0).astype(BF16) for g in range(groups)]
    own, _ = own_key_rows(CMP_BLOCK * per_step)
    s = by_group(lambda g: _dot_nt(q, xs[g])) + jnp.where(own, 0.0, NEG_INF)
    m_p = m_ref[...]
    m_n = jnp.maximum(m_p, jnp.max(s, axis=-1, keepdims=True))
    alpha = jnp.exp2(m_p - m_n)
    p = jnp.exp2(s - m_n)
    pb = pltpu.roll(p, groups, 1).astype(BF16)
    pv = by_group(lambda g: jnp.dot(pb, xs[g], preferred_element_type=F32))
    m_ref[...] = m_n
    l_ref[...] = alpha * l_ref[...] + jnp.sum(p, axis=-1, keepdims=True)
    acc_ref[...] = alpha * acc_ref[...] + pv

    @pl.when(k_id == n_steps - 1)
    def _():
        s_n = new_token_scores(newsel_ref)
        m_p = m_ref[...]
        m_n = jnp.maximum(m_p, s_n)
        alpha = jnp.exp2(m_p - m_n)
        p_n = jnp.exp2(s_n - m_n)
        l_s = alpha * l_ref[...] + p_n
        o_s = (alpha * acc_ref[...] + p_n * new_token_values(newsel_ref)) / l_s

        xw = win_ref[0].reshape(n_buf * n_rows, HEAD_DIM).astype(BF16)
        own_w, tok_w = own_key_rows(n_buf)
        ok_w = own_w & ((n_buf - tok_w) < WINDOW)
        s_w = _dot_nt(q, xw) + jnp.where(ok_w, 0.0, NEG_INF)
        s_wn = new_token_scores(newwin_ref)
        m_w = jnp.maximum(jnp.max(s_w, axis=-1, keepdims=True), s_wn)
        p_w = jnp.exp2(s_w - m_w)
        p_wn = jnp.exp2(s_wn - m_w)
        l_w = jnp.sum(p_w, axis=-1, keepdims=True) + p_wn
        o_w = jnp.dot(pltpu.roll(p_w, groups, 1).astype(BF16), xw, preferred_element_type=F32)
        o_w = (o_w + p_wn * new_token_values(newwin_ref)) / l_w

        winout_ref[0, 0:n_buf - 1] = win_ref[0, 1:n_buf]
        for j in range(n_rows):
            winout_ref[0, n_buf - 1, j:j + 1, :] = newwin_ref[0, :, j * HEAD_DIM:(j + 1) * HEAD_DIM]

        graw = jnp.broadcast_to(g_ref[0], (heads, groups * LANES))
        glane = lax.broadcasted_iota(I32, (heads, groups * LANES), 1)
        hidx = lax.broadcasted_iota(I32, (heads, 1), 0)
        gbase = (hidx // rep) * LANES + 3 * (hidx % rep)
        gate = lambda c: _sigmoid(jnp.sum(jnp.where(glane == gbase + c, graw, 0.0), axis=-1, keepdims=True))
        o = gate(0) * oc_ref[0] + gate(1) * o_s + gate(2) * o_w
        for h in range(heads):
            z = z_ref[0, :, h * HEAD_DIM:(h + 1) * HEAD_DIM]
            act_ref[0, :, h * HEAD_DIM:(h + 1) * HEAD_DIM] = (o[h:h + 1, :] * _silu(z)).astype(BF16)


def _nsa_sample_attend(page_table, sel_idx, q3, cache5, layer, new_sel, new_win, o_c, win_state, gates3,
                       z3, *, groups, rep, n_gather):
    db, heads, _ = q3.shape
    kvw = groups * HEAD_DIM
    n_buf = win_state.shape[2]
    n_rows = 2 * groups
    bpp = PAGE_SIZE // CMP_BLOCK
    assert sel_idx.shape[2] == n_gather
    per_step = max(u for u in range(1, 6) if n_gather % u == 0)
    n_steps = n_gather // per_step

    def gather_spec(g, u):
        def index_map(bi, k, pt, si):
            blk = si[bi, g, k * per_step + u]
            return (layer, pt[bi, blk // bpp], blk % bpp, 0, 0)
        return pl.BlockSpec((None, None, CMP_BLOCK, n_rows, HEAD_DIM), index_map)

    in_specs = [pl.BlockSpec((1, heads, HEAD_DIM), lambda bi, k, pt, si: (bi, 0, 0))]
    in_specs += [gather_spec(g, u) for g in range(groups) for u in range(per_step)]
    row3 = lambda w: pl.BlockSpec((1, 1, w), lambda bi, k, pt, si: (bi, 0, 0))
    in_specs += [
        row3(2 * kvw), row3(2 * kvw),
        pl.BlockSpec((1, heads, HEAD_DIM), lambda bi, k, pt, si: (bi, 0, 0)),
        pl.BlockSpec((None, 1, n_buf, n_rows, HEAD_DIM), lambda bi, k, pt, si: (layer, bi, 0, 0, 0)),
        row3(groups * LANES), row3(heads * HEAD_DIM),
    ]
    grid_spec = pltpu.PrefetchScalarGridSpec(
        num_scalar_prefetch=2,
        grid=(db, n_steps),
        in_specs=in_specs,
        out_specs=[
            row3(heads * HEAD_DIM),
            pl.BlockSpec((1, n_buf, n_rows, HEAD_DIM), lambda bi, k, pt, si: (bi, 0, 0, 0)),
        ],
        scratch_shapes=[pltpu.VMEM((heads, 1), F32), pltpu.VMEM((heads, 1), F32),
                        pltpu.VMEM((heads, HEAD_DIM), F32)],
    )
    kern = functools.partial(_nsa_sample_attend_kernel, groups=groups, rep=rep, n_steps=n_steps,
                             per_step=per_step, n_buf=n_buf)
    return pl.pallas_call(
        kern,
        grid_spec=grid_spec,
        out_shape=[
            jax.ShapeDtypeStruct((db, 1, heads * HEAD_DIM), BF16),
            jax.ShapeDtypeStruct((db, n_buf, n_rows, HEAD_DIM), F32),
        ],
        compiler_params=_cparams(32, ("arbitrary", "arbitrary")),
        name="nsa_sample_attend",
    )(page_table, sel_idx, q3, *([cache5] * (groups * per_step)), new_sel, new_win, o_c, win_state,
      gates3, z3)


def _rope_tables(pos):
    half = ROT_DIM // 2
    inv_freq = ROPE_THETA ** (-jnp.arange(half, dtype=F32) * (2.0 / ROT_DIM))
    ang = pos.astype(F32)[:, None] * inv_freq[None, :]
    cos, sin = jnp.cos(ang), jnp.sin(ang)
    n = pos.shape[0]
    cos_t = jnp.concatenate([cos, cos, jnp.ones((n, HEAD_DIM - ROT_DIM), F32)], axis=1)
    sin_t = jnp.concatenate([-sin, sin, jnp.zeros((n, HEAD_DIM - ROT_DIM), F32)], axis=1)
    return cos_t, sin_t


def _odd_weight_layout(w, cw, kvw, rep):
    n_layers, k, _ = w.shape
    groups = kvw // HEAD_DIM
    n_gate = 3 * groups * rep
    gates = w[:, :, :n_gate].reshape(n_layers, k, groups, 3 * rep)
    gates = jnp.pad(gates, ((0, 0), (0, 0), (0, 0), (0, LANES - 3 * rep))).reshape(n_layers, k, groups * LANES)
    return jnp.concatenate([w[:, :, n_gate:], gates], axis=2)


def kernel(x_prompt, x_sample, cache_cmp_kv, cache_sel_kv, state_win_kv, state_pool, page_table,
           norm_even, w_in_even, v_norm, w_spatial, b_spatial, w_pool, pool_scale, w_out_even,
           norm_odd, w_in_odd, q_norm, k_norm, w_out_odd):
    bsz, seq, d_model = x_prompt.shape
    db, dec_t, _ = x_sample.shape
    assert dec_t == 1
    n_even, n_odd = norm_even.shape[0], norm_odd.shape[0]
    depth = n_even + n_odd
    n_pages = page_table.shape[1]
    past_len = n_pages * PAGE_SIZE
    aw = v_norm.shape[1]
    bw = pool_scale.shape[1]
    groups = C_KV_HEADS
    kvw = groups * HEAD_DIM
    cw = w_out_odd.shape[1]
    rep = cw // kvw
    heads = cw // HEAD_DIM
    n_phys = cache_cmp_kv.shape[1]
    n_buf = state_win_kv.shape[2]
    assert past_len % CMP_BLOCK == 0 and seq % CMP_BLOCK == 0
    nb_p = seq // CMP_BLOCK
    nb_past = past_len // CMP_BLOCK
    n_past_sel = min(N_SEL, nb_past + 1) - 1
    assert nb_past >= 2 and n_past_sel >= 2

    xp = x_prompt.reshape(bsz * seq, d_model)
    xs = x_sample.reshape(db, d_model)

    cos_p, sin_p = _rope_tables(jnp.arange(seq, dtype=I32))
    cos_s, sin_s = _rope_tables(jnp.full((db,), past_len, dtype=I32))
    tables = _mask_tables(seq, LANES, WINDOW + LANES, rep)

    cache_cmp5 = cache_cmp_kv.reshape(n_odd, n_phys, PAGE_SIZE, 2 * groups, HEAD_DIM)
    cache_sel5 = cache_sel_kv.reshape(n_odd, n_phys, PAGE_SIZE, 2 * groups, HEAD_DIM)
    win_state5 = state_win_kv.reshape(n_odd, db, n_buf, 2 * groups, HEAD_DIM)

    tn_in = 2560
    n_even_in = w_in_even.shape[2]
    n_qkv = cw + 6 * kvw
    assert n_even_in % tn_in == 0 and n_qkv % tn_in == 0
    w_in_even_b = w_in_even.astype(BF16)
    w_out_even_b = w_out_even.astype(BF16)
    w_in_odd_b = w_in_odd.astype(BF16)
    w_out_odd_b = w_out_odd.astype(BF16)
    w_zg_b = _odd_weight_layout(w_in_odd_b[:, :, n_qkv:], cw, kvw, rep)
    n_zg = w_zg_b.shape[2]

    cmp_p, cmp_s, sel_p, sel_s, win_p, win_s = [], [], [], [], [], []
    pool_p, pool_s, gv_s = [], [], []
    for layer in range(depth):
        li = layer // 2
        if layer % 2 == 0:
            g_in = norm_even[li][None, :]
            vg = v_norm[li][None, :]
            ps = pool_scale[li][None, :]
            wp = w_pool[li].astype(BF16)
            proj = _norm_matmul(xp, g_in, w_in_even_b, li, n_even_in, tn=tn_in)
            y3, pool16 = _even_mix(proj.reshape(bsz, seq, -1), xp.reshape(bsz, seq, d_model), vg,
                                   w_spatial[li], b_spatial[li].T, wp, ps, w_out_even_b, li)
            xp = y3.reshape(bsz * seq, d_model)
            pool_p.append(pool16[:, 16 - POOL_PAD:])
            proj_s = _norm_matmul(xs, g_in, w_in_even_b, li, n_even_in, tn=tn_in)
            hd = aw // A_HEADS
            w00 = jnp.repeat(w_spatial[li][:, 0, 0], hd)[None, :]
            b0 = jnp.repeat(b_spatial[li][:, 0], hd)[None, :]
            act_s, vn_s, new_state = _even_mix_sample(
                proj_s, vg, w00, b0, wp, ps, jnp.swapaxes(state_pool[li], 0, 1), past_len)
            xs = _matmul_residual(act_s, w_out_even_b, li, xs)
            pool_s.append(jnp.swapaxes(new_state, 0, 1))
            gv_s.append(vn_s.reshape(db, 1, aw))
        else:
            g_in = norm_odd[li][None, :]
            qg = q_norm[li][None, :]
            kg = k_norm[li]
            proj = _norm_matmul(xp, g_in, w_in_odd_b, li, n_qkv, tn=tn_in)
            proj_zg = _norm_matmul(xp, g_in, w_zg_b, li, n_zg, tn=n_zg)
            q_b, kv_cmp, kv_sel, kv_win, kt_sel, v_sel, kt_win, v_win, means = _odd_post(
                proj, cos_p, sin_p, qg, kg, cw=cw, kvw=kvw, for_prompt=True, win_keep=min(WINDOW, seq))
            r3 = lambda a: a.reshape(bsz, seq, a.shape[-1])
            act, means_s = _nsa_prompt(r3(q_b), kt_sel, r3(v_sel), kt_win, r3(v_win),
                                       means.reshape(bsz, nb_p, 2 * kvw), r3(proj_zg), tables,
                                       cache_cmp5, page_table, li, cw=cw, z_off=0, g_off=cw)
            xp = _matmul_residual(act.reshape(bsz * seq, cw), w_out_odd_b, li, xp)
            kv6 = lambda a, n: a.reshape(n, -1, 2, groups, HEAD_DIM)
            cmp_p.append(kv6(kv_cmp, bsz))
            sel_p.append(kv6(kv_sel, bsz))
            win_p.append(kv6(kv_win, bsz))
            proj_s = _norm_matmul(xs, g_in, w_in_odd_b, li, n_qkv, tn=tn_in)
            proj_zg_s = _norm_matmul(xs, g_in, w_zg_b, li, n_zg, tn=n_zg)
            q_s, kvc_s, kvs_s, kvw_s = _odd_post(
                proj_s, cos_s, sin_s, qg, kg, cw=cw, kvw=kvw, for_prompt=False)
            q3s = q_s.reshape(db, heads, HEAD_DIM)
            o_c, idx = _nsa_sample_select(q3s, means_s, groups=groups, rep=rep, q_pos=past_len,
                                          n_past_sel=n_past_sel)
            act_s, win_new = _nsa_sample_attend(
                page_table, idx[:, :, :n_past_sel], q3s, cache_sel5, li,
                kvs_s.reshape(db, 1, 2 * kvw), kvw_s.reshape(db, 1, 2 * kvw), o_c, win_state5,
                proj_zg_s[:, cw:].reshape(db, 1, groups * LANES),
                proj_zg_s[:, :cw].reshape(db, 1, cw),
                groups=groups, rep=rep, n_gather=n_past_sel)
            xs = _matmul_residual(act_s.reshape(db, cw), w_out_odd_b, li, xs)
            cmp_s.append(kv6(kvc_s, db))
            sel_s.append(kv6(kvs_s, db))
            win_s.append(win_new.reshape(db, n_buf, 2, groups, HEAD_DIM))
    return (xp.reshape(bsz, seq, d_model), xs.reshape(db, 1, d_model),
            jnp.stack(cmp_p), jnp.stack(cmp_s), jnp.stack(sel_p), jnp.stack(sel_s),
            jnp.stack(win_p), jnp.stack(win_s), jnp.stack(pool_p), jnp.stack(pool_s), jnp.stack(gv_s))
```
